```python
import math
import jax, jax.numpy as jnp
from jax import lax
import numpy as np

D_MODEL = 1024
BATCH = 8
SEQ = 2048
DEPTH = 2

POOL_WINDOWS = (2, 4, 8, 16)
N_POOL_GROUPS = len(POOL_WINDOWS)
POOL_GROUP_DIM = D_MODEL // N_POOL_GROUPS
HEAD_DIM = 64
N_HEADS = D_MODEL // HEAD_DIM
Q_BLOCK = 128
N_EXPERTS = 32
TOP_K = 4
D_FF = D_MODEL
SWIGLU_LIMIT = 7.0
SWIGLU_ALPHA = 1.702
RMS_EPS = 1e-6
N_MIXERS = 2
N_POOL_LAYERS = (DEPTH + 1) // 2
N_SB_LAYERS = DEPTH // 2

kernel_name = "hybrid_pool_stickbreak_moe"


def rmsnorm(x, g):
    xf = x.astype(jnp.float32)
    y = xf * lax.rsqrt(jnp.mean(xf * xf, axis=-1, keepdims=True) + RMS_EPS)
    return (y * g.astype(jnp.float32)).astype(x.dtype)


def pool_mixer(h, w_groups, scale):
    B, S, D = h.shape
    hf = h.astype(jnp.float32)
    c = jnp.cumsum(hf, axis=1)
    pos = jnp.arange(S)
    outs = []
    for g, w in enumerate(POOL_WINDOWS):
        cg = c[..., g * POOL_GROUP_DIM:(g + 1) * POOL_GROUP_DIM]
        lag = jnp.pad(cg, ((0, 0), (w, 0), (0, 0)))[:, :S]
        cnt = jnp.minimum(pos + 1, w).astype(jnp.float32)[None, :, None]
        outs.append((cg - lag) / cnt)
    pooled = jnp.concatenate(outs, axis=-1) - hf
    pooled = pooled.reshape(B, S, N_POOL_GROUPS, POOL_GROUP_DIM).astype(h.dtype)
    y = jnp.einsum('bsgc,gcd->bsgd', pooled, w_groups).reshape(B, S, D)
    return y * scale


def stick_breaking_attention(h, w_qkv, q_gain, k_gain, w_o):
    B, S, D = h.shape
    qkv = h @ w_qkv
    q, k, v = jnp.split(qkv, 3, axis=-1)
    q = rmsnorm(q.reshape(B, S, N_HEADS, HEAD_DIM), q_gain).transpose(0, 2, 1, 3)
    k = rmsnorm(k.reshape(B, S, N_HEADS, HEAD_DIM), k_gain).transpose(0, 2, 1, 3)
    v = v.reshape(B, S, N_HEADS, HEAD_DIM).transpose(0, 2, 1, 3)
    kf = k.astype(jnp.float32)
    n_blk = S // Q_BLOCK
    q_blocks = jnp.moveaxis(q.reshape(B, N_HEADS, n_blk, Q_BLOCK, HEAD_DIM), 2, 0)
    starts = jnp.arange(n_blk, dtype=jnp.int32) * Q_BLOCK
    inv_sqrt_d = 1.0 / math.sqrt(HEAD_DIM)
    kpos = jnp.arange(S, dtype=jnp.int32)

    def block(args):
        qb, q0 = args
        z = jnp.einsum('bhqd,bhkd->bhqk', qb.astype(jnp.float32), kf) * inv_sqrt_d
        qpos = q0 + jnp.arange(Q_BLOCK, dtype=jnp.int32)
        causal = kpos[None, :] < qpos[:, None]
        log_beta = jax.nn.log_sigmoid(z)
        log_keep = jnp.where(causal, jax.nn.log_sigmoid(-z), 0.0)
        suffix = lax.cumsum(log_keep, axis=3, reverse=True) - log_keep
        a = jnp.where(causal, jnp.exp(log_beta + suffix), 0.0)
        return jnp.einsum('bhqk,bhkd->bhqd', a.astype(v.dtype), v)

    o = lax.map(block, (q_blocks, starts))
    o = jnp.moveaxis(o, 0, 2).reshape(B, N_HEADS, S, HEAD_DIM)
    o = o.transpose(0, 2, 1, 3).reshape(B, S, D)
    return o @ w_o


def moe(h, router_w, router_b, w_gate_up, b_gate_up, w_down, b_down):
    B, S, D = h.shape
    t = h.reshape(B * S, D)
    logits = (t @ router_w).astype(jnp.float32) + router_b.astype(jnp.float32)
    top_val, top_idx = lax.top_k(logits, TOP_K)
    top_w = jax.nn.softmax(top_val, axis=-1)
    gates = jnp.sum(jax.nn.one_hot(top_idx, N_EXPERTS, dtype=jnp.float32) * top_w[..., None], axis=1)
    y = jnp.zeros((B * S, D), jnp.float32)
    for e in range(N_EXPERTS):
        gu = t @ w_gate_up[e] + b_gate_up[e]
        gate, up = gu[:, :D_FF], gu[:, D_FF:]
        gate = jnp.minimum(gate, SWIGLU_LIMIT)
        up = jnp.clip(up, -SWIGLU_LIMIT, SWIGLU_LIMIT)
        act = (up + 1) * (gate * jax.nn.sigmoid(SWIGLU_ALPHA * gate))
        out = act @ w_down[e] + b_down[e]
        y = y + gates[:, e:e + 1] * out.astype(jnp.float32)
    return y.astype(h.dtype).reshape(B, S, D)


def setup_inputs(seed: int = 0) -> dict:
    key = jax.random.key(seed)
    ks = jax.random.split(key, 16)
    f32 = jnp.float32
    D, F, E = D_MODEL, D_FF, N_EXPERTS
    nrm = lambda k, shape, s: jax.random.normal(k, shape, f32) * s
    return {
        "x": nrm(ks[0], (BATCH, SEQ, D), 1.0),
        "mix_norm": 1.0 + nrm(ks[1], (DEPTH, D), 0.02),
        "pool_w": nrm(ks[2], (N_POOL_LAYERS, N_POOL_GROUPS, POOL_GROUP_DIM, POOL_GROUP_DIM), POOL_GROUP_DIM ** -0.5),
        "pool_scale": 1.0 + nrm(ks[3], (N_POOL_LAYERS, D), 0.02),
        "w_qkv": nrm(ks[4], (N_SB_LAYERS, D, 3 * D), D ** -0.5),
        "q_norm": 1.0 + nrm(ks[5], (N_SB_LAYERS, HEAD_DIM), 0.02),
        "k_norm": 1.0 + nrm(ks[6], (N_SB_LAYERS, HEAD_DIM), 0.02),
        "w_o": nrm(ks[7], (N_SB_LAYERS, D, D), D ** -0.5),
        "ffn_norm": 1.0 + nrm(ks[8], (DEPTH, D), 0.02),
        "router_w": nrm(ks[9], (DEPTH, D, E), D ** -0.5),
        "router_b": nrm(ks[10], (DEPTH, E), 0.01),
        "w_gate_up": nrm(ks[11], (DEPTH, E, D, 2 * F), D ** -0.5),
        "b_gate_up": nrm(ks[12], (DEPTH, E, 2 * F), 0.01),
        "w_down": nrm(ks[13], (DEPTH, E, F, D), F ** -0.5),
        "b_down": nrm(ks[14], (DEPTH, E, D), 0.01),
    }


def reference(x, mix_norm, pool_w, pool_scale, w_qkv, q_norm, k_norm, w_o,
              ffn_norm, router_w, router_b, w_gate_up, b_gate_up, w_down, b_down):
    for i in range(DEPTH):
        h = rmsnorm(x, mix_norm[i])
        j = i // N_MIXERS
        if i % N_MIXERS == 0:
            x = x + pool_mixer(h, pool_w[j], pool_scale[j])
        else:
            x = x + stick_breaking_attention(h, w_qkv[j], q_norm[j], k_norm[j], w_o[j])
        h = rmsnorm(x, ffn_norm[i])
        x = x + moe(h, router_w[i], router_b[i], w_gate_up[i], b_gate_up[i], w_down[i], b_down[i])
    return x
```

```python
import functools

import jax
import jax.numpy as jnp
from jax import lax
from jax.experimental import pallas as pl
from jax.experimental.pallas import tpu as pltpu

D_MODEL = 1024
POOL_WINDOWS = (2, 4, 8, 16)
POOL_GROUP_DIM = D_MODEL // len(POOL_WINDOWS)
POOL_HALO = 16
HEAD_DIM = 64
N_HEADS = D_MODEL // HEAD_DIM
N_EXPERTS = 32
TOP_K = 4
D_FF = D_MODEL
SWIGLU_LIMIT = 7.0
SWIGLU_ALPHA = 1.702
RMS_EPS = 1e-6

LANES = 128
SUBLANES = 8
VMEM_LIMIT = 56 * 1024 * 1024

POOL_TILE = 512
CHUNK = 256
SEG_ALIGN = SUBLANES
CHUNK_ROWS = CHUNK * TOP_K + N_EXPERTS * SEG_ALIGN
GEMM_TILE = 256
TOK_TILE = 512
ATT_BLOCK = 128

F32 = jnp.float32
BF16 = jnp.bfloat16


def _dot(a, b):
    return jnp.dot(a, b, preferred_element_type=F32)


def _dot_nt(a, b):
    return lax.dot_general(a, b, (((1,), (1,)), ((), ())), preferred_element_type=F32)


def _split_bf16(v):
    hi = v.astype(BF16)
    lo = (v - hi.astype(F32)).astype(BF16)
    return hi, lo


def _rmsnorm(v, g):
    return v * lax.rsqrt(jnp.mean(v * v, axis=-1, keepdims=True) + RMS_EPS) * g


def _pool_kernel(x_ref, halo_ref, g_ref, w_ref, scale_ref, o_ref):
    i = pl.program_id(1)
    x = x_ref[0]
    g = g_ref[...]
    h = _rmsnorm(x, g)
    hh = _rmsnorm(halo_ref[0], g)
    hh = jnp.where(i > 0, hh, 0.0)
    a = jnp.concatenate([hh, h], axis=0)
    ts = x.shape[0]
    pos = i * ts + lax.broadcasted_iota(jnp.int32, (ts, 1), 0)
    outs = []
    for gi, w in enumerate(POOL_WINDOWS):
        lo, hi = gi * POOL_GROUP_DIM, (gi + 1) * POOL_GROUP_DIM
        s = a[:, lo:hi]
        span = 1
        while span < w:
            s = s + pltpu.roll(s, span, axis=0)
            span *= 2
        cnt = jnp.minimum(pos + 1, w).astype(F32)
        pooled = s[POOL_HALO:, :] / cnt - h[:, lo:hi]
        outs.append(_dot(pooled.astype(BF16), w_ref[gi]))
    y = jnp.concatenate(outs, axis=-1) * scale_ref[...]
    o_ref[0] = x + y


def _pool_layer(x, g, w, scale):
    b, s, d = x.shape
    ts = min(POOL_TILE, s)
    blocks_per_tile = ts // POOL_HALO
    return pl.pallas_call(
        _pool_kernel,
        grid=(b, s // ts),
        in_specs=[
            pl.BlockSpec((1, ts, d), lambda bi, i: (bi, i, 0)),
            pl.BlockSpec((1, POOL_HALO, d), lambda bi, i: (bi, jnp.maximum(i * blocks_per_tile - 1, 0), 0)),
            pl.BlockSpec((1, d), lambda bi, i: (0, 0)),
            pl.BlockSpec((len(POOL_WINDOWS), POOL_GROUP_DIM, POOL_GROUP_DIM), lambda bi, i: (0, 0, 0)),
            pl.BlockSpec((1, d), lambda bi, i: (0, 0)),
        ],
        out_specs=pl.BlockSpec((1, ts, d), lambda bi, i: (bi, i, 0)),
        out_shape=jax.ShapeDtypeStruct(x.shape, F32),
        compiler_params=pltpu.CompilerParams(
            dimension_semantics=("arbitrary", "arbitrary"), vmem_limit_bytes=VMEM_LIMIT),
        name="pool_mixer",
    )(x, x, g.reshape(1, d), w.astype(BF16), scale.reshape(1, d))


def _route_kernel(x_ref, g_ref, wt_hi_ref, wt_lo_ref, b_ref, hb_ref, slot_ref, gate_ref, cnt_ref):
    c = x_ref.shape[0]
    h = _rmsnorm(x_ref[...], g_ref[...])
    h_hi, h_lo = _split_bf16(h)
    hb_ref[...] = h_hi
    logits = (_dot_nt(wt_hi_ref[...], h_hi) + _dot_nt(wt_lo_ref[...], h_hi)
              + _dot_nt(wt_hi_ref[...], h_lo) + b_ref[...])
    eio = lax.broadcasted_iota(jnp.int32, (N_EXPERTS, c), 0).astype(F32)
    vals, idxs = [], []
    l = logits
    for _ in range(TOP_K):
        m = jnp.max(l, axis=0, keepdims=True)
        idx = jnp.min(jnp.where(l == m, eio, float(N_EXPERTS)), axis=0, keepdims=True)
        vals.append(m)
        idxs.append(idx)
        l = jnp.where(eio == idx, -jnp.inf, l)
    es = [jnp.exp(v - vals[0]) for v in vals]
    denom = es[0] + es[1] + es[2] + es[3]
    sel = jnp.zeros((N_EXPERTS, c), F32)
    for idx in idxs:
        sel = sel + jnp.where(eio == idx, 1.0, 0.0)
    upper = jnp.where(lax.broadcasted_iota(jnp.int32, (c, c), 0) <= lax.broadcasted_iota(jnp.int32, (c, c), 1),
                      1.0, 0.0).astype(BF16)
    rank = _dot(sel.astype(BF16), upper)
    n = jnp.broadcast_to(rank[:, c - 1:c], (N_EXPERTS, c))
    n_units = jnp.floor((n + (SEG_ALIGN - 1)) * (1.0 / SEG_ALIGN))
    strict_lower = jnp.where(lax.broadcasted_iota(jnp.int32, (N_EXPERTS, N_EXPERTS), 1)
                             < lax.broadcasted_iota(jnp.int32, (N_EXPERTS, N_EXPERTS), 0), 1.0, 0.0).astype(BF16)
    off = _dot(strict_lower, n_units.astype(BF16)) * SEG_ALIGN
    dest = off + rank - 1.0
    for k in range(TOP_K):
        slot = jnp.sum(jnp.where(eio == idxs[k], dest, 0.0), axis=0, keepdims=True)
        slot_ref[0, k:k + 1, :] = slot.astype(jnp.int32)
        gate_ref[0, k:k + 1, :] = es[k] / denom
    slot_ref[0, TOP_K:, :] = jnp.zeros((SUBLANES - TOP_K, c), jnp.int32)
    gate_ref[0, TOP_K:, :] = jnp.zeros((SUBLANES - TOP_K, c), F32)
    cnt_ref[0] = (n_units[:, :LANES] * SEG_ALIGN).astype(jnp.int32)


def _route(x2d, g, router_w, router_b):
    t, d = x2d.shape
    nc = t // CHUNK
    wt_hi, wt_lo = _split_bf16(router_w.T)
    return pl.pallas_call(
        _route_kernel,
        grid=(nc,),
        in_specs=[
            pl.BlockSpec((CHUNK, d), lambda c: (c, 0)),
            pl.BlockSpec((1, d), lambda c: (0, 0)),
            pl.BlockSpec((N_EXPERTS, d), lambda c: (0, 0)),
            pl.BlockSpec((N_EXPERTS, d), lambda c: (0, 0)),
            pl.BlockSpec((N_EXPERTS, 1), lambda c: (0, 0)),
        ],
        out_specs=[
            pl.BlockSpec((CHUNK, d), lambda c: (c, 0)),
            pl.BlockSpec((1, SUBLANES, CHUNK), lambda c: (c, 0, 0)),
            pl.BlockSpec((1, SUBLANES, CHUNK), lambda c: (c, 0, 0)),
            pl.BlockSpec((1, N_EXPERTS, LANES), lambda c: (c, 0, 0)),
        ],
        out_shape=[
            jax.ShapeDtypeStruct((t, d), BF16),
            jax.ShapeDtypeStruct((nc, SUBLANES, CHUNK), jnp.int32),
            jax.ShapeDtypeStruct((nc, SUBLANES, CHUNK), F32),
            jax.ShapeDtypeStruct((nc, N_EXPERTS, LANES), jnp.int32),
        ],
        compiler_params=pltpu.CompilerParams(dimension_semantics=("arbitrary",), vmem_limit_bytes=VMEM_LIMIT),
        name="moe_route",
    )(x2d, g.reshape(1, d), wt_hi, wt_lo, router_b.reshape(N_EXPERTS, 1))


def _segment_tables(cnt, n_tiles_max):
    seg_n = cnt[:, :, 0]
    seg_loc = jnp.cumsum(seg_n, axis=1) - seg_n
    tot = jnp.sum(seg_n, axis=0)
    tiles = (tot + GEMM_TILE - 1) // GEMM_TILE
    tile_end = jnp.cumsum(tiles)
    tile_start = tile_end - tiles
    seg_dst = tile_start[None, :] * GEMM_TILE + jnp.cumsum(seg_n, axis=0) - seg_n
    used = tile_end[-1]
    tile_ids = jnp.arange(n_tiles_max, dtype=jnp.int32)
    tile_expert = jnp.sum(tile_ids[:, None] >= tile_end[None, :], axis=1).astype(jnp.int32)
    last_expert = jnp.sum(jnp.maximum(used - 1, 0) >= tile_end).astype(jnp.int32)
    tile_expert = jnp.where(tile_ids < used, tile_expert, last_expert)
    i32 = lambda v: v.astype(jnp.int32).reshape(-1)
    gap = jnp.concatenate([tile_start * GEMM_TILE + tot, tiles * GEMM_TILE - tot, used[None]])
    return i32(seg_loc), i32(seg_n), i32(seg_dst), tile_expert, i32(used), i32(gap)


def _segment_copy(loc_ref, n_ref, dst_ref, c, e, local_buf, sorted_hbm, sem, to_sorted):
    j = c * N_EXPERTS + e
    n = pl.multiple_of(n_ref[j], SEG_ALIGN)
    local =local_buf.at[pl.ds(pl.multiple_of(loc_ref[j], SEG_ALIGN), n)]
    remote = sorted_hbm.at[pl.ds(pl.multiple_of(dst_ref[j], SEG_ALIGN), n)]
    return n, (pltpu.make_async_copy(local, remote, sem) if to_sorted
               else pltpu.make_async_copy(remote, local, sem))


def _for_each_segment(loc_ref, n_ref, dst_ref, c, local_buf, sorted_hbm, sem, to_sorted, action):
    def body(e, carry):
        n, cp = _segment_copy(loc_ref, n_ref, dst_ref, c, e, local_buf, sorted_hbm, sem, to_sorted)

        @pl.when(n > 0)
        def _():
            action(cp)
        return carry
    lax.fori_loop(0, N_EXPERTS, body, 0)


def _zero_unused_rows(gap_ref, xs_hbm, zeros, sem, action):
    def expert_gap(e, carry):
        n = pl.multiple_of(gap_ref[N_EXPERTS + e], SEG_ALIGN)

        @pl.when(n > 0)
        def _():
            action(pltpu.make_async_copy(
                zeros.at[pl.ds(0, n)], xs_hbm.at[pl.ds(pl.multiple_of(gap_ref[e], SEG_ALIGN), n)], sem))
        return carry
    lax.fori_loop(0, N_EXPERTS, expert_gap, 0)

    def unused_tile(i, carry):
        action(pltpu.make_async_copy(zeros, xs_hbm.at[pl.ds(pl.multiple_of(i * GEMM_TILE, GEMM_TILE), GEMM_TILE)], sem))
        return carry
    lax.fori_loop(gap_ref[2 * N_EXPERTS], xs_hbm.shape[0] // GEMM_TILE, unused_tile, 0)


def _dispatch_kernel(loc_ref, n_ref, dst_ref, gap_ref, hb_ref, slot_ref, xs_hbm, buf, zeros, sems):
    c = pl.program_id(0)
    nc = pl.num_programs(0)
    par = c % 2
    start = lambda cp: cp.start()
    wait = lambda cp: cp.wait()

    @pl.when(c == 0)
    def _():
        zeros[...] = jnp.zeros(zeros.shape, F32)
        _zero_unused_rows(gap_ref, xs_hbm, zeros, sems.at[2], start)

    for p in range(2):
        @pl.when(jnp.logical_and(par == p, c >= 2))
        def _():
            _for_each_segment(loc_ref, n_ref, dst_ref, c - 2, buf.at[p], xs_hbm, sems.at[p], True, wait)

    rows = lax.broadcasted_iota(jnp.int32, (CHUNK_ROWS, CHUNK), 0)
    hit = rows == slot_ref[0, 0:1, :]
    for k in range(1, TOP_K):
        hit = jnp.logical_or(hit, rows == slot_ref[0, k:k + 1, :])
    onehot = jnp.where(hit, 1.0, 0.0).astype(BF16)
    compact = _dot(onehot, hb_ref[...])

    for p in range(2):
        @pl.when(par == p)
        def _():
            buf[p] = compact
            _for_each_segment(loc_ref, n_ref, dst_ref, c, buf.at[p], xs_hbm, sems.at[p], True, start)

    @pl.when(c == nc - 1)
    def _():
        _zero_unused_rows(gap_ref, xs_hbm, zeros, sems.at[2], wait)
        for p in range(2):
            @pl.when(jnp.logical_and(par != p, nc >= 2))
            def _():
                _for_each_segment(loc_ref, n_ref, dst_ref, c - 1, buf.at[p], xs_hbm, sems.at[p], True, wait)

            @pl.when(par == p)
            def _():
                _for_each_segment(loc_ref, n_ref, dst_ref, c, buf.at[p], xs_hbm, sems.at[p], True, wait)


def _dispatch(hb, slots, seg_loc, seg_n, seg_dst, gap, n_rows):
    t, d = hb.shape
    nc = t // CHUNK
    return pl.pallas_call(
        _dispatch_kernel,
        grid_spec=pltpu.PrefetchScalarGridSpec(
            num_scalar_prefetch=4,
            grid=(nc,),
            in_specs=[
                pl.BlockSpec((CHUNK, d), lambda c, *_: (c, 0)),
                pl.BlockSpec((1, SUBLANES, CHUNK), lambda c, *_: (c, 0, 0)),
            ],
            out_specs=pl.BlockSpec(memory_space=pl.ANY),
            scratch_shapes=[pltpu.VMEM((2, CHUNK_ROWS, d), F32), pltpu.VMEM((GEMM_TILE, d), F32),
                            pltpu.SemaphoreType.DMA((3,))],
        ),
        out_shape=jax.ShapeDtypeStruct((n_rows, d), F32),
        compiler_params=pltpu.CompilerParams(dimension_semantics=("arbitrary",), vmem_limit_bytes=VMEM_LIMIT),
        name="moe_dispatch",
    )(seg_loc, seg_n, seg_dst, gap, hb, slots)


def _expert_kernel(te_ref, used_ref, x_ref, wgu_ref, bgu_ref, wd_ref, bd_ref, o_ref, wgu_bf, wd_bf):
    i = pl.program_id(0)
    prev = te_ref[jnp.maximum(i - 1, 0)]

    @pl.when(jnp.logical_or(i == 0, te_ref[i] != prev))
    def _():
        wgu_bf[...] = wgu_ref[0].astype(BF16)
        wd_bf[...] = wd_ref[0].astype(BF16)

    @pl.when(i < used_ref[0])
    def _():
        gu = _dot(x_ref[...].astype(BF16), wgu_bf[...]) + bgu_ref[0]
        gate = jnp.minimum(gu[:, :D_FF], SWIGLU_LIMIT)
        up = jnp.clip(gu[:, D_FF:], -SWIGLU_LIMIT, SWIGLU_LIMIT)
        act = (up + 1.0) * (gate * (1.0 / (1.0 + jnp.exp(-SWIGLU_ALPHA * gate))))
        o_ref[...] = _dot(act.astype(BF16), wd_bf[...]) + bd_ref[0]

    @pl.when(i >= used_ref[0])
    def _():
        o_ref[...] = jnp.zeros(o_ref.shape, F32)


def _experts(xs, tile_expert, used, w_gate_up, b_gate_up, w_down, b_down):
    n_rows, d = xs.shape
    n_tiles = n_rows // GEMM_TILE
    row_map = lambda i, te, u: (jnp.minimum(i, jnp.maximum(u[0] - 1, 0)), 0)
    exp_map = lambda i, te, u: (te[i], 0, 0)
    return pl.pallas_call(
        _expert_kernel,
        grid_spec=pltpu.PrefetchScalarGridSpec(
            num_scalar_prefetch=2,
            grid=(n_tiles,),
            in_specs=[
                pl.BlockSpec((GEMM_TILE, d), row_map),
                pl.BlockSpec((1, d, 2 * D_FF), exp_map),
                pl.BlockSpec((1, 1, 2 * D_FF), exp_map),
                pl.BlockSpec((1, D_FF, d), exp_map),
                pl.BlockSpec((1, 1, d), exp_map),
            ],
            out_specs=pl.BlockSpec((GEMM_TILE, d), lambda i, te, u: (i, 0)),
            scratch_shapes=[pltpu.VMEM((d, 2 * D_FF), BF16), pltpu.VMEM((D_FF, d), BF16)],
        ),
        out_shape=jax.ShapeDtypeStruct((n_rows, d), F32),
        compiler_params=pltpu.CompilerParams(dimension_semantics=("arbitrary",), vmem_limit_bytes=VMEM_LIMIT),
        name="moe_experts",
    )(tile_expert, used, xs, w_gate_up, b_gate_up.reshape(N_EXPERTS, 1, 2 * D_FF),
      w_down, b_down.reshape(N_EXPERTS, 1, d))


def _combine_kernel(loc_ref, n_ref, dst_ref, x_ref, slot_ref, gate_ref, slot_t_ref, ys_hbm, o_ref, buf, sems):
    c = pl.program_id(0)
    nc = pl.num_programs(0)
    par = c % 2
    start = lambda cp: cp.start()
    wait = lambda cp: cp.wait()

    @pl.when(c == 0)
    def _():
        buf[...] = jnp.zeros(buf.shape, F32)
        _for_each_segment(loc_ref, n_ref, dst_ref, c, buf.at[0], ys_hbm, sems.at[0], False, start)

    for p in range(2):
        @pl.when(jnp.logical_and(par == p, c + 1 < nc))
        def _():
            _for_each_segment(loc_ref, n_ref, dst_ref, c + 1, buf.at[1 - p], ys_hbm, sems.at[1 - p], False, start)

    rows = lax.broadcasted_iota(jnp.int32, (CHUNK_ROWS, CHUNK), 0)
    wsum = jnp.zeros((CHUNK_ROWS, CHUNK), F32)
    for k in range(TOP_K):
        wsum = wsum + jnp.where(rows == slot_ref[0, k:k + 1, :], gate_ref[0, k:k + 1, :], 0.0)
    row_gate = jnp.sum(wsum, axis=1, keepdims=True)
    cols = lax.broadcasted_iota(jnp.int32, (CHUNK, CHUNK_ROWS), 1)
    st = slot_t_ref[0]
    hit = cols == st[:, 0:1]
    for k in range(1, TOP_K):
        hit = jnp.logical_or(hit, cols == st[:, k:k + 1])
    back = jnp.where(hit, 1.0, 0.0).astype(BF16)

    for p in range(2):
        @pl.when(par == p)
        def _():
            _for_each_segment(loc_ref, n_ref, dst_ref, c, buf.at[p], ys_hbm, sems.at[p], False, wait)
            hi, lo = _split_bf16(buf[p] * row_gate)
            o_ref[...] = x_ref[...] + (_dot(back, hi) + _dot(back, lo))


def _combine(x2d, ys, slots, gates, slots_t, seg_loc, seg_n, seg_dst):
    t, d = x2d.shape
    nc = t // CHUNK
    return pl.pallas_call(
        _combine_kernel,
        grid_spec=pltpu.PrefetchScalarGridSpec(
            num_scalar_prefetch=3,
            grid=(nc,),
            in_specs=[
                pl.BlockSpec((CHUNK, d), lambda c, *_: (c, 0)),
                pl.BlockSpec((1, SUBLANES, CHUNK), lambda c, *_: (c, 0, 0)),
                pl.BlockSpec((1, SUBLANES, CHUNK), lambda c, *_: (c, 0, 0)),
                pl.BlockSpec((1, CHUNK, TOP_K), lambda c, *_: (c, 0, 0)),
                pl.BlockSpec(memory_space=pl.ANY),
            ],
            out_specs=pl.BlockSpec((CHUNK, d), lambda c, *_: (c, 0)),
            scratch_shapes=[pltpu.VMEM((2, CHUNK_ROWS, d), F32), pltpu.SemaphoreType.DMA((2,))],
        ),
        out_shape=jax.ShapeDtypeStruct((t, d), F32),
        compiler_params=pltpu.CompilerParams(dimension_semantics=("arbitrary",), vmem_limit_bytes=VMEM_LIMIT),
        name="moe_combine",
    )(seg_loc, seg_n, seg_dst, x2d, slots, gates, slots_t, ys)


def _moe_layer(x2d, g, router_w, router_b, w_gate_up, b_gate_up, w_down, b_down):
    t, d = x2d.shape
    nc = t // CHUNK
    n_tiles_max = (t * TOP_K + nc * N_EXPERTS * (SEG_ALIGN - 1)) // GEMM_TILE + N_EXPERTS
    hb, slots, gates, cnt = _route(x2d, g, router_w, router_b)
    seg_loc, seg_n, seg_dst, tile_expert, used, gap = _segment_tables(cnt, n_tiles_max)
    xs = _dispatch(hb, slots, seg_loc, seg_n, seg_dst, gap, n_tiles_max * GEMM_TILE)
    ys = _experts(xs, tile_expert, used, w_gate_up, b_gate_up, w_down, b_down)
    slots_t = jnp.swapaxes(slots[:, :TOP_K, :], 1, 2)
    return _combine(x2d, ys, slots, gates, slots_t, seg_loc, seg_n, seg_dst)


def _qkv_kernel(x_ref, g_ref, w_ref, qg_ref, kg_ref, q_ref, k_ref, v_ref):
    d = x_ref.shape[1]
    h = _rmsnorm(x_ref[...], g_ref[...]).astype(BF16)
    qkv = _dot(h, w_ref[...])
    li = lax.broadcasted_iota(jnp.int32, (LANES, LANES), 0) // HEAD_DIM
    lj = lax.broadcasted_iota(jnp.int32, (LANES, LANES), 1) // HEAD_DIM
    same_head = jnp.where(li == lj, 1.0, 0.0).astype(BF16)

    def head_norm(v, gain, out_scale):
        parts = []
        for j in range(d // LANES):
            vj = v[:, j * LANES:(j + 1) * LANES]
            hi, lo = _split_bf16(vj * vj)
            ss = _dot(hi, same_head) + _dot(lo, same_head)
            parts.append(vj * lax.rsqrt(ss * (1.0 / HEAD_DIM) + RMS_EPS))
        return (jnp.concatenate(parts, axis=-1) * gain * out_scale).astype(BF16)

    q_ref[...] = head_norm(qkv[:, :d], qg_ref[...], HEAD_DIM ** -0.5)
    k_ref[...] = head_norm(qkv[:, d:2 * d], kg_ref[...], 1.0)
    v_ref[...] = qkv[:, 2 * d:].astype(BF16)


def _qkv(x2d, g, w_qkv, q_gain, k_gain):
    t, d = x2d.shape
    tt = min(TOK_TILE, t)
    row = lambda i: (i, 0)
    fixed = lambda i: (0, 0)
    return pl.pallas_call(
        _qkv_kernel,
        grid=(t // tt,),
        in_specs=[
            pl.BlockSpec((tt, d), row),
            pl.BlockSpec((1, d), fixed),
            pl.BlockSpec((d, 3 * d), fixed),
            pl.BlockSpec((1, d), fixed),
            pl.BlockSpec((1, d), fixed),
        ],
        out_specs=[pl.BlockSpec((tt, d), row)] * 3,
        out_shape=[jax.ShapeDtypeStruct((t, d), BF16)] * 3,
        compiler_params=pltpu.CompilerParams(dimension_semantics=("arbitrary",), vmem_limit_bytes=VMEM_LIMIT),
        name="attn_qkv",
    )(x2d, g.reshape(1, d), w_qkv.astype(BF16), jnp.tile(q_gain, N_HEADS).reshape(1, d),
      jnp.tile(k_gain, N_HEADS).reshape(1, d))


def _attn_kernel(q_ref, k_ref, v_ref, o_ref):
    i = pl.program_id(2)
    blk = ATT_BLOCK
    q = q_ref[0]
    lane = lax.broadcasted_iota(jnp.int32, (blk, LANES), 1)
    head_of_lane = lane // HEAD_DIM
    q_heads = [jnp.where(head_of_lane == hd, q, jnp.zeros_like(q)) for hd in range(2)]
    ki = lax.broadcasted_iota(jnp.int32, (blk, blk), 0)
    kj = lax.broadcasted_iota(jnp.int32, (blk, blk), 1)
    suffix = jnp.where(ki >= kj, 1.0, 0.0).astype(BF16)
    suffix2 = jnp.concatenate([suffix, suffix], axis=0)
    causal = kj < ki

    def block(j, q_h, acc, carry, diagonal):
        k_blk = k_ref[0, pl.ds(pl.multiple_of(j * blk, blk), blk), :]
        v_blk = v_ref[0, pl.ds(pl.multiple_of(j * blk, blk), blk), :]
        z = _dot_nt(q_h, k_blk)
        log_keep = jnp.minimum(-z, 0.0) - jnp.log(1.0 + jnp.exp(-jnp.abs(z)))
        if diagonal:
            log_keep = jnp.where(causal, log_keep, 0.0)
        hi, lo = _split_bf16(log_keep)
        within = _dot(jnp.concatenate([hi, lo], axis=1), suffix2)
        a = jnp.exp(z + within + carry)
        if diagonal:
            a = jnp.where(causal, a, 0.0)
        acc = acc + _dot(a.astype(BF16), v_blk)
        carry = carry + within[:, 0:1]
        return acc, carry

    outs = []
    for hd in range(2):
        acc0 = jnp.zeros((blk, LANES), F32)
        carry0 = jnp.zeros((blk, 1), F32)
        acc, carry = block(i, q_heads[hd], acc0, carry0, True)

        def body(s, ac, q_h=q_heads[hd]):
            return block(i - 1 - s, q_h, ac[0], ac[1], False)
        acc, carry = lax.fori_loop(0, i, body, (acc, carry))
        outs.append(acc)
    o_ref[0] = jnp.where(head_of_lane == 0, outs[0], outs[1]).astype(BF16)


def _attention(q, k, v, b, s):
    d = q.shape[-1]
    q3, k3, v3 = (a.reshape(b, s, d) for a in (q, k, v))
    n_blk = s // ATT_BLOCK
    pairs = d // LANES
    o = pl.pallas_call(
        _attn_kernel,
        grid=(b, pairs, n_blk),
        in_specs=[
            pl.BlockSpec((1, ATT_BLOCK, LANES), lambda bi, hp, i: (bi, i, hp)),
            pl.BlockSpec((1, s, LANES), lambda bi, hp, i: (bi, 0, hp)),
            pl.BlockSpec((1, s, LANES), lambda bi, hp, i: (bi, 0, hp)),
        ],
        out_specs=pl.BlockSpec((1, ATT_BLOCK, LANES), lambda bi, hp, i: (bi, i, hp)),
        out_shape=jax.ShapeDtypeStruct((b, s, d), BF16),
        compiler_params=pltpu.CompilerParams(
            dimension_semantics=("arbitrary", "arbitrary", "arbitrary"), vmem_limit_bytes=VMEM_LIMIT),
        name="attn_core",
    )(q3, k3, v3)
    return o.reshape(b * s, d)


def _out_proj_kernel(x_ref, o_ref, w_ref, y_ref):
    y_ref[...] = x_ref[...] + _dot(o_ref[...], w_ref[...])


def _out_proj(x2d, o, w_o):
    t, d = x2d.shape
    tt = min(TOK_TILE, t)
    return pl.pallas_call(
        _out_proj_kernel,
        grid=(t // tt,),
        in_specs=[
            pl.BlockSpec((tt, d), lambda i: (i, 0)),
            pl.BlockSpec((tt, d), lambda i: (i, 0)),
            pl.BlockSpec((d, d), lambda i: (0, 0)),
        ],
        out_specs=pl.BlockSpec((tt, d), lambda i: (i, 0)),
        out_shape=jax.ShapeDtypeStruct((t, d), F32),
        compiler_params=pltpu.CompilerParams(dimension_semantics=("arbitrary",), vmem_limit_bytes=VMEM_LIMIT),
        name="attn_out_proj",
    )(x2d, o, w_o.astype(BF16))


def _attention_layer(x2d, b, s, g, w_qkv, q_gain, k_gain, w_o):
    q, k, v = _qkv(x2d, g, w_qkv, q_gain, k_gain)
    o = _attention(q, k, v, b, s)
    return _out_proj(x2d, o, w_o)


def kernel(x, mix_norm, pool_w, pool_scale, w_qkv, q_norm, k_norm, w_o, ffn_norm, router_w, router_b,
           w_gate_up, b_gate_up, w_down, b_down):
    b, s, d = x.shape
    assert d == D_MODEL and s % ATT_BLOCK == 0 and (b * s) % CHUNK == 0
    depth = mix_norm.shape[0]
    for i in range(depth):
        j = i // 2
        if i % 2 == 0:
            x2d = _pool_layer(x, mix_norm[i], pool_w[j], pool_scale[j]).reshape(b * s, d)
        else:
            x2d = _attention_layer(x.reshape(b * s, d), b, s, mix_norm[i], w_qkv[j], q_norm[j], k_norm[j], w_o[j])
        x2d = _moe_layer(x2d, ffn_norm[i], router_w[i], router_b[i], w_gate_up[i], b_gate_up[i],
                         w_down[i], b_down[i])
        x = x2d.reshape(b, s, d)
    return x
```

```python
import functools

import jax
import jax.numpy as jnp
from jax import lax
from jax.experimental import pallas as pl
from jax.experimental.pallas import tpu as pltpu

D_MODEL = 1024
POOL_WINDOWS = (2, 4, 8, 16)
POOL_GROUP_DIM = D_MODEL // len(POOL_WINDOWS)
POOL_HALO = 16
HEAD_DIM = 64
N_HEADS = D_MODEL // HEAD_DIM
N_EXPERTS = 32
TOP_K = 4
D_FF = D_MODEL
SWIGLU_LIMIT = 7.0
SWIGLU_ALPHA = 1.702
RMS_EPS = 1e-6

LANES = 128
SUBLANES = 8
VMEM_LIMIT = 56 * 1024 * 1024

POOL_TILE = 512
CHUNK = 256
SEG_ALIGN = SUBLANES
CHUNK_ROWS = CHUNK * TOP_K + N_EXPERTS * SEG_ALIGN
GEMM_TILE = 256
TOK_TILE = 512
ATT_Q = 512
ATT_K = LANES
ATT_GROUP = 2

F32 = jnp.float32
BF16 = jnp.bfloat16


def _dot(a, b):
    return jnp.dot(a, b, preferred_element_type=F32)


def _dot_nt(a, b):
    return lax.dot_general(a, b, (((1,), (1,)), ((), ())), preferred_element_type=F32)


def _split_bf16(v):
    hi = v.astype(BF16)
    lo = (v - hi.astype(F32)).astype(BF16)
    return hi, lo


def _rmsnorm(v, g):
    return v * lax.rsqrt(jnp.mean(v * v, axis=-1, keepdims=True) + RMS_EPS) * g


def _pool_kernel(x_ref, halo_ref, g_ref, w_ref, scale_ref, o_ref):
    i = pl.program_id(1)
    x = x_ref[0]
    g = g_ref[...]
    h = _rmsnorm(x, g)
    hh = _rmsnorm(halo_ref[0], g)
    hh = jnp.where(i > 0, hh, 0.0)
    a = jnp.concatenate([hh, h], axis=0)
    ts = x.shape[0]
    pos = i * ts + lax.broadcasted_iota(jnp.int32, (ts, 1), 0)
    outs = []
    for gi, w in enumerate(POOL_WINDOWS):
        lo, hi = gi * POOL_GROUP_DIM, (gi + 1) * POOL_GROUP_DIM
        s = a[:, lo:hi]
        span = 1
        while span < w:
            s = s + pltpu.roll(s, span, axis=0)
            span *= 2
        cnt = jnp.minimum(pos + 1, w).astype(F32)
        pooled = s[POOL_HALO:, :] / cnt - h[:, lo:hi]
        outs.append(_dot(pooled.astype(BF16), w_ref[gi]))
    y = jnp.concatenate(outs, axis=-1) * scale_ref[...]
    o_ref[0] = x + y


def _pool_layer(x, g, w, scale):
    b, s, d = x.shape
    ts = min(POOL_TILE, s)
    blocks_per_tile = ts // POOL_HALO
    return pl.pallas_call(
        _pool_kernel,
        grid=(b, s // ts),
        in_specs=[
            pl.BlockSpec((1, ts, d), lambda bi, i: (bi, i, 0)),
            pl.BlockSpec((1, POOL_HALO, d), lambda bi, i: (bi, jnp.maximum(i * blocks_per_tile - 1, 0), 0)),
            pl.BlockSpec((1, d), lambda bi, i: (0, 0)),
            pl.BlockSpec((len(POOL_WINDOWS), POOL_GROUP_DIM, POOL_GROUP_DIM), lambda bi, i: (0, 0, 0)),
            pl.BlockSpec((1, d), lambda bi, i: (0, 0)),
        ],
        out_specs=pl.BlockSpec((1, ts, d), lambda bi, i: (bi, i, 0)),
        out_shape=jax.ShapeDtypeStruct(x.shape, F32),
        compiler_params=pltpu.CompilerParams(
            dimension_semantics=("arbitrary", "arbitrary"), vmem_limit_bytes=VMEM_LIMIT),
        name="pool_mixer",
    )(x, x, g.reshape(1, d), w.astype(BF16), scale.reshape(1, d))


def _route_kernel(x_ref, g_ref, wt_hi_ref, wt_lo_ref, b_ref, hb_ref, slot_ref, gate_ref, cnt_ref):
    c = x_ref.shape[0]
    h = _rmsnorm(x_ref[...], g_ref[...])
    h_hi, h_lo = _split_bf16(h)
    hb_ref[...] = h_hi
    logits = (_dot_nt(wt_hi_ref[...], h_hi) + _dot_nt(wt_lo_ref[...], h_hi)
              + _dot_nt(wt_hi_ref[...], h_lo) + b_ref[...])
    eio = lax.broadcasted_iota(jnp.int32, (N_EXPERTS, c), 0).astype(F32)
    vals, idxs = [], []
    l = logits
    for _ in range(TOP_K):
        m = jnp.max(l, axis=0, keepdims=True)
        idx = jnp.min(jnp.where(l == m, eio, float(N_EXPERTS)), axis=0, keepdims=True)
        vals.append(m)
        idxs.append(idx)
        l = jnp.where(eio == idx, -jnp.inf, l)
    es = [jnp.exp(v - vals[0]) for v in vals]
    denom = es[0] + es[1] + es[2] + es[3]
    sel = jnp.zeros((N_EXPERTS, c), F32)
    for idx in idxs:
        sel = sel + jnp.where(eio == idx, 1.0, 0.0)
    upper = jnp.where(lax.broadcasted_iota(jnp.int32, (c, c), 0) <= lax.broadcasted_iota(jnp.int32, (c, c), 1),
                      1.0, 0.0).astype(BF16)
    rank = _dot(sel.astype(BF16), upper)
    n = jnp.broadcast_to(rank[:, c - 1:c], (N_EXPERTS, c))
    n_units = jnp.floor((n + (SEG_ALIGN - 1)) * (1.0 / SEG_ALIGN))
    strict_lower = jnp.where(lax.broadcasted_iota(jnp.int32, (N_EXPERTS, N_EXPERTS), 1)
                             < lax.broadcasted_iota(jnp.int32, (N_EXPERTS, N_EXPERTS), 0), 1.0, 0.0).astype(BF16)
    off = _dot(strict_lower, n_units.astype(BF16)) * SEG_ALIGN
    dest = off + rank - 1.0
    for k in range(TOP_K):
        slot = jnp.sum(jnp.where(eio == idxs[k], dest, 0.0), axis=0, keepdims=True)
        slot_ref[0, k:k + 1, :] = slot.astype(jnp.int32)
        gate_ref[0, k:k + 1, :] = es[k] / denom
    slot_ref[0, TOP_K:, :] = jnp.zeros((SUBLANES - TOP_K, c), jnp.int32)
    gate_ref[0, TOP_K:, :] = jnp.zeros((SUBLANES - TOP_K, c), F32)
    cnt_ref[0] = (n_units[:, :LANES] * SEG_ALIGN).astype(jnp.int32)


def _route(x2d, g, router_w, router_b):
    t, d = x2d.shape
    nc = t // CHUNK
    wt_hi, wt_lo = _split_bf16(router_w.T)
    return pl.pallas_call(
        _route_kernel,
        grid=(nc,),
        in_specs=[
            pl.BlockSpec((CHUNK, d), lambda c: (c, 0)),
            pl.BlockSpec((1, d), lambda c: (0, 0)),
            pl.BlockSpec((N_EXPERTS, d), lambda c: (0, 0)),
            pl.BlockSpec((N_EXPERTS, d), lambda c: (0, 0)),
            pl.BlockSpec((N_EXPERTS, 1), lambda c: (0, 0)),
        ],
        out_specs=[
            pl.BlockSpec((CHUNK, d), lambda c: (c, 0)),
            pl.BlockSpec((1, SUBLANES, CHUNK), lambda c: (c, 0, 0)),
            pl.BlockSpec((1, SUBLANES, CHUNK), lambda c: (c, 0, 0)),
            pl.BlockSpec((1, N_EXPERTS, LANES), lambda c: (c, 0, 0)),
        ],
        out_shape=[
            jax.ShapeDtypeStruct((t, d), BF16),
            jax.ShapeDtypeStruct((nc, SUBLANES, CHUNK), jnp.int32),
            jax.ShapeDtypeStruct((nc, SUBLANES, CHUNK), F32),
            jax.ShapeDtypeStruct((nc, N_EXPERTS, LANES), jnp.int32),
        ],
        compiler_params=pltpu.CompilerParams(dimension_semantics=("arbitrary",), vmem_limit_bytes=VMEM_LIMIT),
        name="moe_route",
    )(x2d, g.reshape(1, d), wt_hi, wt_lo, router_b.reshape(N_EXPERTS, 1))


def _segment_tables(cnt, n_tiles_max):
    seg_n = cnt[:, :, 0]
    seg_loc = jnp.cumsum(seg_n, axis=1) - seg_n
    tot = jnp.sum(seg_n, axis=0)
    tiles = (tot + GEMM_TILE - 1) // GEMM_TILE
    tile_end = jnp.cumsum(tiles)
    tile_start = tile_end - tiles
    seg_dst = tile_start[None, :] * GEMM_TILE + jnp.cumsum(seg_n, axis=0) - seg_n
    used = tile_end[-1]
    tile_ids = jnp.arange(n_tiles_max, dtype=jnp.int32)
    tile_expert = jnp.sum(tile_ids[:, None] >= tile_end[None, :], axis=1).astype(jnp.int32)
    last_expert = jnp.sum(jnp.maximum(used - 1, 0) >= tile_end).astype(jnp.int32)
    tile_expert = jnp.where(tile_ids < used, tile_expert, last_expert)
    i32 = lambda v: v.astype(jnp.int32).reshape(-1)
    gap = jnp.concatenate([tile_start * GEMM_TILE + tot, tiles * GEMM_TILE - tot, used[None]])
    return i32(seg_loc), i32(seg_n), i32(seg_dst), tile_expert, i32(used), i32(gap)


def _segment_copy(loc_ref, n_ref, dst_ref, c, e, local_buf, sorted_hbm, sem, to_sorted):
    j = c * N_EXPERTS + e
    n = pl.multiple_of(n_ref[j], SEG_ALIGN)
    local =local_buf.at[pl.ds(pl.multiple_of(loc_ref[j], SEG_ALIGN), n)]
    remote = sorted_hbm.at[pl.ds(pl.multiple_of(dst_ref[j], SEG_ALIGN), n)]
    return n, (pltpu.make_async_copy(local, remote, sem) if to_sorted
               else pltpu.make_async_copy(remote, local, sem))


def _for_each_segment(loc_ref, n_ref, dst_ref, c, local_buf, sorted_hbm, sem, to_sorted, action):
    def body(e, carry):
        n, cp = _segment_copy(loc_ref, n_ref, dst_ref, c, e, local_buf, sorted_hbm, sem, to_sorted)

        @pl.when(n > 0)
        def _():
            action(cp)
        return carry
    lax.fori_loop(0, N_EXPERTS, body, 0)


def _zero_unused_rows(gap_ref, xs_hbm, zeros, sem, action):
    def expert_gap(e, carry):
        n = pl.multiple_of(gap_ref[N_EXPERTS + e], SEG_ALIGN)

        @pl.when(n > 0)
        def _():
            action(pltpu.make_async_copy(
                zeros.at[pl.ds(0, n)], xs_hbm.at[pl.ds(pl.multiple_of(gap_ref[e], SEG_ALIGN), n)], sem))
        return carry
    lax.fori_loop(0, N_EXPERTS, expert_gap, 0)

    def unused_tile(i, carry):
        action(pltpu.make_async_copy(zeros, xs_hbm.at[pl.ds(pl.multiple_of(i * GEMM_TILE, GEMM_TILE), GEMM_TILE)], sem))
        return carry
    lax.fori_loop(gap_ref[2 * N_EXPERTS], xs_hbm.shape[0] // GEMM_TILE, unused_tile, 0)


def _dispatch_kernel(loc_ref, n_ref, dst_ref, gap_ref, hb_ref, slot_ref, xs_hbm, buf, zeros, sems):
    c = pl.program_id(0)
    nc = pl.num_programs(0)
    par = c % 2
    start = lambda cp: cp.start()
    wait = lambda cp: cp.wait()

    @pl.when(c == 0)
    def _():
        zeros[...] = jnp.zeros(zeros.shape, F32)
        _zero_unused_rows(gap_ref, xs_hbm, zeros, sems.at[2], start)

    for p in range(2):
        @pl.when(jnp.logical_and(par == p, c >= 2))
        def _():
            _for_each_segment(loc_ref, n_ref, dst_ref, c - 2, buf.at[p], xs_hbm, sems.at[p], True, wait)

    rows = lax.broadcasted_iota(jnp.int32, (CHUNK_ROWS, CHUNK), 0)
    hit = rows == slot_ref[0, 0:1, :]
    for k in range(1, TOP_K):
        hit = jnp.logical_or(hit, rows == slot_ref[0, k:k + 1, :])
    onehot = jnp.where(hit, 1.0, 0.0).astype(BF16)
    compact = _dot(onehot, hb_ref[...])

    for p in range(2):
        @pl.when(par == p)
        def _():
            buf[p] = compact
            _for_each_segment(loc_ref, n_ref, dst_ref, c, buf.at[p], xs_hbm, sems.at[p], True, start)

    @pl.when(c == nc - 1)
    def _():
        _zero_unused_rows(gap_ref, xs_hbm, zeros, sems.at[2], wait)
        for p in range(2):
            @pl.when(jnp.logical_and(par != p, nc >= 2))
            def _():
                _for_each_segment(loc_ref, n_ref, dst_ref, c - 1, buf.at[p], xs_hbm, sems.at[p], True, wait)

            @pl.when(par == p)
            def _():
                _for_each_segment(loc_ref, n_ref, dst_ref, c, buf.at[p], xs_hbm, sems.at[p], True, wait)


def _dispatch(hb, slots, seg_loc, seg_n, seg_dst, gap, n_rows):
    t, d = hb.shape
    nc = t // CHUNK
    return pl.pallas_call(
        _dispatch_kernel,
        grid_spec=pltpu.PrefetchScalarGridSpec(
            num_scalar_prefetch=4,
            grid=(nc,),
            in_specs=[
                pl.BlockSpec((CHUNK, d), lambda c, *_: (c, 0)),
                pl.BlockSpec((1, SUBLANES, CHUNK), lambda c, *_: (c, 0, 0)),
            ],
            out_specs=pl.BlockSpec(memory_space=pl.ANY),
            scratch_shapes=[pltpu.VMEM((2, CHUNK_ROWS, d), F32), pltpu.VMEM((GEMM_TILE, d), F32),
                            pltpu.SemaphoreType.DMA((3,))],
        ),
        out_shape=jax.ShapeDtypeStruct((n_rows, d), F32),
        compiler_params=pltpu.CompilerParams(dimension_semantics=("arbitrary",), vmem_limit_bytes=VMEM_LIMIT),
        name="moe_dispatch",
    )(seg_loc, seg_n, seg_dst, gap, hb, slots)


def _expert_kernel(te_ref, used_ref, x_ref, wgu_ref, bgu_ref, wd_ref, bd_ref, o_ref, wgu_bf, wd_bf):
    i = pl.program_id(0)
    prev = te_ref[jnp.maximum(i - 1, 0)]

    @pl.when(jnp.logical_or(i == 0, te_ref[i] != prev))
    def _():
        wgu_bf[...] = wgu_ref[0].astype(BF16)
        wd_bf[...] = wd_ref[0].astype(BF16)

    @pl.when(i < used_ref[0])
    def _():
        gu = _dot(x_ref[...].astype(BF16), wgu_bf[...]) + bgu_ref[0]
        gate = jnp.minimum(gu[:, :D_FF], SWIGLU_LIMIT)
        up = jnp.clip(gu[:, D_FF:], -SWIGLU_LIMIT, SWIGLU_LIMIT)
        act = (up + 1.0) * (gate * (1.0 / (1.0 + jnp.exp(-SWIGLU_ALPHA * gate))))
        o_ref[...] = _dot(act.astype(BF16), wd_bf[...]) + bd_ref[0]

    @pl.when(i >= used_ref[0])
    def _():
        o_ref[...] = jnp.zeros(o_ref.shape, F32)


def _experts(xs, tile_expert, used, layer, w_gate_up, b_gate_up, w_down, b_down):
    n_rows, d = xs.shape
    n_tiles = n_rows // GEMM_TILE
    n_stacked = w_gate_up.shape[0] * N_EXPERTS
    row_map = lambda i, te, u: (jnp.minimum(i, jnp.maximum(u[0] - 1, 0)), 0)
    exp_map = lambda i, te, u: (layer * N_EXPERTS + te[i], 0, 0)
    return pl.pallas_call(
        _expert_kernel,
        grid_spec=pltpu.PrefetchScalarGridSpec(
            num_scalar_prefetch=2,
            grid=(n_tiles,),
            in_specs=[
                pl.BlockSpec((GEMM_TILE, d), row_map),
                pl.BlockSpec((1, d, 2 * D_FF), exp_map),
                pl.BlockSpec((1, 1, 2 * D_FF), exp_map),
                pl.BlockSpec((1, D_FF, d), exp_map),
                pl.BlockSpec((1, 1, d), exp_map),
            ],
            out_specs=pl.BlockSpec((GEMM_TILE, d), lambda i, te, u: (i, 0)),
            scratch_shapes=[pltpu.VMEM((d, 2 * D_FF), BF16), pltpu.VMEM((D_FF, d), BF16)],
        ),
        out_shape=jax.ShapeDtypeStruct((n_rows, d), F32),
        compiler_params=pltpu.CompilerParams(dimension_semantics=("arbitrary",), vmem_limit_bytes=VMEM_LIMIT),
        name="moe_experts",
    )(tile_expert, used, xs, w_gate_up.reshape(n_stacked, d, 2 * D_FF), b_gate_up.reshape(n_stacked, 1, 2 * D_FF),
      w_down.reshape(n_stacked, D_FF, d), b_down.reshape(n_stacked, 1, d))


def _combine_kernel(loc_ref, n_ref, dst_ref, x_ref, slot_ref, gate_ref, slot_t_ref, ys_hbm, o_ref, buf, sems):
    c = pl.program_id(0)
    nc = pl.num_programs(0)
    par = c % 2
    start = lambda cp: cp.start()
    wait = lambda cp: cp.wait()

    @pl.when(c == 0)
    def _():
        buf[...] = jnp.zeros(buf.shape, F32)
        _for_each_segment(loc_ref, n_ref, dst_ref, c, buf.at[0], ys_hbm, sems.at[0], False, start)

    for p in range(2):
        @pl.when(jnp.logical_and(par == p, c + 1 < nc))
        def _():
            _for_each_segment(loc_ref, n_ref, dst_ref, c + 1, buf.at[1 - p], ys_hbm, sems.at[1 - p], False, start)

    rows = lax.broadcasted_iota(jnp.int32, (CHUNK_ROWS, CHUNK), 0)
    wsum = jnp.zeros((CHUNK_ROWS, CHUNK), F32)
    for k in range(TOP_K):
        wsum = wsum + jnp.where(rows == slot_ref[0, k:k + 1, :], gate_ref[0, k:k + 1, :], 0.0)
    row_gate = jnp.sum(wsum, axis=1, keepdims=True)
    cols = lax.broadcasted_iota(jnp.int32, (CHUNK, CHUNK_ROWS), 1)
    st = slot_t_ref[0]
    hit = cols == st[:, 0:1]
    for k in range(1, TOP_K):
        hit = jnp.logical_or(hit, cols == st[:, k:k + 1])
    back = jnp.where(hit, 1.0, 0.0).astype(BF16)

    for p in range(2):
        @pl.when(par == p)
        def _():
            _for_each_segment(loc_ref, n_ref, dst_ref, c, buf.at[p], ys_hbm, sems.at[p], False, wait)
            hi, lo = _split_bf16(buf[p] * row_gate)
            o_ref[...] = x_ref[...] + (_dot(back, hi) + _dot(back, lo))


def _combine(x2d, ys, slots, gates, slots_t, seg_loc, seg_n, seg_dst):
    t, d = x2d.shape
    nc = t // CHUNK
    return pl.pallas_call(
        _combine_kernel,
        grid_spec=pltpu.PrefetchScalarGridSpec(
            num_scalar_prefetch=3,
            grid=(nc,),
            in_specs=[
                pl.BlockSpec((CHUNK, d), lambda c, *_: (c, 0)),
                pl.BlockSpec((1, SUBLANES, CHUNK), lambda c, *_: (c, 0, 0)),
                pl.BlockSpec((1, SUBLANES, CHUNK), lambda c, *_: (c, 0, 0)),
                pl.BlockSpec((1, CHUNK, TOP_K), lambda c, *_: (c, 0, 0)),
                pl.BlockSpec(memory_space=pl.ANY),
            ],
            out_specs=pl.BlockSpec((CHUNK, d), lambda c, *_: (c, 0)),
            scratch_shapes=[pltpu.VMEM((2, CHUNK_ROWS, d), F32), pltpu.SemaphoreType.DMA((2,))],
        ),
        out_shape=jax.ShapeDtypeStruct((t, d), F32),
        compiler_params=pltpu.CompilerParams(dimension_semantics=("arbitrary",), vmem_limit_bytes=VMEM_LIMIT),
        name="moe_combine",
    )(seg_loc, seg_n, seg_dst, x2d, slots, gates, slots_t, ys)


def _moe_layer(x2d, g, router_w, router_b, layer, w_gate_up, b_gate_up, w_down, b_down):
    t, d = x2d.shape
    nc = t // CHUNK
    n_tiles_max = (t * TOP_K + nc * N_EXPERTS * (SEG_ALIGN - 1)) // GEMM_TILE + N_EXPERTS
    hb, slots, gates, cnt = _route(x2d, g, router_w, router_b)
    seg_loc, seg_n, seg_dst, tile_expert, used, gap = _segment_tables(cnt, n_tiles_max)
    xs = _dispatch(hb, slots, seg_loc, seg_n, seg_dst, gap, n_tiles_max * GEMM_TILE)
    ys = _experts(xs, tile_expert, used, layer, w_gate_up, b_gate_up, w_down, b_down)
    slots_t = jnp.swapaxes(slots[:, :TOP_K, :], 1, 2)
    return _combine(x2d, ys, slots, gates, slots_t, seg_loc, seg_n, seg_dst)


def _qkv_kernel(x_ref, g_ref, w_ref, qg_ref, kg_ref, q_ref, k_ref, v_ref):
    d = x_ref.shape[1]
    h = _rmsnorm(x_ref[...], g_ref[...]).astype(BF16)
    qkv = _dot(h, w_ref[...])
    li = lax.broadcasted_iota(jnp.int32, (LANES, LANES), 0) // HEAD_DIM
    lj = lax.broadcasted_iota(jnp.int32, (LANES, LANES), 1) // HEAD_DIM
    same_head = jnp.where(li == lj, 1.0, 0.0).astype(BF16)

    def head_norm(v, gain, out_scale):
        parts = []
        for j in range(d // LANES):
            vj = v[:, j * LANES:(j + 1) * LANES]
            hi, lo = _split_bf16(vj * vj)
            ss = _dot(hi, same_head) + _dot(lo, same_head)
            parts.append(vj * lax.rsqrt(ss * (1.0 / HEAD_DIM) + RMS_EPS))
        return (jnp.concatenate(parts, axis=-1) * gain * out_scale).astype(BF16)

    q_ref[...] = head_norm(qkv[:, :d], qg_ref[...], HEAD_DIM ** -0.5)
    k_ref[...] = head_norm(qkv[:, d:2 * d], kg_ref[...], 1.0)
    v_ref[...] = qkv[:, 2 * d:].astype(BF16)


def _qkv(x2d, g, w_qkv, q_gain, k_gain):
    t, d = x2d.shape
    tt = min(TOK_TILE, t)
    row = lambda i: (i, 0)
    fixed = lambda i: (0, 0)
    return pl.pallas_call(
        _qkv_kernel,
        grid=(t // tt,),
        in_specs=[
            pl.BlockSpec((tt, d), row),
            pl.BlockSpec((1, d), fixed),
            pl.BlockSpec((d, 3 * d), fixed),
            pl.BlockSpec((1, d), fixed),
            pl.BlockSpec((1, d), fixed),
        ],
        out_specs=[pl.BlockSpec((tt, d), row)] * 3,
        out_shape=[jax.ShapeDtypeStruct((t, d), BF16)] * 3,
        compiler_params=pltpu.CompilerParams(dimension_semantics=("arbitrary",), vmem_limit_bytes=VMEM_LIMIT),
        name="attn_qkv",
    )(x2d, g.reshape(1, d), w_qkv.astype(BF16), jnp.tile(q_gain, N_HEADS).reshape(1, d),
      jnp.tile(k_gain, N_HEADS).reshape(1, d))


def _attn_kernel(q_ref, k_ref, v_ref, o_ref, acc_ref, carry_ref):
    qi = pl.program_id(2)
    tq, kb = ATT_Q, ATT_K
    q = q_ref[0]
    head_of_lane = lax.broadcasted_iota(jnp.int32, (tq, LANES), 1) // HEAD_DIM
    q_heads = [jnp.where(head_of_lane == hd, q, jnp.zeros_like(q)) for hd in range(2)]
    key_in = lax.broadcasted_iota(jnp.int32, (2 * kb, 2 * kb), 0) % kb
    col = lax.broadcasted_iota(jnp.int32, (2 * kb, 2 * kb), 1)
    suffix_and_total = jnp.where(jnp.logical_or(col >= kb, key_in >= col), 1.0, 0.0).astype(BF16)
    row = lax.broadcasted_iota(jnp.int32, (tq, kb), 0)
    key = lax.broadcasted_iota(jnp.int32, (tq, kb), 1)

    acc_ref[...] = jnp.zeros(acc_ref.shape, F32)
    carry_ref[...] = jnp.zeros(carry_ref.shape, F32)

    def key_group(j0, diag_offsets):
        nb = len(diag_offsets)
        start = pl.multiple_of(j0 * kb, kb)
        kk = k_ref[0, pl.ds(start, nb * kb), :]
        vv = v_ref[0, pl.ds(start, nb * kb), :]
        for hd in range(2):
            z_all = _dot_nt(q_heads[hd], kk)
            carry = carry_ref[hd]
            probs = [None] * nb
            for b in reversed(range(nb)):
                z = z_all[:, b * kb:(b + 1) * kb]
                log_keep = jnp.minimum(-z, 0.0) - jnp.log(1.0 + jnp.exp(-jnp.abs(z)))
                causal = None if diag_offsets[b] is None else key + diag_offsets[b] < row
                if causal is not None:
                    log_keep = jnp.where(causal, log_keep, 0.0)
                hi, lo = _split_bf16(log_keep)
                sums = _dot(jnp.concatenate([hi, lo], axis=1), suffix_and_total)
                a = jnp.exp(z + sums[:, :kb] + carry)
                if causal is not None:
                    a = jnp.where(causal, a, 0.0)
                probs[b] = a.astype(BF16)
                carry = carry + sums[:, kb:]
            carry_ref[hd] = carry
            acc_ref[hd] += _dot(jnp.concatenate(probs, axis=1), vv)

    blocks_per_q = tq // kb
    for g in reversed(range(blocks_per_q // ATT_GROUP)):
        offs = [(g * ATT_GROUP + b) * kb for b in range(ATT_GROUP)]
        key_group(qi * blocks_per_q + g * ATT_GROUP, offs)

    def body(s, c):
        key_group(qi * blocks_per_q - (s + 1) * ATT_GROUP, [None] * ATT_GROUP)
        return c
    lax.fori_loop(0, qi * (blocks_per_q // ATT_GROUP), body, 0)
    o_ref[0] = jnp.where(head_of_lane == 0, acc_ref[0], acc_ref[1]).astype(BF16)


def _attention(q, k, v, b, s):
    d = q.shape[-1]
    q3, k3, v3 = (a.reshape(b, s, d) for a in (q, k, v))
    pairs = d // LANES
    o = pl.pallas_call(
        _attn_kernel,
        grid=(b, pairs, s // ATT_Q),
        in_specs=[
            pl.BlockSpec((1, ATT_Q, LANES), lambda bi, hp, i: (bi, i, hp)),
            pl.BlockSpec((1, s, LANES), lambda bi, hp, i: (bi, 0, hp)),
            pl.BlockSpec((1, s, LANES), lambda bi, hp, i: (bi, 0, hp)),
        ],
        out_specs=pl.BlockSpec((1, ATT_Q, LANES), lambda bi, hp, i: (bi, i, hp)),
        out_shape=jax.ShapeDtypeStruct((b, s, d), BF16),
        scratch_shapes=[pltpu.VMEM((2, ATT_Q, LANES), F32), pltpu.VMEM((2, ATT_Q, LANES), F32)],
        compiler_params=pltpu.CompilerParams(
            dimension_semantics=("arbitrary", "arbitrary", "arbitrary"), vmem_limit_bytes=VMEM_LIMIT),
        name="attn_core",
    )(q3, k3, v3)
    return o.reshape(b * s, d)


def _out_proj_kernel(x_ref, o_ref, w_ref, y_ref):
    y_ref[...] = x_ref[...] + _dot(o_ref[...], w_ref[...])


def _out_proj(x2d, o, w_o):
    t, d = x2d.shape
    tt = min(TOK_TILE, t)
    return pl.pallas_call(
        _out_proj_kernel,
        grid=(t // tt,),
        in_specs=[
            pl.BlockSpec((tt, d), lambda i: (i, 0)),
            pl.BlockSpec((tt, d), lambda i: (i, 0)),
            pl.BlockSpec((d, d), lambda i: (0, 0)),
        ],
        out_specs=pl.BlockSpec((tt, d), lambda i: (i, 0)),
        out_shape=jax.ShapeDtypeStruct((t, d), F32),
        compiler_params=pltpu.CompilerParams(dimension_semantics=("arbitrary",), vmem_limit_bytes=VMEM_LIMIT),
        name="attn_out_proj",
    )(x2d, o, w_o.astype(BF16))


def _attention_layer(x2d, b, s, g, w_qkv, q_gain, k_gain, w_o):
    q, k, v = _qkv(x2d, g, w_qkv, q_gain, k_gain)
    o = _attention(q, k, v, b, s)
    return _out_proj(x2d, o, w_o)


def kernel(x, mix_norm, pool_w, pool_scale, w_qkv, q_norm, k_norm, w_o, ffn_norm, router_w, router_b,
           w_gate_up, b_gate_up, w_down, b_down):
    b, s, d = x.shape
    assert d == D_MODEL and s % ATT_Q == 0 and (b * s) % CHUNK == 0
    depth = mix_norm.shape[0]
    for i in range(depth):
        j = i // 2
        if i % 2 == 0:
            x2d = _pool_layer(x, mix_norm[i], pool_w[j], pool_scale[j]).reshape(b * s, d)
        else:
            x2d = _attention_layer(x.reshape(b * s, d), b, s, mix_norm[i], w_qkv[j], q_norm[j], k_norm[j], w_o[j])
        x2d = _moe_layer(x2d, ffn_norm[i], router_w[i], router_b[i], i, w_gate_up, b_gate_up, w_down, b_down)
        x = x2d.reshape(b, s, d)
    return x
```

```python
import jax
import jax.numpy as jnp
from jax import lax
from jax.experimental import pallas as pl
from jax.experimental.pallas import tpu as pltpu

D_MODEL = 1024
POOL_WINDOWS = (2, 4, 8, 16)
POOL_GROUP_DIM = D_MODEL // len(POOL_WINDOWS)
POOL_HALO = 16
HEAD_DIM = 64
N_HEADS = D_MODEL // HEAD_DIM
N_EXPERTS = 32
TOP_K = 4
D_FF = D_MODEL
SWIGLU_LIMIT = 7.0
SWIGLU_ALPHA = 1.702
RMS_EPS = 1e-6

LANES = 128
SUBLANES = 8
VMEM_LIMIT = 56 * 1024 * 1024

POOL_TILE = 512
CHUNK = 256
SEG_ALIGN = SUBLANES
CHUNK_ROWS = CHUNK * TOP_K + N_EXPERTS * SEG_ALIGN
GEMM_TILE = 512
GEMM_SUB = 256
TOK_TILE = 512
ATT_Q = 512
ATT_K = LANES
ATT_GROUP = 2

F32 = jnp.float32
BF16 = jnp.bfloat16


def _dot(a, b):
    return jnp.dot(a, b, preferred_element_type=F32)


def _dot_nt(a, b):
    return lax.dot_general(a, b, (((1,), (1,)), ((), ())), preferred_element_type=F32)


def _split_bf16(v):
    hi = v.astype(BF16)
    lo = (v - hi.astype(F32)).astype(BF16)
    return hi, lo


def _rmsnorm(v, g):
    return v * lax.rsqrt(jnp.mean(v * v, axis=-1, keepdims=True) + RMS_EPS) * g


def _pool_kernel(x_ref, halo_ref, g_ref, w_ref, scale_ref, o_ref):
    i = pl.program_id(1)
    x = x_ref[0]
    g = g_ref[...]
    h = _rmsnorm(x, g)
    hh = _rmsnorm(halo_ref[0], g)
    hh = jnp.where(i > 0, hh, 0.0)
    a = jnp.concatenate([hh, h], axis=0)
    ts = x.shape[0]
    pos = i * ts + lax.broadcasted_iota(jnp.int32, (ts, 1), 0)
    outs = []
    for gi, w in enumerate(POOL_WINDOWS):
        lo, hi = gi * POOL_GROUP_DIM, (gi + 1) * POOL_GROUP_DIM
        s = a[:, lo:hi]
        span = 1
        while span < w:
            s = s + pltpu.roll(s, span, axis=0)
            span *= 2
        cnt = jnp.minimum(pos + 1, w).astype(F32)
        pooled = s[POOL_HALO:, :] / cnt - h[:, lo:hi]
        outs.append(_dot(pooled.astype(BF16), w_ref[gi]))
    y = jnp.concatenate(outs, axis=-1) * scale_ref[...]
    o_ref[0] = x + y


def _pool_layer(x, g, w, scale):
    b, s, d = x.shape
    ts = min(POOL_TILE, s)
    blocks_per_tile = ts // POOL_HALO
    return pl.pallas_call(
        _pool_kernel,
        grid=(b, s // ts),
        in_specs=[
            pl.BlockSpec((1, ts, d), lambda bi, i: (bi, i, 0)),
            pl.BlockSpec((1, POOL_HALO, d), lambda bi, i: (bi, jnp.maximum(i * blocks_per_tile - 1, 0), 0)),
            pl.BlockSpec((1, d), lambda bi, i: (0, 0)),
            pl.BlockSpec((len(POOL_WINDOWS), POOL_GROUP_DIM, POOL_GROUP_DIM), lambda bi, i: (0, 0, 0)),
            pl.BlockSpec((1, d), lambda bi, i: (0, 0)),
        ],
        out_specs=pl.BlockSpec((1, ts, d), lambda bi, i: (bi, i, 0)),
        out_shape=jax.ShapeDtypeStruct(x.shape, F32),
        compiler_params=pltpu.CompilerParams(
            dimension_semantics=("arbitrary", "arbitrary"), vmem_limit_bytes=VMEM_LIMIT),
        name="pool_mixer",
    )(x, x, g.reshape(1, d), w.astype(BF16), scale.reshape(1, d))


def _route_kernel(x_ref, g_ref, wt_hi_ref, wt_lo_ref, b_ref, hb_ref, slot_ref, gate_ref, cnt_ref):
    c = x_ref.shape[0]
    h = _rmsnorm(x_ref[...], g_ref[...])
    h_hi, h_lo = _split_bf16(h)
    hb_ref[...] = h_hi
    logits = (_dot_nt(wt_hi_ref[...], h_hi) + _dot_nt(wt_lo_ref[...], h_hi)
              + _dot_nt(wt_hi_ref[...], h_lo) + b_ref[...])
    eio = lax.broadcasted_iota(jnp.int32, (N_EXPERTS, c), 0).astype(F32)
    vals, idxs = [], []
    l = logits
    for _ in range(TOP_K):
        m = jnp.max(l, axis=0, keepdims=True)
        idx = jnp.min(jnp.where(l == m, eio, float(N_EXPERTS)), axis=0, keepdims=True)
        vals.append(m)
        idxs.append(idx)
        l = jnp.where(eio == idx, -jnp.inf, l)
    es = [jnp.exp(v - vals[0]) for v in vals]
    denom = es[0] + es[1] + es[2] + es[3]
    sel = jnp.zeros((N_EXPERTS, c), F32)
    for idx in idxs:
        sel = sel + jnp.where(eio == idx, 1.0, 0.0)
    upper = jnp.where(lax.broadcasted_iota(jnp.int32, (c, c), 0) <= lax.broadcasted_iota(jnp.int32, (c, c), 1),
                      1.0, 0.0).astype(BF16)
    rank = _dot(sel.astype(BF16), upper)
    n = jnp.broadcast_to(rank[:, c - 1:c], (N_EXPERTS, c))
    n_units = jnp.floor((n + (SEG_ALIGN - 1)) * (1.0 / SEG_ALIGN))
    strict_lower = jnp.where(lax.broadcasted_iota(jnp.int32, (N_EXPERTS, N_EXPERTS), 1)
                             < lax.broadcasted_iota(jnp.int32, (N_EXPERTS, N_EXPERTS), 0), 1.0, 0.0).astype(BF16)
    off = _dot(strict_lower, n_units.astype(BF16)) * SEG_ALIGN
    dest = off + rank - 1.0
    for k in range(TOP_K):
        slot = jnp.sum(jnp.where(eio == idxs[k], dest, 0.0), axis=0, keepdims=True)
        slot_ref[0, k:k + 1, :] = slot.astype(jnp.int32)
        gate_ref[0, k:k + 1, :] = es[k] / denom
    slot_ref[0, TOP_K:, :] = jnp.zeros((SUBLANES - TOP_K, c), jnp.int32)
    gate_ref[0, TOP_K:, :] = jnp.zeros((SUBLANES - TOP_K, c), F32)
    cnt_ref[0] = (n_units[:, :LANES] * SEG_ALIGN).astype(jnp.int32)


def _route(x2d, g, router_w, router_b):
    t, d = x2d.shape
    nc = t // CHUNK
    wt_hi, wt_lo = _split_bf16(router_w.T)
    return pl.pallas_call(
        _route_kernel,
        grid=(nc,),
        in_specs=[
            pl.BlockSpec((CHUNK, d), lambda c: (c, 0)),
            pl.BlockSpec((1, d), lambda c: (0, 0)),
            pl.BlockSpec((N_EXPERTS, d), lambda c: (0, 0)),
            pl.BlockSpec((N_EXPERTS, d), lambda c: (0, 0)),
            pl.BlockSpec((N_EXPERTS, 1), lambda c: (0, 0)),
        ],
        out_specs=[
            pl.BlockSpec((CHUNK, d), lambda c: (c, 0)),
            pl.BlockSpec((1, SUBLANES, CHUNK), lambda c: (c, 0, 0)),
            pl.BlockSpec((1, SUBLANES, CHUNK), lambda c: (c, 0, 0)),
            pl.BlockSpec((1, N_EXPERTS, LANES), lambda c: (c, 0, 0)),
        ],
        out_shape=[
            jax.ShapeDtypeStruct((t, d), BF16),
            jax.ShapeDtypeStruct((nc, SUBLANES, CHUNK), jnp.int32),
            jax.ShapeDtypeStruct((nc, SUBLANES, CHUNK), F32),
            jax.ShapeDtypeStruct((nc, N_EXPERTS, LANES), jnp.int32),
        ],
        compiler_params=pltpu.CompilerParams(dimension_semantics=("arbitrary",), vmem_limit_bytes=VMEM_LIMIT),
        name="moe_route",
    )(x2d, g.reshape(1, d), wt_hi, wt_lo, router_b.reshape(N_EXPERTS, 1))


def _segment_tables(cnt, n_tiles_max):
    seg_n = cnt[:, :, 0]
    seg_loc = jnp.cumsum(seg_n, axis=1) - seg_n
    tot = jnp.sum(seg_n, axis=0)
    tiles = (tot + GEMM_TILE - 1) // GEMM_TILE
    tile_end = jnp.cumsum(tiles)
    tile_start = tile_end - tiles
    seg_dst = tile_start[None, :] * GEMM_TILE + jnp.cumsum(seg_n, axis=0) - seg_n
    used = tile_end[-1]
    tile_ids = jnp.arange(n_tiles_max, dtype=jnp.int32)
    tile_expert = jnp.sum(tile_ids[:, None] >= tile_end[None, :], axis=1).astype(jnp.int32)
    last_expert = jnp.sum(jnp.maximum(used - 1, 0) >= tile_end).astype(jnp.int32)
    tile_expert = jnp.where(tile_ids < used, tile_expert, last_expert)
    tile_valid = jnp.clip(tot[tile_expert] - (tile_ids - tile_start[tile_expert]) * GEMM_TILE, 0, GEMM_TILE)
    tile_valid = jnp.where(tile_ids < used, tile_valid, 0)
    i32 = lambda v: v.astype(jnp.int32).reshape(-1)
    gap = jnp.concatenate([tile_start * GEMM_TILE + tot, tiles * GEMM_TILE - tot, used[None]])
    return i32(seg_loc), i32(seg_n), i32(seg_dst), tile_expert, i32(tile_valid), i32(used), i32(gap)


def _segment_copy(loc_ref, n_ref, dst_ref, c, e, local_buf, sorted_hbm, sem, to_sorted):
    j = c * N_EXPERTS + e
    n = pl.multiple_of(n_ref[j], SEG_ALIGN)
    local = local_buf.at[pl.ds(pl.multiple_of(loc_ref[j], SEG_ALIGN), n)]
    remote = sorted_hbm.at[pl.ds(pl.multiple_of(dst_ref[j], SEG_ALIGN), n)]
    return n, (pltpu.make_async_copy(local, remote, sem) if to_sorted
               else pltpu.make_async_copy(remote, local, sem))


def _for_each_segment(loc_ref, n_ref, dst_ref, c, local_buf, sorted_hbm, sem, to_sorted, action):
    def body(e, carry):
        n, cp = _segment_copy(loc_ref, n_ref, dst_ref, c, e, local_buf, sorted_hbm, sem, to_sorted)

        @pl.when(n > 0)
        def _():
            action(cp)
        return carry
    lax.fori_loop(0, N_EXPERTS, body, 0)


def _zero_unused_rows(gap_ref, xs_hbm, zeros, sem, action):
    def expert_gap(e, carry):
        n = pl.multiple_of(gap_ref[N_EXPERTS + e], SEG_ALIGN)

        @pl.when(n > 0)
        def _():
            action(pltpu.make_async_copy(
                zeros.at[pl.ds(0, n)], xs_hbm.at[pl.ds(pl.multiple_of(gap_ref[e], SEG_ALIGN), n)], sem))
        return carry
    lax.fori_loop(0, N_EXPERTS, expert_gap, 0)

    def unused_tile(i, carry):
        action(pltpu.make_async_copy(zeros, xs_hbm.at[pl.ds(pl.multiple_of(i * GEMM_TILE, GEMM_TILE), GEMM_TILE)], sem))
        return carry
    lax.fori_loop(gap_ref[2 * N_EXPERTS], xs_hbm.shape[0] // GEMM_TILE, unused_tile, 0)


def _dispatch_kernel(loc_ref, n_ref, dst_ref, gap_ref, hb_ref, slot_ref, xs_hbm, buf, zeros, sems):
    c = pl.program_id(0)
    nc = pl.num_programs(0)
    par = c % 2
    start = lambda cp: cp.start()
    wait = lambda cp: cp.wait()

    @pl.when(c == 0)
    def _():
        zeros[...] = jnp.zeros(zeros.shape, F32)
        _zero_unused_rows(gap_ref, xs_hbm, zeros, sems.at[2], start)

    @pl.when(c >= 2)
    def _():
        _for_each_segment(loc_ref, n_ref, dst_ref, c - 2, buf.at[par], xs_hbm, sems.at[par], True, wait)

    rows = lax.broadcasted_iota(jnp.int32, (CHUNK_ROWS, CHUNK), 0)
    hit = rows == slot_ref[0, 0:1, :]
    for k in range(1, TOP_K):
        hit = jnp.logical_or(hit, rows == slot_ref[0, k:k + 1, :])
    onehot = jnp.where(hit, 1.0, 0.0).astype(BF16)
    buf[par] = _dot(onehot, hb_ref[...])
    _for_each_segment(loc_ref, n_ref, dst_ref, c, buf.at[par], xs_hbm, sems.at[par], True, start)

    @pl.when(c == nc - 1)
    def _():
        _zero_unused_rows(gap_ref, xs_hbm, zeros, sems.at[2], wait)

        @pl.when(c >= 1)
        def _():
            _for_each_segment(loc_ref, n_ref, dst_ref, c - 1, buf.at[1 - par], xs_hbm, sems.at[1 - par], True, wait)
        _for_each_segment(loc_ref, n_ref, dst_ref, c, buf.at[par], xs_hbm, sems.at[par], True, wait)


def _dispatch(hb, slots, seg_loc, seg_n, seg_dst, gap, n_rows):
    t, d = hb.shape
    nc = t // CHUNK
    return pl.pallas_call(
        _dispatch_kernel,
        grid_spec=pltpu.PrefetchScalarGridSpec(
            num_scalar_prefetch=4,
            grid=(nc,),
            in_specs=[
                pl.BlockSpec((CHUNK, d), lambda c, *_: (c, 0)),
                pl.BlockSpec((1, SUBLANES, CHUNK), lambda c, *_: (c, 0, 0)),
            ],
            out_specs=pl.BlockSpec(memory_space=pl.ANY),
            scratch_shapes=[pltpu.VMEM((2, CHUNK_ROWS, d), F32), pltpu.VMEM((GEMM_TILE, d), F32),
                            pltpu.SemaphoreType.DMA((3,))],
        ),
        out_shape=jax.ShapeDtypeStruct((n_rows, d), F32),
        compiler_params=pltpu.CompilerParams(dimension_semantics=("arbitrary",), vmem_limit_bytes=VMEM_LIMIT),
        name="moe_dispatch",
    )(seg_loc, seg_n, seg_dst, gap, hb, slots)


def _expert_kernel(te_ref, valid_ref, used_ref, x_ref, wgu_ref, bgu_ref, wd_ref, bd_ref, o_ref, wgu_bf, wd_bf):
    i = pl.program_id(0)
    prev = te_ref[jnp.maximum(i - 1, 0)]

    @pl.when(jnp.logical_or(i == 0, te_ref[i] != prev))
    def _():
        wgu_bf[...] = wgu_ref[0].astype(BF16)
        wd_bf[...] = wd_ref[0].astype(BF16)

    for sub in range(GEMM_TILE // GEMM_SUB):
        rows = pl.ds(sub * GEMM_SUB, GEMM_SUB)

        @pl.when(valid_ref[i] > sub * GEMM_SUB)
        def _():
            gu = _dot(x_ref[rows, :].astype(BF16), wgu_bf[...]) + bgu_ref[0]
            gate = jnp.minimum(gu[:, :D_FF], SWIGLU_LIMIT)
            up = jnp.clip(gu[:, D_FF:], -SWIGLU_LIMIT, SWIGLU_LIMIT)
            act = (up + 1.0) * (gate * (1.0 / (1.0 + jnp.exp(-SWIGLU_ALPHA * gate))))
            out = _dot(act.astype(BF16), wd_bf[...]) + bd_ref[0]
            o_ref[rows, :] = out.astype(BF16).astype(F32)

        @pl.when(valid_ref[i] <= sub * GEMM_SUB)
        def _():
            o_ref[rows, :] = jnp.zeros((GEMM_SUB, D_MODEL), F32)


def _experts(xs, tile_expert, tile_valid, used, layer, w_gate_up, b_gate_up, w_down, b_down):
    n_rows, d = xs.shape
    n_tiles = n_rows // GEMM_TILE
    n_stacked = w_gate_up.shape[0] * N_EXPERTS
    row_map = lambda i, te, tv, u: (jnp.minimum(i, jnp.maximum(u[0] - 1, 0)), 0)
    exp_map = lambda i, te, tv, u: (layer * N_EXPERTS + te[i], 0, 0)
    return pl.pallas_call(
        _expert_kernel,
        grid_spec=pltpu.PrefetchScalarGridSpec(
            num_scalar_prefetch=3,
            grid=(n_tiles,),
            in_specs=[
                pl.BlockSpec((GEMM_TILE, d), row_map),
                pl.BlockSpec((1, d, 2 * D_FF), exp_map),
                pl.BlockSpec((1, 1, 2 * D_FF), exp_map),
                pl.BlockSpec((1, D_FF, d), exp_map),
                pl.BlockSpec((1, 1, d), exp_map),
            ],
            out_specs=pl.BlockSpec((GEMM_TILE, d), lambda i, te, tv, u: (i, 0)),
            scratch_shapes=[pltpu.VMEM((d, 2 * D_FF), BF16), pltpu.VMEM((D_FF, d), BF16)],
        ),
        out_shape=jax.ShapeDtypeStruct((n_rows, d), F32),
        compiler_params=pltpu.CompilerParams(dimension_semantics=("arbitrary",), vmem_limit_bytes=VMEM_LIMIT),
        name="moe_experts",
    )(tile_expert, tile_valid, used, xs, w_gate_up.reshape(n_stacked, d, 2 * D_FF),
      b_gate_up.reshape(n_stacked, 1, 2 * D_FF), w_down.reshape(n_stacked, D_FF, d), b_down.reshape(n_stacked, 1, d))


def _combine_kernel(loc_ref, n_ref, dst_ref, x_ref, slot_t_ref, gate_t_ref, ys_hbm, o_ref, buf, sems):
    c = pl.program_id(0)
    nc = pl.num_programs(0)
    par = c % 2
    start = lambda cp: cp.start()
    wait = lambda cp: cp.wait()

    @pl.when(c == 0)
    def _():
        buf[...] = jnp.zeros(buf.shape, F32)
        _for_each_segment(loc_ref, n_ref, dst_ref, c, buf.at[0], ys_hbm, sems.at[0], False, start)

    @pl.when(c + 1 < nc)
    def _():
        _for_each_segment(loc_ref, n_ref, dst_ref, c + 1, buf.at[1 - par], ys_hbm, sems.at[1 - par], False, start)

    cols = lax.broadcasted_iota(jnp.int32, (CHUNK, CHUNK_ROWS), 1)
    st = slot_t_ref[0]
    gt = gate_t_ref[0]
    back = jnp.zeros((CHUNK, CHUNK_ROWS), F32)
    for k in range(TOP_K):
        back = back + jnp.where(cols == st[:, k:k + 1], gt[:, k:k + 1], 0.0)
    back_hi, back_lo = _split_bf16(back)

    _for_each_segment(loc_ref, n_ref, dst_ref, c, buf.at[par], ys_hbm, sems.at[par], False, wait)
    ys = buf[par].astype(BF16)
    o_ref[...] = x_ref[...] + (_dot(back_hi, ys) + _dot(back_lo, ys))


def _combine(x2d, ys, slots_t, gates_t, seg_loc, seg_n, seg_dst):
    t, d = x2d.shape
    nc = t // CHUNK
    return pl.pallas_call(
        _combine_kernel,
        grid_spec=pltpu.PrefetchScalarGridSpec(
            num_scalar_prefetch=3,
            grid=(nc,),
            in_specs=[
                pl.BlockSpec((CHUNK, d), lambda c, *_: (c, 0)),
                pl.BlockSpec((1, CHUNK, TOP_K), lambda c, *_: (c, 0, 0)),
                pl.BlockSpec((1, CHUNK, TOP_K), lambda c, *_: (c, 0, 0)),
                pl.BlockSpec(memory_space=pl.ANY),
            ],
            out_specs=pl.BlockSpec((CHUNK, d), lambda c, *_: (c, 0)),
            scratch_shapes=[pltpu.VMEM((2, CHUNK_ROWS, d), F32), pltpu.SemaphoreType.DMA((2,))],
        ),
        out_shape=jax.ShapeDtypeStruct((t, d), F32),
        compiler_params=pltpu.CompilerParams(dimension_semantics=("arbitrary",), vmem_limit_bytes=VMEM_LIMIT),
        name="moe_combine",
    )(seg_loc, seg_n, seg_dst, x2d, slots_t, gates_t, ys)


def _moe_layer(x2d, g, router_w, router_b, layer, w_gate_up, b_gate_up, w_down, b_down):
    t, d = x2d.shape
    nc = t // CHUNK
    n_tiles_max = (t * TOP_K + nc * N_EXPERTS * (SEG_ALIGN - 1)) // GEMM_TILE + N_EXPERTS
    hb, slots, gates, cnt = _route(x2d, g, router_w, router_b)
    seg_loc, seg_n, seg_dst, tile_expert, tile_valid, used, gap = _segment_tables(cnt, n_tiles_max)
    xs = _dispatch(hb, slots, seg_loc, seg_n, seg_dst, gap, n_tiles_max * GEMM_TILE)
    ys = _experts(xs, tile_expert, tile_valid, used, layer, w_gate_up, b_gate_up, w_down, b_down)
    slots_t = jnp.swapaxes(slots[:, :TOP_K, :], 1, 2)
    gates_t = jnp.swapaxes(gates[:, :TOP_K, :], 1, 2)
    return _combine(x2d, ys, slots_t, gates_t, seg_loc, seg_n, seg_dst)


def _qkv_kernel(x_ref, g_ref, w_ref, qg_ref, kg_ref, q_ref, k_ref, v_ref):
    d = x_ref.shape[1]
    h = _rmsnorm(x_ref[...], g_ref[...]).astype(BF16)
    qkv = _dot(h, w_ref[...])
    li = lax.broadcasted_iota(jnp.int32, (LANES, LANES), 0) // HEAD_DIM
    lj = lax.broadcasted_iota(jnp.int32, (LANES, LANES), 1) // HEAD_DIM
    same_head = jnp.where(li == lj, 1.0, 0.0).astype(BF16)

    def head_norm(v, gain, out_scale):
        parts = []
        for j in range(d // LANES):
            vj = v[:, j * LANES:(j + 1) * LANES]
            hi, lo = _split_bf16(vj * vj)
            ss = _dot(hi, same_head) + _dot(lo, same_head)
            parts.append(vj * lax.rsqrt(ss * (1.0 / HEAD_DIM) + RMS_EPS))
        return (jnp.concatenate(parts, axis=-1) * gain * out_scale).astype(BF16)

    q_ref[...] = head_norm(qkv[:, :d], qg_ref[...], HEAD_DIM ** -0.5)
    k_ref[...] = head_norm(qkv[:, d:2 * d], kg_ref[...], 1.0)
    v_ref[...] = qkv[:, 2 * d:].astype(BF16)


def _qkv(x2d, g, w_qkv, q_gain, k_gain):
    t, d = x2d.shape
    tt = min(TOK_TILE, t)
    row = lambda i: (i, 0)
    fixed = lambda i: (0, 0)
    return pl.pallas_call(
        _qkv_kernel,
        grid=(t // tt,),
        in_specs=[
            pl.BlockSpec((tt, d), row),
            pl.BlockSpec((1, d), fixed),
            pl.BlockSpec((d, 3 * d), fixed),
            pl.BlockSpec((1, d), fixed),
            pl.BlockSpec((1, d), fixed),
        ],
        out_specs=[pl.BlockSpec((tt, d), row)] * 3,
        out_shape=[jax.ShapeDtypeStruct((t, d), BF16)] * 3,
        compiler_params=pltpu.CompilerParams(dimension_semantics=("arbitrary",), vmem_limit_bytes=VMEM_LIMIT),
        name="attn_qkv",
    )(x2d, g.reshape(1, d), w_qkv.astype(BF16), jnp.tile(q_gain, N_HEADS).reshape(1, d),
      jnp.tile(k_gain, N_HEADS).reshape(1, d))


def _attn_kernel(q_ref, k_ref, v_ref, o_ref, acc_ref, carry_ref):
    qi = pl.program_id(2)
    tq, kb = ATT_Q, ATT_K
    q = q_ref[0]
    head_of_lane = lax.broadcasted_iota(jnp.int32, (tq, LANES), 1) // HEAD_DIM
    q_heads = [jnp.where(head_of_lane == hd, q, jnp.zeros_like(q)) for hd in range(2)]
    r2 = lax.broadcasted_iota(jnp.int32, (2 * kb, 2 * kb), 0)
    c2 = lax.broadcasted_iota(jnp.int32, (2 * kb, 2 * kb), 1)
    suffix2 = jnp.where(jnp.logical_and(r2 // kb == c2 // kb, r2 >= c2), 1.0, 0.0).astype(BF16)

    acc_ref[...] = jnp.zeros(acc_ref.shape, F32)
    carry_ref[...] = jnp.zeros(carry_ref.shape, F32)

    def key_group(j0, diag_offsets, r0):
        nb = len(diag_offsets)
        rows = tq - r0
        start = pl.multiple_of(j0 * kb, kb)
        kk = k_ref[0, pl.ds(start, nb * kb), :]
        vv = v_ref[0, pl.ds(start, nb * kb), :]
        row = lax.broadcasted_iota(jnp.int32, (rows, kb), 0) + r0
        key = lax.broadcasted_iota(jnp.int32, (rows, kb), 1)
        z_all = [_dot_nt(q_heads[hd][r0:, :], kk) for hd in range(2)]
        carry = [carry_ref[hd, r0:, :] for hd in range(2)]
        probs = [[None] * nb for _ in range(2)]
        for b in reversed(range(nb)):
            causal = None if diag_offsets[b] is None else key + diag_offsets[b] < row
            zs, log_keeps = [], []
            for hd in range(2):
                z = z_all[hd][:, b * kb:(b + 1) * kb]
                neg = -z
                log_keep = jnp.minimum(neg, 0.0) - jnp.log(1.0 + jnp.exp(jnp.minimum(z, neg)))
                if causal is not None:
                    log_keep = jnp.where(causal, log_keep, 0.0)
                zs.append(z)
                log_keeps.append(log_keep.astype(BF16))
            sums = _dot(jnp.concatenate(log_keeps, axis=1), suffix2)
            for hd in range(2):
                s_in = sums[:, hd * kb:(hd + 1) * kb]
                a = jnp.exp(zs[hd] + s_in + carry[hd])
                if causal is not None:
                    a = jnp.where(causal, a, 0.0)
                probs[hd][b] = a.astype(BF16)
                carry[hd] = carry[hd] + jnp.broadcast_to(s_in[:, 0:1], (rows, kb))
        for hd in range(2):
            carry_ref[hd, r0:, :] = carry[hd]
            acc_ref[hd, r0:, :] += _dot(jnp.concatenate(probs[hd], axis=1), vv)

    blocks_per_q = tq // kb
    for g in reversed(range(blocks_per_q // ATT_GROUP)):
        offs = [(g * ATT_GROUP + b) * kb for b in range(ATT_GROUP)]
        key_group(qi * blocks_per_q + g * ATT_GROUP, offs, offs[0])

    def body(s, c):
        key_group(qi * blocks_per_q - (s + 1) * ATT_GROUP, [None] * ATT_GROUP, 0)
        return c
    lax.fori_loop(0, qi * (blocks_per_q // ATT_GROUP), body, 0)
    o_ref[0] = jnp.where(head_of_lane == 0, acc_ref[0], acc_ref[1]).astype(BF16)


def _attention(q, k, v, b, s):
    d = q.shape[-1]
    q3, k3, v3 = (a.reshape(b, s, d) for a in (q, k, v))
    pairs = d // LANES
    o = pl.pallas_call(
        _attn_kernel,
        grid=(b, pairs, s // ATT_Q),
        in_specs=[
            pl.BlockSpec((1, ATT_Q, LANES), lambda bi, hp, i: (bi, i, hp)),
            pl.BlockSpec((1, s, LANES), lambda bi, hp, i: (bi, 0, hp)),
            pl.BlockSpec((1, s, LANES), lambda bi, hp, i: (bi, 0, hp)),
        ],
        out_specs=pl.BlockSpec((1, ATT_Q, LANES), lambda bi, hp, i: (bi, i, hp)),
        out_shape=jax.ShapeDtypeStruct((b, s, d), BF16),
        scratch_shapes=[pltpu.VMEM((2, ATT_Q, LANES), F32), pltpu.VMEM((2, ATT_Q, LANES), F32)],
        compiler_params=pltpu.CompilerParams(
            dimension_semantics=("arbitrary", "arbitrary", "arbitrary"), vmem_limit_bytes=VMEM_LIMIT),
        name="attn_core",
    )(q3, k3, v3)
    return o.reshape(b * s, d)


def _out_proj_kernel(x_ref, o_ref, w_ref, y_ref):
    y_ref[...] = x_ref[...] + _dot(o_ref[...], w_ref[...])


def _out_proj(x2d, o, w_o):
    t, d = x2d.shape
    tt = min(TOK_TILE, t)
    return pl.pallas_call(
        _out_proj_kernel,
        grid=(t // tt,),
        in_specs=[
            pl.BlockSpec((tt, d), lambda i: (i, 0)),
            pl.BlockSpec((tt, d), lambda i: (i, 0)),
            pl.BlockSpec((d, d), lambda i: (0, 0)),
        ],
        out_specs=pl.BlockSpec((tt, d), lambda i: (i, 0)),
        out_shape=jax.ShapeDtypeStruct((t, d), F32),
        compiler_params=pltpu.CompilerParams(dimension_semantics=("arbitrary",), vmem_limit_bytes=VMEM_LIMIT),
        name="attn_out_proj",
    )(x2d, o, w_o.astype(BF16))


def _attention_layer(x2d, b, s, g, w_qkv, q_gain, k_gain, w_o):
    q, k, v = _qkv(x2d, g, w_qkv, q_gain, k_gain)
    o = _attention(q, k, v, b, s)
    return _out_proj(x2d, o, w_o)


def kernel(x, mix_norm, pool_w, pool_scale, w_qkv, q_norm, k_norm, w_o, ffn_norm, router_w, router_b,
           w_gate_up, b_gate_up, w_down, b_down):
    b, s, d = x.shape
    assert d == D_MODEL and s % ATT_Q == 0 and (b * s) % CHUNK == 0
    depth = mix_norm.shape[0]
    for i in range(depth):
        j = i // 2
        if i % 2 == 0:
            x2d = _pool_layer(x, mix_norm[i], pool_w[j], pool_scale[j]).reshape(b * s, d)
        else:
            x2d = _attention_layer(x.reshape(b * s, d), b, s, mix_norm[i], w_qkv[j], q_norm[j], k_norm[j], w_o[j])
        x2d = _moe_layer(x2d, ffn_norm[i], router_w[i], router_b[i], i, w_gate_up, b_gate_up, w_down, b_down)
        x = x2d.reshape(b, s, d)
    return x
```

```python
import jax
import jax.numpy as jnp
from jax import lax
from jax.experimental import pallas as pl
from jax.experimental.pallas import tpu as pltpu

D_MODEL = 1024
POOL_WINDOWS = (2, 4, 8, 16)
POOL_GROUP_DIM = D_MODEL // len(POOL_WINDOWS)
POOL_HALO = 16
HEAD_DIM = 64
N_HEADS = D_MODEL // HEAD_DIM
N_EXPERTS = 32
TOP_K = 4
D_FF = D_MODEL
SWIGLU_LIMIT = 7.0
SWIGLU_ALPHA = 1.702
RMS_EPS = 1e-6
LOG2_E = 1.4426950408889634

LANES = 128
SUBLANES = 8
VMEM_LIMIT = 56 * 1024 * 1024

POOL_TILE = 512
CHUNK = 256
ROUTE_CHUNKS = 1
SEG_ALIGN = 2 * SUBLANES
CHUNK_ROWS = CHUNK * TOP_K + N_EXPERTS * SEG_ALIGN
GEMM_TILE = 512
GEMM_SUB = 256
TOK_TILE = 512
ATT_Q = 512
ATT_K = LANES
ATT_GROUP = 4
ATT_DIAG_GROUP = 2

F32 = jnp.float32
BF16 = jnp.bfloat16


def _dot(a, b):
    return jnp.dot(a, b, preferred_element_type=F32)


def _dot_nt(a, b):
    return lax.dot_general(a, b, (((1,), (1,)), ((), ())), preferred_element_type=F32)


def _split_bf16(v):
    hi = v.astype(BF16)
    lo = (v - hi.astype(F32)).astype(BF16)
    return hi, lo


def _rmsnorm(v, g):
    return v * lax.rsqrt(jnp.mean(v * v, axis=-1, keepdims=True) + RMS_EPS) * g


def _pool_kernel(x_ref, halo_ref, g_ref, w_ref, scale_ref, o_ref):
    i = pl.program_id(1)
    x = x_ref[0]
    g = g_ref[...]
    h = _rmsnorm(x, g)
    hh = _rmsnorm(halo_ref[0], g)
    hh = jnp.where(i > 0, hh, 0.0)
    a = jnp.concatenate([hh, h], axis=0)
    ts = x.shape[0]
    pos = i * ts + lax.broadcasted_iota(jnp.int32, (ts, 1), 0)
    outs = []
    for gi, w in enumerate(POOL_WINDOWS):
        lo, hi = gi * POOL_GROUP_DIM, (gi + 1) * POOL_GROUP_DIM
        s = a[:, lo:hi]
        span = 1
        while span < w:
            s = s + pltpu.roll(s, span, axis=0)
            span *= 2
        cnt = jnp.minimum(pos + 1, w).astype(F32)
        pooled = s[POOL_HALO:, :] / cnt - h[:, lo:hi]
        outs.append(_dot(pooled.astype(BF16), w_ref[gi]))
    y = jnp.concatenate(outs, axis=-1) * scale_ref[...]
    o_ref[0] = x + y


def _pool_layer(x, g, w, scale):
    b, s, d = x.shape
    ts = min(POOL_TILE, s)
    blocks_per_tile = ts // POOL_HALO
    return pl.pallas_call(
        _pool_kernel,
        grid=(b, s // ts),
        in_specs=[
            pl.BlockSpec((1, ts, d), lambda bi, i: (bi, i, 0)),
            pl.BlockSpec((1, POOL_HALO, d), lambda bi, i: (bi, jnp.maximum(i * blocks_per_tile - 1, 0), 0)),
            pl.BlockSpec((1, d), lambda bi, i: (0, 0)),
            pl.BlockSpec((len(POOL_WINDOWS), POOL_GROUP_DIM, POOL_GROUP_DIM), lambda bi, i: (0, 0, 0)),
            pl.BlockSpec((1, d), lambda bi, i: (0, 0)),
        ],
        out_specs=pl.BlockSpec((1, ts, d), lambda bi, i: (bi, i, 0)),
        out_shape=jax.ShapeDtypeStruct(x.shape, F32),
        compiler_params=pltpu.CompilerParams(
            dimension_semantics=("arbitrary", "arbitrary"), vmem_limit_bytes=VMEM_LIMIT),
        name="pool_mixer",
    )(x, x, g.reshape(1, d), w.astype(BF16), scale.reshape(1, d))


def _route_kernel(x_ref, g_ref, wt_hi_ref, wt_lo_ref, b_ref, hb_ref, slot_ref, gate_ref, cnt_ref):
    for ci in range(ROUTE_CHUNKS):
        _route_chunk(ci, x_ref, g_ref, wt_hi_ref, wt_lo_ref, b_ref, hb_ref, slot_ref, gate_ref, cnt_ref)


def _route_chunk(ci, x_ref, g_ref, wt_hi_ref, wt_lo_ref, b_ref, hb_ref, slot_ref, gate_ref, cnt_ref):
    c = CHUNK
    tok = pl.ds(ci * CHUNK, CHUNK)
    h = _rmsnorm(x_ref[tok, :], g_ref[...])
    h_hi, h_lo = _split_bf16(h)
    hb_ref[tok, :] = h_hi
    logits = (_dot_nt(wt_hi_ref[...], h_hi) + _dot_nt(wt_lo_ref[...], h_hi)
              + _dot_nt(wt_hi_ref[...], h_lo) + b_ref[...])
    eio = lax.broadcasted_iota(jnp.int32, (N_EXPERTS, c), 0).astype(F32)
    vals, idxs = [], []
    l = logits
    for _ in range(TOP_K):
        m = jnp.max(l, axis=0, keepdims=True)
        idx = jnp.min(jnp.where(l == m, eio, float(N_EXPERTS)), axis=0, keepdims=True)
        vals.append(m)
        idxs.append(idx)
        l = jnp.where(eio == idx, -jnp.inf, l)
    es = [jnp.exp(v - vals[0]) for v in vals]
    denom = es[0] + es[1] + es[2] + es[3]
    sel = jnp.zeros((N_EXPERTS, c), F32)
    for idx in idxs:
        sel = sel + jnp.where(eio == idx, 1.0, 0.0)
    upper = jnp.where(lax.broadcasted_iota(jnp.int32, (c, c), 0) <= lax.broadcasted_iota(jnp.int32, (c, c), 1),
                      1.0, 0.0).astype(BF16)
    rank = _dot(sel.astype(BF16), upper)
    n = jnp.broadcast_to(rank[:, c - 1:c], (N_EXPERTS, c))
    n_units = jnp.floor((n + (SEG_ALIGN - 1)) * (1.0 / SEG_ALIGN))
    strict_lower = jnp.where(lax.broadcasted_iota(jnp.int32, (N_EXPERTS, N_EXPERTS), 1)
                             < lax.broadcasted_iota(jnp.int32, (N_EXPERTS, N_EXPERTS), 0), 1.0, 0.0).astype(BF16)
    off = _dot(strict_lower, n_units.astype(BF16)) * SEG_ALIGN
    dest = off + rank - 1.0
    for k in range(TOP_K):
        slot = jnp.sum(jnp.where(eio == idxs[k], dest, 0.0), axis=0, keepdims=True)
        slot_ref[ci, k:k + 1, :] = slot.astype(jnp.int32)
        gate_ref[ci, k:k + 1, :] = es[k] / denom
    slot_ref[ci, TOP_K:, :] = jnp.zeros((SUBLANES - TOP_K, c), jnp.int32)
    gate_ref[ci, TOP_K:, :] = jnp.zeros((SUBLANES - TOP_K, c), F32)
    cnt_ref[ci] = (n_units[:, :LANES] * SEG_ALIGN).astype(jnp.int32)


def _route(x2d, g, router_w, router_b):
    t, d = x2d.shape
    nc = t // CHUNK
    rc = ROUTE_CHUNKS
    wt_hi, wt_lo = _split_bf16(router_w.T)
    return pl.pallas_call(
        _route_kernel,
        grid=(nc // rc,),
        in_specs=[
            pl.BlockSpec((rc * CHUNK, d), lambda c: (c, 0)),
            pl.BlockSpec((1, d), lambda c: (0, 0)),
            pl.BlockSpec((N_EXPERTS, d), lambda c: (0, 0)),
            pl.BlockSpec((N_EXPERTS, d), lambda c: (0, 0)),
            pl.BlockSpec((N_EXPERTS, 1), lambda c: (0, 0)),
        ],
        out_specs=[
            pl.BlockSpec((rc * CHUNK, d), lambda c: (c, 0)),
            pl.BlockSpec((rc, SUBLANES, CHUNK), lambda c: (c, 0, 0)),
            pl.BlockSpec((rc, SUBLANES, CHUNK), lambda c: (c, 0, 0)),
            pl.BlockSpec((rc, N_EXPERTS, LANES), lambda c: (c, 0, 0)),
        ],
        out_shape=[
            jax.ShapeDtypeStruct((t, d), BF16),
            jax.ShapeDtypeStruct((nc, SUBLANES, CHUNK), jnp.int32),
            jax.ShapeDtypeStruct((nc, SUBLANES, CHUNK), F32),
            jax.ShapeDtypeStruct((nc, N_EXPERTS, LANES), jnp.int32),
        ],
        compiler_params=pltpu.CompilerParams(dimension_semantics=("arbitrary",), vmem_limit_bytes=VMEM_LIMIT),
        name="moe_route",
    )(x2d, g.reshape(1, d), wt_hi, wt_lo, router_b.reshape(N_EXPERTS, 1))


def _segment_tables(cnt, n_tiles_max):
    seg_n = cnt[:, :, 0]
    seg_loc = jnp.cumsum(seg_n, axis=1) - seg_n
    tot = jnp.sum(seg_n, axis=0)
    tiles = (tot + GEMM_TILE - 1) // GEMM_TILE
    tile_end = jnp.cumsum(tiles)
    tile_start = tile_end - tiles
    seg_dst = tile_start[None, :] * GEMM_TILE + jnp.cumsum(seg_n, axis=0) - seg_n
    used = tile_end[-1]
    tile_ids = jnp.arange(n_tiles_max, dtype=jnp.int32)
    tile_expert = jnp.sum(tile_ids[:, None] >= tile_end[None, :], axis=1).astype(jnp.int32)
    last_expert = jnp.sum(jnp.maximum(used - 1, 0) >= tile_end).astype(jnp.int32)
    tile_expert = jnp.where(tile_ids < used, tile_expert, last_expert)
    owner = tile_expert[:, None] == jnp.arange(N_EXPERTS, dtype=jnp.int32)[None, :]
    rows_left = jnp.sum(jnp.where(owner, (tot + tile_start * GEMM_TILE)[None, :], 0), axis=1) - tile_ids * GEMM_TILE
    tile_valid = jnp.where(tile_ids < used, jnp.clip(rows_left, 0, GEMM_TILE), 0)
    i32 = lambda v: v.astype(jnp.int32).reshape(-1)
    gap = jnp.concatenate([tile_start * GEMM_TILE + tot, tiles * GEMM_TILE - tot, used[None]])
    seg_tot = jnp.sum(seg_n, axis=1)
    return i32(seg_loc), i32(seg_n), i32(seg_dst), i32(seg_tot), tile_expert, i32(tile_valid), i32(used), i32(gap)


def _segment_copy(loc_ref, n_ref, dst_ref, c, e, local_buf, sorted_hbm, sem, to_sorted):
    j = c * N_EXPERTS + e
    n = pl.multiple_of(n_ref[j], SEG_ALIGN)
    local = local_buf.at[pl.ds(pl.multiple_of(loc_ref[j], SEG_ALIGN), n)]
    remote = sorted_hbm.at[pl.ds(pl.multiple_of(dst_ref[j], SEG_ALIGN), n)]
    return n, (pltpu.make_async_copy(local, remote, sem) if to_sorted
               else pltpu.make_async_copy(remote, local, sem))


def _for_each_segment(loc_ref, n_ref, dst_ref, c, local_buf, sorted_hbm, sem, to_sorted, action):
    def body(e, carry):
        n, cp = _segment_copy(loc_ref, n_ref, dst_ref, c, e, local_buf, sorted_hbm, sem, to_sorted)

        @pl.when(n > 0)
        def _():
            action(cp)
        return carry
    lax.fori_loop(0, N_EXPERTS, body, 0)


def _wait_chunk(tot_ref, c, local_buf, sorted_hbm, sem, to_sorted):
    n = pl.multiple_of(tot_ref[c], SEG_ALIGN)
    local = local_buf.at[pl.ds(0, n)]
    remote = sorted_hbm.at[pl.ds(0, n)]
    (pltpu.make_async_copy(local, remote, sem) if to_sorted else pltpu.make_async_copy(remote, local, sem)).wait()


def _zero_unused_rows(gap_ref, xs_hbm, zeros, sem, action):
    def expert_gap(e, carry):
        n = pl.multiple_of(gap_ref[N_EXPERTS + e], SEG_ALIGN)

        @pl.when(n > 0)
        def _():
            action(pltpu.make_async_copy(
                zeros.at[pl.ds(0, n)], xs_hbm.at[pl.ds(pl.multiple_of(gap_ref[e], SEG_ALIGN), n)], sem))
        return carry
    lax.fori_loop(0, N_EXPERTS, expert_gap, 0)

    def unused_tile(i, carry):
        action(pltpu.make_async_copy(zeros, xs_hbm.at[pl.ds(pl.multiple_of(i * GEMM_TILE, GEMM_TILE), GEMM_TILE)], sem))
        return carry
    lax.fori_loop(gap_ref[2 * N_EXPERTS], xs_hbm.shape[0] // GEMM_TILE, unused_tile, 0)


def _dispatch_kernel(loc_ref, n_ref, dst_ref, tot_ref, gap_ref, hb_ref, slot_ref, xs_hbm, buf, zeros, sems):
    c = pl.program_id(0)
    nc = pl.num_programs(0)
    par = c % 2
    start = lambda cp: cp.start()
    wait = lambda cp: cp.wait()

    @pl.when(c == 0)
    def _():
        zeros[...] = jnp.zeros(zeros.shape, BF16)
        _zero_unused_rows(gap_ref, xs_hbm, zeros, sems.at[2], start)

    @pl.when(c >= 2)
    def _():
        _wait_chunk(tot_ref, c - 2, buf.at[par], xs_hbm, sems.at[par], True)

    rows = lax.broadcasted_iota(jnp.int32, (CHUNK_ROWS, CHUNK), 0)
    hit = rows == slot_ref[0, 0:1, :]
    for k in range(1, TOP_K):
        hit = jnp.logical_or(hit, rows == slot_ref[0, k:k + 1, :])
    onehot = jnp.where(hit, 1.0, 0.0).astype(BF16)
    buf[par] = _dot(onehot, hb_ref[...]).astype(BF16)
    _for_each_segment(loc_ref, n_ref, dst_ref, c, buf.at[par], xs_hbm, sems.at[par], True, start)

    @pl.when(c == nc - 1)
    def _():
        _zero_unused_rows(gap_ref, xs_hbm, zeros, sems.at[2], wait)

        @pl.when(c >= 1)
        def _():
            _wait_chunk(tot_ref, c - 1, buf.at[1 - par], xs_hbm, sems.at[1 - par], True)
        _wait_chunk(tot_ref, c, buf.at[par], xs_hbm, sems.at[par], True)


def _dispatch(hb, slots, seg_loc, seg_n, seg_dst, seg_tot, gap, n_rows):
    t, d = hb.shape
    nc = t // CHUNK
    return pl.pallas_call(
        _dispatch_kernel,
        grid_spec=pltpu.PrefetchScalarGridSpec(
            num_scalar_prefetch=5,
            grid=(nc,),
            in_specs=[
                pl.BlockSpec((CHUNK, d), lambda c, *_: (c, 0)),
                pl.BlockSpec((1, SUBLANES, CHUNK), lambda c, *_: (c, 0, 0)),
            ],
            out_specs=pl.BlockSpec(memory_space=pl.ANY),
            scratch_shapes=[pltpu.VMEM((2, CHUNK_ROWS, d), BF16), pltpu.VMEM((GEMM_TILE, d), BF16),
                            pltpu.SemaphoreType.DMA((3,))],
        ),
        out_shape=jax.ShapeDtypeStruct((n_rows, d), BF16),
        compiler_params=pltpu.CompilerParams(dimension_semantics=("arbitrary",), vmem_limit_bytes=VMEM_LIMIT),
        name="moe_dispatch",
    )(seg_loc, seg_n, seg_dst, seg_tot, gap, hb, slots)


def _expert_kernel(te_ref, valid_ref, used_ref, x_ref, wgu_ref, bgu_ref, wd_ref, bd_ref, o_ref, wgu_bf, wd_bf):
    i = pl.program_id(0)
    prev = te_ref[jnp.maximum(i - 1, 0)]

    @pl.when(jnp.logical_or(i == 0, te_ref[i] != prev))
    def _():
        wgu_bf[...] = wgu_ref[0].astype(BF16)
        wd_bf[...] = wd_ref[0].astype(BF16)

    for sub in range(GEMM_TILE // GEMM_SUB):
        rows = pl.ds(sub * GEMM_SUB, GEMM_SUB)

        @pl.when(valid_ref[i] > sub * GEMM_SUB)
        def _():
            gu = _dot(x_ref[rows, :], wgu_bf[...]) + bgu_ref[0]
            gate = jnp.minimum(gu[:, :D_FF], SWIGLU_LIMIT)
            up = jnp.clip(gu[:, D_FF:], -SWIGLU_LIMIT, SWIGLU_LIMIT)
            act = (up + 1.0) * (gate * (1.0 / (1.0 + jnp.exp(-SWIGLU_ALPHA * gate))))
            out = _dot(act.astype(BF16), wd_bf[...]) + bd_ref[0]
            o_ref[rows, :] = out.astype(BF16)

        @pl.when(valid_ref[i] <= sub * GEMM_SUB)
        def _():
            o_ref[rows, :] = jnp.zeros((GEMM_SUB, D_MODEL), BF16)


def _experts(xs, tile_expert, tile_valid, used, layer, w_gate_up, b_gate_up, w_down, b_down):
    n_rows, d = xs.shape
    n_tiles = n_rows // GEMM_TILE
    n_stacked = w_gate_up.shape[0] * N_EXPERTS
    row_map = lambda i, te, tv, u: (jnp.minimum(i, jnp.maximum(u[0] - 1, 0)), 0)
    exp_map = lambda i, te, tv, u: (layer * N_EXPERTS + te[i], 0, 0)
    return pl.pallas_call(
        _expert_kernel,
        grid_spec=pltpu.PrefetchScalarGridSpec(
            num_scalar_prefetch=3,
            grid=(n_tiles,),
            in_specs=[
                pl.BlockSpec((GEMM_TILE, d), row_map),
                pl.BlockSpec((1, d, 2 * D_FF), exp_map),
                pl.BlockSpec((1, 1, 2 * D_FF), exp_map),
                pl.BlockSpec((1, D_FF, d), exp_map),
                pl.BlockSpec((1, 1, d), exp_map),
            ],
            out_specs=pl.BlockSpec((GEMM_TILE, d), lambda i, te, tv, u: (i, 0)),
            scratch_shapes=[pltpu.VMEM((d, 2 * D_FF), BF16), pltpu.VMEM((D_FF, d), BF16)],
        ),
        out_shape=jax.ShapeDtypeStruct((n_rows, d), BF16),
        compiler_params=pltpu.CompilerParams(dimension_semantics=("arbitrary",), vmem_limit_bytes=VMEM_LIMIT),
        name="moe_experts",
    )(tile_expert, tile_valid, used, xs, w_gate_up.reshape(n_stacked, d, 2 * D_FF),
      b_gate_up.reshape(n_stacked, 1, 2 * D_FF), w_down.reshape(n_stacked, D_FF, d), b_down.reshape(n_stacked, 1, d))


def _combine_kernel(loc_ref, n_ref, dst_ref, tot_ref, x_ref, slot_t_ref, gate_t_ref, ys_hbm, o_ref, buf, sems):
    c = pl.program_id(0)
    nc = pl.num_programs(0)
    par = c % 2
    start = lambda cp: cp.start()
    wait = lambda cp: cp.wait()

    @pl.when(c == 0)
    def _():
        buf[...] = jnp.zeros(buf.shape, BF16)
        _for_each_segment(loc_ref, n_ref, dst_ref, c, buf.at[0], ys_hbm, sems.at[0], False, start)

    @pl.when(c + 1 < nc)
    def _():
        _for_each_segment(loc_ref, n_ref, dst_ref, c + 1, buf.at[1 - par], ys_hbm, sems.at[1 - par], False, start)

    cols = lax.broadcasted_iota(jnp.int32, (CHUNK, CHUNK_ROWS), 1)
    st = slot_t_ref[0]
    gt = gate_t_ref[0]
    back = jnp.zeros((CHUNK, CHUNK_ROWS), F32)
    for k in range(TOP_K):
        back = jnp.where(cols == st[:, k:k + 1], gt[:, k:k + 1], back)
    back = back.astype(BF16)

    _wait_chunk(tot_ref, c, buf.at[par], ys_hbm, sems.at[par], False)
    ys = buf[par]
    o_ref[...] = x_ref[...] + _dot(back, ys)


def _combine(x2d, ys, slots_t, gates_t, seg_loc, seg_n, seg_dst, seg_tot):
    t, d = x2d.shape
    nc = t // CHUNK
    return pl.pallas_call(
        _combine_kernel,
        grid_spec=pltpu.PrefetchScalarGridSpec(
            num_scalar_prefetch=4,
            grid=(nc,),
            in_specs=[
                pl.BlockSpec((CHUNK, d), lambda c, *_: (c, 0)),
                pl.BlockSpec((1, CHUNK, TOP_K), lambda c, *_: (c, 0, 0)),
                pl.BlockSpec((1, CHUNK, TOP_K), lambda c, *_: (c, 0, 0)),
                pl.BlockSpec(memory_space=pl.ANY),
            ],
            out_specs=pl.BlockSpec((CHUNK, d), lambda c, *_: (c, 0)),
            scratch_shapes=[pltpu.VMEM((2, CHUNK_ROWS, d), BF16), pltpu.SemaphoreType.DMA((2,))],
        ),
        out_shape=jax.ShapeDtypeStruct((t, d), F32),
        compiler_params=pltpu.CompilerParams(dimension_semantics=("arbitrary",), vmem_limit_bytes=VMEM_LIMIT),
        name="moe_combine",
    )(seg_loc, seg_n, seg_dst, seg_tot, x2d, slots_t, gates_t, ys)


def _moe_layer(x2d, g, router_w, router_b, layer, w_gate_up, b_gate_up, w_down, b_down):
    t, d = x2d.shape
    nc = t // CHUNK
    n_tiles_max = (t * TOP_K + nc * N_EXPERTS * (SEG_ALIGN - 1)) // GEMM_TILE + N_EXPERTS
    hb, slots, gates, cnt = _route(x2d, g, router_w, router_b)
    seg_loc, seg_n, seg_dst, seg_tot, tile_expert, tile_valid, used, gap = _segment_tables(cnt, n_tiles_max)
    xs = _dispatch(hb, slots, seg_loc, seg_n, seg_dst, seg_tot, gap, n_tiles_max * GEMM_TILE)
    ys = _experts(xs, tile_expert, tile_valid, used, layer, w_gate_up, b_gate_up, w_down, b_down)
    slots_t = jnp.swapaxes(slots[:, :TOP_K, :], 1, 2)
    gates_t = jnp.swapaxes(gates[:, :TOP_K, :], 1, 2)
    return _combine(x2d, ys, slots_t, gates_t, seg_loc, seg_n, seg_dst, seg_tot)


def _qkv_kernel(x_ref, g_ref, w_ref, qg_ref, kg_ref, q_ref, k_ref, v_ref):
    d = x_ref.shape[1]
    h = _rmsnorm(x_ref[...], g_ref[...]).astype(BF16)
    qkv = _dot(h, w_ref[...])
    li = lax.broadcasted_iota(jnp.int32, (LANES, LANES), 0) // HEAD_DIM
    lj = lax.broadcasted_iota(jnp.int32, (LANES, LANES), 1) // HEAD_DIM
    same_head = jnp.where(li == lj, 1.0, 0.0).astype(BF16)

    def head_norm(v, gain, out_scale):
        parts = []
        for j in range(d // LANES):
            vj = v[:, j * LANES:(j + 1) * LANES]
            hi, lo = _split_bf16(vj * vj)
            ss = _dot(hi, same_head) + _dot(lo, same_head)
            parts.append(vj * lax.rsqrt(ss * (1.0 / HEAD_DIM) + RMS_EPS))
        return (jnp.concatenate(parts, axis=-1) * gain * out_scale).astype(BF16)

    q_ref[...] = head_norm(qkv[:, :d], qg_ref[...], HEAD_DIM ** -0.5 * LOG2_E)
    k_ref[...] = head_norm(qkv[:, d:2 * d], kg_ref[...], 1.0)
    v_ref[...] = qkv[:, 2 * d:].astype(BF16)


def _qkv(x2d, g, w_qkv, q_gain, k_gain):
    t, d = x2d.shape
    tt = min(TOK_TILE, t)
    row = lambda i: (i, 0)
    fixed = lambda i: (0, 0)
    return pl.pallas_call(
        _qkv_kernel,
        grid=(t // tt,),
        in_specs=[
            pl.BlockSpec((tt, d), row),
            pl.BlockSpec((1, d), fixed),
            pl.BlockSpec((d, 3 * d), fixed),
            pl.BlockSpec((1, d), fixed),
            pl.BlockSpec((1, d), fixed),
        ],
        out_specs=[pl.BlockSpec((tt, d), row)] * 3,
        out_shape=[jax.ShapeDtypeStruct((t, d), BF16)] * 3,
        compiler_params=pltpu.CompilerParams(dimension_semantics=("arbitrary",), vmem_limit_bytes=VMEM_LIMIT),
        name="attn_qkv",
    )(x2d, g.reshape(1, d), w_qkv.astype(BF16), jnp.tile(q_gain, N_HEADS).reshape(1, d),
      jnp.tile(k_gain, N_HEADS).reshape(1, d))


def _attn_kernel(q_ref, k_ref, v_ref, o_ref, acc_ref, carry_ref):
    qi = pl.program_id(2)
    tq, kb = ATT_Q, ATT_K
    q = q_ref[0]
    head_of_lane = lax.broadcasted_iota(jnp.int32, (tq, LANES), 1) // HEAD_DIM
    q_heads = [jnp.where(head_of_lane == hd, q, jnp.zeros_like(q)) for hd in range(2)]
    r2 = lax.broadcasted_iota(jnp.int32, (2 * kb, 2 * kb), 0)
    c2 = lax.broadcasted_iota(jnp.int32, (2 * kb, 2 * kb), 1)
    suffix2 = jnp.where(jnp.logical_and(r2 // kb == c2 // kb, r2 >= c2), 1.0, 0.0).astype(BF16)

    acc_ref[...] = jnp.zeros(acc_ref.shape, F32)
    carry_ref[...] = jnp.zeros(carry_ref.shape, F32)

    def key_group(j0, diag_offsets, r0):
        nb = len(diag_offsets)
        rows = tq - r0
        start = pl.multiple_of(j0 * kb, kb)
        kk = k_ref[0, pl.ds(start, nb * kb), :]
        vv = v_ref[0, pl.ds(start, nb * kb), :]
        row = lax.broadcasted_iota(jnp.int32, (rows, kb), 0) + r0
        key = lax.broadcasted_iota(jnp.int32, (rows, kb), 1)
        z_all = [_dot_nt(q_heads[hd][r0:, :], kk) for hd in range(2)]
        carry = [carry_ref[hd, r0:, :] for hd in range(2)]
        probs = [[None] * nb for _ in range(2)]
        for b in reversed(range(nb)):
            causal = None if diag_offsets[b] is None else key + diag_offsets[b] < row
            zs, log_keeps = [], []
            for hd in range(2):
                z = z_all[hd][:, b * kb:(b + 1) * kb]
                neg = -z
                log_keep = jnp.minimum(neg, 0.0) - jnp.log(1.0 + jnp.exp2(jnp.minimum(z, neg))) * LOG2_E
                if causal is not None:
                    log_keep = jnp.where(causal, log_keep, 0.0)
                zs.append(z)
                log_keeps.append(log_keep.astype(BF16))
            sums = _dot(jnp.concatenate(log_keeps, axis=1), suffix2)
            for hd in range(2):
                s_in = sums[:, hd * kb:(hd + 1) * kb]
                a = jnp.exp2(zs[hd] + s_in + carry[hd])
                if causal is not None:
                    a = jnp.where(causal, a, 0.0)
                probs[hd][b] = a.astype(BF16)
                carry[hd] = carry[hd] + jnp.broadcast_to(s_in[:, 0:1], (rows, kb))
        for hd in range(2):
            carry_ref[hd, r0:, :] = carry[hd]
            acc_ref[hd, r0:, :] += _dot(jnp.concatenate(probs[hd], axis=1), vv)

    blocks_per_q = tq // kb
    for g in reversed(range(blocks_per_q // ATT_DIAG_GROUP)):
        offs = [(g * ATT_DIAG_GROUP + b) * kb for b in range(ATT_DIAG_GROUP)]
        key_group(qi * blocks_per_q + g * ATT_DIAG_GROUP, offs, offs[0])

    def body(s, c):
        key_group(qi * blocks_per_q - (s + 1) * ATT_GROUP, [None] * ATT_GROUP, 0)
        return c
    lax.fori_loop(0, qi * (blocks_per_q // ATT_GROUP), body, 0)
    o_ref[0] = jnp.where(head_of_lane == 0, acc_ref[0], acc_ref[1]).astype(BF16)


def _attention(q, k, v, b, s):
    d = q.shape[-1]
    q3, k3, v3 = (a.reshape(b, s, d) for a in (q, k, v))
    pairs = d // LANES
    o = pl.pallas_call(
        _attn_kernel,
        grid=(b, pairs, s // ATT_Q),
        in_specs=[
            pl.BlockSpec((1, ATT_Q, LANES), lambda bi, hp, i: (bi, i, hp)),
            pl.BlockSpec((1, s, LANES), lambda bi, hp, i: (bi, 0, hp)),
            pl.BlockSpec((1, s, LANES), lambda bi, hp, i: (bi, 0, hp)),
        ],
        out_specs=pl.BlockSpec((1, ATT_Q, LANES), lambda bi, hp, i: (bi, i, hp)),
        out_shape=jax.ShapeDtypeStruct((b, s, d), BF16),
        scratch_shapes=[pltpu.VMEM((2, ATT_Q, LANES), F32), pltpu.VMEM((2, ATT_Q, LANES), F32)],
        compiler_params=pltpu.CompilerParams(
            dimension_semantics=("arbitrary", "arbitrary", "arbitrary"), vmem_limit_bytes=VMEM_LIMIT),
        name="attn_core",
    )(q3, k3, v3)
    return o.reshape(b * s, d)


def _out_proj_kernel(x_ref, o_ref, w_ref, y_ref):
    y_ref[...] = x_ref[...] + _dot(o_ref[...], w_ref[...])


def _out_proj(x2d, o, w_o):
    t, d = x2d.shape
    tt = min(TOK_TILE, t)
    return pl.pallas_call(
        _out_proj_kernel,
        grid=(t // tt,),
        in_specs=[
            pl.BlockSpec((tt, d), lambda i: (i, 0)),
            pl.BlockSpec((tt, d), lambda i: (i, 0)),
            pl.BlockSpec((d, d), lambda i: (0, 0)),
        ],
        out_specs=pl.BlockSpec((tt, d), lambda i: (i, 0)),
        out_shape=jax.ShapeDtypeStruct((t, d), F32),
        compiler_params=pltpu.CompilerParams(dimension_semantics=("arbitrary",), vmem_limit_bytes=VMEM_LIMIT),
        name="attn_out_proj",
    )(x2d, o, w_o.astype(BF16))


def _attention_layer(x2d, b, s, g, w_qkv, q_gain, k_gain, w_o):
    q, k, v = _qkv(x2d, g, w_qkv, q_gain, k_gain)
    o = _attention(q, k, v, b, s)
    return _out_proj(x2d, o, w_o)


def kernel(x, mix_norm, pool_w, pool_scale, w_qkv, q_norm, k_norm, w_o, ffn_norm, router_w, router_b,
           w_gate_up, b_gate_up, w_down, b_down):
    b, s, d = x.shape
    assert d == D_MODEL and s % ATT_Q == 0 and (b * s) % (CHUNK * ROUTE_CHUNKS) == 0
    depth = mix_norm.shape[0]
    for i in range(depth):
        j = i // 2
        if i % 2 == 0:
            x2d = _pool_layer(x, mix_norm[i], pool_w[j], pool_scale[j]).reshape(b * s, d)
        else:
            x2d = _attention_layer(x.reshape(b * s, d), b, s, mix_norm[i], w_qkv[j], q_norm[j], k_norm[j], w_o[j])
        x2d = _moe_layer(x2d, ffn_norm[i], router_w[i], router_b[i], i, w_gate_up, b_gate_up, w_down, b_down)
        x = x2d.reshape(b, s, d)
    return x
```

```python
import jax
import jax.numpy as jnp
from jax import lax
from jax.experimental import pallas as pl
from jax.experimental.pallas import tpu as pltpu

D_MODEL = 1024
POOL_WINDOWS = (2, 4, 8, 16)
POOL_GROUP_DIM = D_MODEL // len(POOL_WINDOWS)
POOL_HALO = 16
HEAD_DIM = 64
N_HEADS = D_MODEL // HEAD_DIM
N_EXPERTS = 32
TOP_K = 4
D_FF = D_MODEL
SWIGLU_LIMIT = 7.0
SWIGLU_ALPHA = 1.702
RMS_EPS = 1e-6
LOG2_E = 1.4426950408889634

LANES = 128
SUBLANES = 8
VMEM_LIMIT = 56 * 1024 * 1024

CHUNK = 256
SEG_ALIGN = 2 * SUBLANES
CHUNK_ROWS = CHUNK * TOP_K + N_EXPERTS * SEG_ALIGN
GEMM_TILE = 512
GEMM_SUB = 256
TOK_TILE = 512
ATT_Q = 512
ATT_K = LANES
ATT_GROUP = 4
ATT_DIAG_GROUP = 2

F32 = jnp.float32
BF16 = jnp.bfloat16


def _dot(a, b):
    return jnp.dot(a, b, preferred_element_type=F32)


def _dot_nt(a, b):
    return lax.dot_general(a, b, (((1,), (1,)), ((), ())), preferred_element_type=F32)


def _split_bf16(v):
    hi = v.astype(BF16)
    lo = (v - hi.astype(F32)).astype(BF16)
    return hi, lo


def _rmsnorm(v, g):
    return v * lax.rsqrt(jnp.mean(v * v, axis=-1, keepdims=True) + RMS_EPS) * g


def _pool_route_kernel(x_ref, halo_ref, g_ref, w_ref, scale_ref, rg_ref, wt_hi_ref, wt_lo_ref, rb_ref,
                       o_ref, hb_ref, slot_ref, gate_ref, cnt_ref):
    x1 = _pool_tile(x_ref[0], halo_ref[0], pl.program_id(1), g_ref, w_ref, scale_ref)
    o_ref[0] = x1
    _route_rows(x1, rg_ref, wt_hi_ref, wt_lo_ref, rb_ref, hb_ref, slot_ref, gate_ref, cnt_ref)


def _pool_tile(x, halo, i, g_ref, w_ref, scale_ref):
    g = g_ref[...]
    h = _rmsnorm(x, g)
    hh = _rmsnorm(halo, g)
    hh = jnp.where(i > 0, hh, 0.0)
    a = jnp.concatenate([hh, h], axis=0)
    ts = x.shape[0]
    pos = i * ts + lax.broadcasted_iota(jnp.int32, (ts, 1), 0)
    outs = []
    for gi, w in enumerate(POOL_WINDOWS):
        lo, hi = gi * POOL_GROUP_DIM, (gi + 1) * POOL_GROUP_DIM
        s = a[:, lo:hi]
        span = 1
        while span < w:
            s = s + pltpu.roll(s, span, axis=0)
            span *= 2
        cnt = jnp.minimum(pos + 1, w).astype(F32)
        pooled = s[POOL_HALO:, :] / cnt - h[:, lo:hi]
        outs.append(_dot(pooled.astype(BF16), w_ref[gi]))
    return x + jnp.concatenate(outs, axis=-1) * scale_ref[...]


def _route_specs(d, step_of):
    fixed2 = lambda *_: (0, 0)
    per_step = TOK_TILE // CHUNK
    in_specs = [
        pl.BlockSpec((1, d), fixed2),
        pl.BlockSpec((N_EXPERTS, d), fixed2),
        pl.BlockSpec((N_EXPERTS, d), fixed2),
        pl.BlockSpec((N_EXPERTS, 1), fixed2),
    ]
    out_specs = [
        pl.BlockSpec((TOK_TILE, d), lambda *ids: (step_of(*ids), 0)),
        pl.BlockSpec((per_step, SUBLANES, CHUNK), lambda *ids: (step_of(*ids), 0, 0)),
        pl.BlockSpec((per_step, SUBLANES, CHUNK), lambda *ids: (step_of(*ids), 0, 0)),
        pl.BlockSpec((per_step, N_EXPERTS, LANES), lambda *ids: (step_of(*ids), 0, 0)),
    ]
    return in_specs, out_specs


def _route_args(g, router_w, router_b):
    wt_hi, wt_lo = _split_bf16(router_w.T)
    return g.reshape(1, -1), wt_hi, wt_lo, router_b.reshape(N_EXPERTS, 1)


def _route_shapes(t, d):
    nc = t // CHUNK
    return [
        jax.ShapeDtypeStruct((t, d), BF16),
        jax.ShapeDtypeStruct((nc, SUBLANES, CHUNK), jnp.int32),
        jax.ShapeDtypeStruct((nc, SUBLANES, CHUNK), F32),
        jax.ShapeDtypeStruct((nc, N_EXPERTS, LANES), jnp.int32),
    ]


def _pool_route_layer(x, g, w, scale, ffn_g, router_w, router_b):
    b, s, d = x.shape
    ts = TOK_TILE
    tiles_per_seq = s // ts
    blocks_per_tile = ts // POOL_HALO
    route_in, route_out = _route_specs(d, lambda bi, i: bi * tiles_per_seq + i)
    outs = pl.pallas_call(
        _pool_route_kernel,
        grid=(b, tiles_per_seq),
        in_specs=[
            pl.BlockSpec((1, ts, d), lambda bi, i: (bi, i, 0)),
            pl.BlockSpec((1, POOL_HALO, d), lambda bi, i: (bi, jnp.maximum(i * blocks_per_tile - 1, 0), 0)),
            pl.BlockSpec((1, d), lambda bi, i: (0, 0)),
            pl.BlockSpec((len(POOL_WINDOWS), POOL_GROUP_DIM, POOL_GROUP_DIM), lambda bi, i: (0, 0, 0)),
            pl.BlockSpec((1, d), lambda bi, i: (0, 0)),
        ] + route_in,
        out_specs=[pl.BlockSpec((1, ts, d), lambda bi, i: (bi, i, 0))] + route_out,
        out_shape=[jax.ShapeDtypeStruct(x.shape, F32)] + _route_shapes(b * s, d),
        compiler_params=pltpu.CompilerParams(
            dimension_semantics=("arbitrary", "arbitrary"), vmem_limit_bytes=VMEM_LIMIT),
        name="pool_route",
    )(x, x, g.reshape(1, d), w.astype(BF16), scale.reshape(1, d), *_route_args(ffn_g, router_w, router_b))
    return outs[0].reshape(b * s, d), outs[1:]


def _route_rows(x, g_ref, wt_hi_ref, wt_lo_ref, b_ref, hb_ref, slot_ref, gate_ref, cnt_ref):
    for ci in range(TOK_TILE // CHUNK):
        _route_chunk(ci, x[ci * CHUNK:(ci + 1) * CHUNK, :], g_ref, wt_hi_ref, wt_lo_ref, b_ref,
                     hb_ref, slot_ref, gate_ref, cnt_ref)


def _route_chunk(ci, x, g_ref, wt_hi_ref, wt_lo_ref, b_ref, hb_ref, slot_ref, gate_ref, cnt_ref):
    c = CHUNK
    h = _rmsnorm(x, g_ref[...])
    h_hi, h_lo = _split_bf16(h)
    hb_ref[pl.ds(ci * CHUNK, CHUNK), :] = h_hi
    logits = (_dot_nt(wt_hi_ref[...], h_hi) + _dot_nt(wt_lo_ref[...], h_hi)
              + _dot_nt(wt_hi_ref[...], h_lo) + b_ref[...])
    eio = lax.broadcasted_iota(jnp.int32, (N_EXPERTS, c), 0).astype(F32)
    vals, idxs = [], []
    l = logits
    for _ in range(TOP_K):
        m = jnp.max(l, axis=0, keepdims=True)
        idx = jnp.min(jnp.where(l == m, eio, float(N_EXPERTS)), axis=0, keepdims=True)
        vals.append(m)
        idxs.append(idx)
        l = jnp.where(eio == idx, -jnp.inf, l)
    es = [jnp.exp(v - vals[0]) for v in vals]
    denom = es[0] + es[1] + es[2] + es[3]
    sel = jnp.zeros((N_EXPERTS, c), F32)
    for idx in idxs:
        sel = sel + jnp.where(eio == idx, 1.0, 0.0)
    upper = jnp.where(lax.broadcasted_iota(jnp.int32, (c, c), 0) <= lax.broadcasted_iota(jnp.int32, (c, c), 1),
                      1.0, 0.0).astype(BF16)
    rank = _dot(sel.astype(BF16), upper)
    n = jnp.broadcast_to(rank[:, c - 1:c], (N_EXPERTS, c))
    n_units = jnp.floor((n + (SEG_ALIGN - 1)) * (1.0 / SEG_ALIGN))
    strict_lower = jnp.where(lax.broadcasted_iota(jnp.int32, (N_EXPERTS, N_EXPERTS), 1)
                             < lax.broadcasted_iota(jnp.int32, (N_EXPERTS, N_EXPERTS), 0), 1.0, 0.0).astype(BF16)
    off = _dot(strict_lower, n_units.astype(BF16)) * SEG_ALIGN
    dest = off + rank - 1.0
    for k in range(TOP_K):
        slot = jnp.sum(jnp.where(eio == idxs[k], dest, 0.0), axis=0, keepdims=True)
        slot_ref[ci, k:k + 1, :] = slot.astype(jnp.int32)
        gate_ref[ci, k:k + 1, :] = es[k] / denom
    slot_ref[ci, TOP_K:, :] = jnp.zeros((SUBLANES - TOP_K, c), jnp.int32)
    gate_ref[ci, TOP_K:, :] = jnp.zeros((SUBLANES - TOP_K, c), F32)
    cnt_ref[ci] = (n_units[:, :LANES] * SEG_ALIGN).astype(jnp.int32)


def _segment_tables(cnt, n_tiles_max):
    seg_n = cnt[:, :, 0]
    seg_loc = jnp.cumsum(seg_n, axis=1) - seg_n
    tot = jnp.sum(seg_n, axis=0)
    tiles = (tot + GEMM_TILE - 1) // GEMM_TILE
    tile_end = jnp.cumsum(tiles)
    tile_start = tile_end - tiles
    front_pad = tiles * GEMM_TILE - tot
    seg_dst = (tile_start * GEMM_TILE + front_pad)[None, :] + jnp.cumsum(seg_n, axis=0) - seg_n
    used = tile_end[-1]
    tile_ids = jnp.arange(n_tiles_max, dtype=jnp.int32)
    tile_expert = jnp.sum(tile_ids[:, None] >= tile_end[None, :], axis=1).astype(jnp.int32)
    last_expert = jnp.sum(jnp.maximum(used - 1, 0) >= tile_end).astype(jnp.int32)
    tile_expert = jnp.where(tile_ids < used, tile_expert, last_expert)
    owner = tile_expert[:, None] == jnp.arange(N_EXPERTS, dtype=jnp.int32)[None, :]
    pad_here = jnp.sum(jnp.where(owner, jnp.where(tile_ids[:, None] == tile_start[None, :], front_pad[None, :], 0), 0),
                       axis=1)
    tile_valid = jnp.where(tile_ids < used, GEMM_TILE - pad_here, 0)
    i32 = lambda v: v.astype(jnp.int32).reshape(-1)
    gap = jnp.concatenate([tile_start * GEMM_TILE, front_pad, used[None]])
    seg_tot = jnp.sum(seg_n, axis=1)
    return i32(seg_loc), i32(seg_n), i32(seg_dst), i32(seg_tot), tile_expert, i32(tile_valid), i32(used), i32(gap)


def _segment_copy(loc_ref, n_ref, dst_ref, c, e, local_buf, sorted_hbm, sem, to_sorted):
    j = c * N_EXPERTS + e
    n = pl.multiple_of(n_ref[j], SEG_ALIGN)
    local = local_buf.at[pl.ds(pl.multiple_of(loc_ref[j], SEG_ALIGN), n)]
    remote = sorted_hbm.at[pl.ds(pl.multiple_of(dst_ref[j], SEG_ALIGN), n)]
    return n, (pltpu.make_async_copy(local, remote, sem) if to_sorted
               else pltpu.make_async_copy(remote, local, sem))


def _for_each_segment(loc_ref, n_ref, dst_ref, c, local_buf, sorted_hbm, sem, to_sorted, action):
    def body(e, carry):
        n, cp = _segment_copy(loc_ref, n_ref, dst_ref, c, e, local_buf, sorted_hbm, sem, to_sorted)

        @pl.when(n > 0)
        def _():
            action(cp)
        return carry
    lax.fori_loop(0, N_EXPERTS, body, 0)


def _wait_chunk(tot_ref, c, local_buf, sorted_hbm, sem, to_sorted):
    n = pl.multiple_of(tot_ref[c], SEG_ALIGN)
    local = local_buf.at[pl.ds(0, n)]
    remote = sorted_hbm.at[pl.ds(0, n)]
    (pltpu.make_async_copy(local, remote, sem) if to_sorted else pltpu.make_async_copy(remote, local, sem)).wait()


def _zero_unused_rows(gap_ref, xs_hbm, zeros, sem, action):
    def expert_gap(e, carry):
        n = pl.multiple_of(gap_ref[N_EXPERTS + e], SEG_ALIGN)

        @pl.when(n > 0)
        def _():
            action(pltpu.make_async_copy(
                zeros.at[pl.ds(0, n)], xs_hbm.at[pl.ds(pl.multiple_of(gap_ref[e], SEG_ALIGN), n)], sem))
        return carry
    lax.fori_loop(0, N_EXPERTS, expert_gap, 0)

    def unused_tile(i, carry):
        action(pltpu.make_async_copy(zeros, xs_hbm.at[pl.ds(pl.multiple_of(i * GEMM_TILE, GEMM_TILE), GEMM_TILE)], sem))
        return carry
    lax.fori_loop(gap_ref[2 * N_EXPERTS], xs_hbm.shape[0] // GEMM_TILE, unused_tile, 0)


def _dispatch_kernel(loc_ref, n_ref, dst_ref, tot_ref, gap_ref, hb_ref, slot_ref, xs_hbm, buf, zeros, sems):
    c = pl.program_id(0)
    nc = pl.num_programs(0)
    par = c % 2
    start = lambda cp: cp.start()
    wait = lambda cp: cp.wait()

    @pl.when(c == 0)
    def _():
        zeros[...] = jnp.zeros(zeros.shape, BF16)
        _zero_unused_rows(gap_ref, xs_hbm, zeros, sems.at[2], start)

    @pl.when(c >= 2)
    def _():
        _wait_chunk(tot_ref, c - 2, buf.at[par], xs_hbm, sems.at[par], True)

    rows = lax.broadcasted_iota(jnp.int32, (CHUNK_ROWS, CHUNK), 0)
    hit = rows == slot_ref[0, 0:1, :]
    for k in range(1, TOP_K):
        hit = jnp.logical_or(hit, rows == slot_ref[0, k:k + 1, :])
    onehot = jnp.where(hit, 1.0, 0.0).astype(BF16)
    buf[par] = _dot(onehot, hb_ref[...]).astype(BF16)
    _for_each_segment(loc_ref, n_ref, dst_ref, c, buf.at[par], xs_hbm, sems.at[par], True, start)

    @pl.when(c == nc - 1)
    def _():
        _zero_unused_rows(gap_ref, xs_hbm, zeros, sems.at[2], wait)

        @pl.when(c >= 1)
        def _():
            _wait_chunk(tot_ref, c - 1, buf.at[1 - par], xs_hbm, sems.at[1 - par], True)
        _wait_chunk(tot_ref, c, buf.at[par], xs_hbm, sems.at[par], True)


def _dispatch(hb, slots, seg_loc, seg_n, seg_dst, seg_tot, gap, n_rows):
    t, d = hb.shape
    nc = t // CHUNK
    return pl.pallas_call(
        _dispatch_kernel,
        grid_spec=pltpu.PrefetchScalarGridSpec(
            num_scalar_prefetch=5,
            grid=(nc,),
            in_specs=[
                pl.BlockSpec((CHUNK, d), lambda c, *_: (c, 0)),
                pl.BlockSpec((1, SUBLANES, CHUNK), lambda c, *_: (c, 0, 0)),
            ],
            out_specs=pl.BlockSpec(memory_space=pl.ANY),
            scratch_shapes=[pltpu.VMEM((2, CHUNK_ROWS, d), BF16), pltpu.VMEM((GEMM_TILE, d), BF16),
                            pltpu.SemaphoreType.DMA((3,))],
        ),
        out_shape=jax.ShapeDtypeStruct((n_rows, d), BF16),
        compiler_params=pltpu.CompilerParams(dimension_semantics=("arbitrary",), vmem_limit_bytes=VMEM_LIMIT),
        name="moe_dispatch",
    )(seg_loc, seg_n, seg_dst, seg_tot, gap, hb, slots)


def _expert_kernel(te_ref, valid_ref, used_ref, x_ref, wgu_ref, bgu_ref, wd_ref, bd_ref, o_ref, wgu_bf, wd_bf):
    i = pl.program_id(0)
    prev = te_ref[jnp.maximum(i - 1, 0)]

    @pl.when(jnp.logical_or(i == 0, te_ref[i] != prev))
    def _():
        wgu_bf[...] = wgu_ref[0].astype(BF16)
        wd_bf[...] = wd_ref[0].astype(BF16)

    for sub in range(GEMM_TILE // GEMM_SUB):
        rows = pl.ds(sub * GEMM_SUB, GEMM_SUB)
        rows_after = GEMM_TILE - (sub + 1) * GEMM_SUB

        @pl.when(valid_ref[i] > rows_after)
        def _():
            gu = _dot(x_ref[rows, :], wgu_bf[...]) + bgu_ref[0]
            gate = jnp.minimum(gu[:, :D_FF], SWIGLU_LIMIT)
            up = jnp.clip(gu[:, D_FF:], -SWIGLU_LIMIT, SWIGLU_LIMIT)
            act = (up + 1.0) * (gate * (1.0 / (1.0 + jnp.exp(-SWIGLU_ALPHA * gate))))
            out = _dot(act.astype(BF16), wd_bf[...]) + bd_ref[0]
            o_ref[rows, :] = out.astype(BF16)

        @pl.when(valid_ref[i] <= rows_after)
        def _():
            o_ref[rows, :] = jnp.zeros((GEMM_SUB, D_MODEL), BF16)


def _experts(xs, tile_expert, tile_valid, used, layer, w_gate_up, b_gate_up, w_down, b_down):
    n_rows, d = xs.shape
    n_tiles = n_rows // GEMM_TILE
    n_stacked = w_gate_up.shape[0] * N_EXPERTS
    row_map = lambda i, te, tv, u: (jnp.minimum(i, jnp.maximum(u[0] - 1, 0)), 0)
    exp_map = lambda i, te, tv, u: (layer * N_EXPERTS + te[i], 0, 0)
    return pl.pallas_call(
        _expert_kernel,
        grid_spec=pltpu.PrefetchScalarGridSpec(
            num_scalar_prefetch=3,
            grid=(n_tiles,),
            in_specs=[
                pl.BlockSpec((GEMM_TILE, d), row_map),
                pl.BlockSpec((1, d, 2 * D_FF), exp_map),
                pl.BlockSpec((1, 1, 2 * D_FF), exp_map),
                pl.BlockSpec((1, D_FF, d), exp_map),
                pl.BlockSpec((1, 1, d), exp_map),
            ],
            out_specs=pl.BlockSpec((GEMM_TILE, d), lambda i, te, tv, u: (i, 0)),
            scratch_shapes=[pltpu.VMEM((d, 2 * D_FF), BF16), pltpu.VMEM((D_FF, d), BF16)],
        ),
        out_shape=jax.ShapeDtypeStruct((n_rows, d), BF16),
        compiler_params=pltpu.CompilerParams(dimension_semantics=("arbitrary",), vmem_limit_bytes=VMEM_LIMIT),
        name="moe_experts",
    )(tile_expert, tile_valid, used, xs, w_gate_up.reshape(n_stacked, d, 2 * D_FF),
      b_gate_up.reshape(n_stacked, 1, 2 * D_FF), w_down.reshape(n_stacked, D_FF, d), b_down.reshape(n_stacked, 1, d))


def _combine_kernel(loc_ref, n_ref, dst_ref, tot_ref, x_ref, slot_t_ref, gate_t_ref, ys_hbm, o_ref, buf, sems):
    c = pl.program_id(0)
    nc = pl.num_programs(0)
    par = c % 2
    start = lambda cp: cp.start()
    wait = lambda cp: cp.wait()

    @pl.when(c == 0)
    def _():
        buf[...] = jnp.zeros(buf.shape, BF16)
        _for_each_segment(loc_ref, n_ref, dst_ref, c, buf.at[0], ys_hbm, sems.at[0], False, start)

    @pl.when(c + 1 < nc)
    def _():
        _for_each_segment(loc_ref, n_ref, dst_ref, c + 1, buf.at[1 - par], ys_hbm, sems.at[1 - par], False, start)

    cols = lax.broadcasted_iota(jnp.int32, (CHUNK, CHUNK_ROWS), 1)
    st = slot_t_ref[0]
    gt = gate_t_ref[0]
    back = jnp.zeros((CHUNK, CHUNK_ROWS), F32)
    for k in range(TOP_K):
        back = jnp.where(cols == st[:, k:k + 1], gt[:, k:k + 1], back)
    back = back.astype(BF16)

    _wait_chunk(tot_ref, c, buf.at[par], ys_hbm, sems.at[par], False)
    ys = buf[par]
    o_ref[...] = x_ref[...] + _dot(back, ys)


def _combine(x2d, ys, slots_t, gates_t, seg_loc, seg_n, seg_dst, seg_tot):
    t, d = x2d.shape
    nc = t // CHUNK
    return pl.pallas_call(
        _combine_kernel,
        grid_spec=pltpu.PrefetchScalarGridSpec(
            num_scalar_prefetch=4,
            grid=(nc,),
            in_specs=[
                pl.BlockSpec((CHUNK, d), lambda c, *_: (c, 0)),
                pl.BlockSpec((1, CHUNK, TOP_K), lambda c, *_: (c, 0, 0)),
                pl.BlockSpec((1, CHUNK, TOP_K), lambda c, *_: (c, 0, 0)),
                pl.BlockSpec(memory_space=pl.ANY),
            ],
            out_specs=pl.BlockSpec((CHUNK, d), lambda c, *_: (c, 0)),
            scratch_shapes=[pltpu.VMEM((2, CHUNK_ROWS, d), BF16), pltpu.SemaphoreType.DMA((2,))],
        ),
        out_shape=jax.ShapeDtypeStruct((t, d), F32),
        compiler_params=pltpu.CompilerParams(dimension_semantics=("arbitrary",), vmem_limit_bytes=VMEM_LIMIT),
        name="moe_combine",
    )(seg_loc, seg_n, seg_dst, seg_tot, x2d, slots_t, gates_t, ys)


def _moe_layer(x2d, routed, layer, w_gate_up, b_gate_up, w_down, b_down):
    t, d = x2d.shape
    nc = t // CHUNK
    n_tiles_max = (t * TOP_K + nc * N_EXPERTS * (SEG_ALIGN - 1)) // GEMM_TILE + N_EXPERTS
    hb, slots, gates, cnt = routed
    seg_loc, seg_n, seg_dst, seg_tot, tile_expert, tile_valid, used, gap = _segment_tables(cnt, n_tiles_max)
    xs = _dispatch(hb, slots, seg_loc, seg_n, seg_dst, seg_tot, gap, n_tiles_max * GEMM_TILE)
    ys = _experts(xs, tile_expert, tile_valid, used, layer, w_gate_up, b_gate_up, w_down, b_down)
    slots_t = jnp.swapaxes(slots[:, :TOP_K, :], 1, 2)
    gates_t = jnp.swapaxes(gates[:, :TOP_K, :], 1, 2)
    return _combine(x2d, ys, slots_t, gates_t, seg_loc, seg_n, seg_dst, seg_tot)


def _qkv_kernel(x_ref, g_ref, w_ref, qg_ref, kg_ref, q_ref, k_ref, v_ref):
    d = x_ref.shape[1]
    h = _rmsnorm(x_ref[...], g_ref[...]).astype(BF16)
    qkv = _dot(h, w_ref[...])
    width = 2 * LANES
    li = lax.broadcasted_iota(jnp.int32, (width, width), 0) // HEAD_DIM
    lj = lax.broadcasted_iota(jnp.int32, (width, width), 1) // HEAD_DIM
    same_head = jnp.where(li == lj, 1.0, 0.0).astype(BF16)

    def head_norm(v, gain, out_scale):
        parts = []
        for j in range(d // width):
            vj = v[:, j * width:(j + 1) * width]
            ss = _dot((vj * vj).astype(BF16), same_head)
            parts.append(vj * lax.rsqrt(ss * (1.0 / HEAD_DIM) + RMS_EPS))
        return (jnp.concatenate(parts, axis=-1) * gain * out_scale).astype(BF16)

    q_ref[...] = head_norm(qkv[:, :d], qg_ref[...], HEAD_DIM ** -0.5 * LOG2_E)
    k_ref[...] = head_norm(qkv[:, d:2 * d], kg_ref[...], 1.0)
    v_ref[...] = qkv[:, 2 * d:].astype(BF16)


def _qkv(x2d, g, w_qkv, q_gain, k_gain):
    t, d = x2d.shape
    tt = min(TOK_TILE, t)
    row = lambda i: (i, 0)
    fixed = lambda i: (0, 0)
    return pl.pallas_call(
        _qkv_kernel,
        grid=(t // tt,),
        in_specs=[
            pl.BlockSpec((tt, d), row),
            pl.BlockSpec((1, d), fixed),
            pl.BlockSpec((d, 3 * d), fixed),
            pl.BlockSpec((1, d), fixed),
            pl.BlockSpec((1, d), fixed),
        ],
        out_specs=[pl.BlockSpec((tt, d), row)] * 3,
        out_shape=[jax.ShapeDtypeStruct((t, d), BF16)] * 3,
        compiler_params=pltpu.CompilerParams(dimension_semantics=("arbitrary",), vmem_limit_bytes=VMEM_LIMIT),
        name="attn_qkv",
    )(x2d, g.reshape(1, d), w_qkv.astype(BF16), jnp.tile(q_gain, N_HEADS).reshape(1, d),
      jnp.tile(k_gain, N_HEADS).reshape(1, d))


def _attn_kernel(q_ref, k_ref, v_ref, o_ref, acc_ref, carry_ref):
    qi = pl.program_id(2)
    tq, kb = ATT_Q, ATT_K
    q = q_ref[0]
    head_of_lane = lax.broadcasted_iota(jnp.int32, (tq, LANES), 1) // HEAD_DIM
    q_heads = [jnp.where(head_of_lane == hd, q, jnp.zeros_like(q)) for hd in range(2)]
    r2 = lax.broadcasted_iota(jnp.int32, (2 * kb, 2 * kb), 0)
    c2 = lax.broadcasted_iota(jnp.int32, (2 * kb, 2 * kb), 1)
    suffix2 = jnp.where(jnp.logical_and(r2 // kb == c2 // kb, r2 >= c2), 1.0, 0.0).astype(BF16)

    acc_ref[...] = jnp.zeros(acc_ref.shape, F32)
    carry_ref[...] = jnp.zeros(carry_ref.shape, F32)

    def key_group(j0, diag_offsets, r0):
        nb = len(diag_offsets)
        rows = tq - r0
        start = pl.multiple_of(j0 * kb, kb)
        kk = k_ref[0, pl.ds(start, nb * kb), :]
        vv = v_ref[0, pl.ds(start, nb * kb), :]
        row = lax.broadcasted_iota(jnp.int32, (rows, kb), 0) + r0
        key = lax.broadcasted_iota(jnp.int32, (rows, kb), 1)
        z_all = [_dot_nt(q_heads[hd][r0:, :], kk) for hd in range(2)]
        carry = [carry_ref[hd, r0:, :] for hd in range(2)]
        probs = [[None] * nb for _ in range(2)]
        for b in reversed(range(nb)):
            causal = None if diag_offsets[b] is None else key + diag_offsets[b] < row
            zs, log_keeps = [], []
            for hd in range(2):
                z = z_all[hd][:, b * kb:(b + 1) * kb]
                neg = -z
                log_keep = jnp.minimum(neg, 0.0) - jnp.log(1.0 + jnp.exp2(jnp.minimum(z, neg))) * LOG2_E
                if causal is not None:
                    log_keep = jnp.where(causal, log_keep, 0.0)
                zs.append(z)
                log_keeps.append(log_keep.astype(BF16))
            sums = _dot(jnp.concatenate(log_keeps, axis=1), suffix2)
            for hd in range(2):
                s_in = sums[:, hd * kb:(hd + 1) * kb]
                a = jnp.exp2(zs[hd] + s_in + carry[hd])
                if causal is not None:
                    a = jnp.where(causal, a, 0.0)
                probs[hd][b] = a.astype(BF16)
                carry[hd] = carry[hd] + jnp.broadcast_to(s_in[:, 0:1], (rows, kb))
        for hd in range(2):
            carry_ref[hd, r0:, :] = carry[hd]
            acc_ref[hd, r0:, :] += _dot(jnp.concatenate(probs[hd], axis=1), vv)

    blocks_per_q = tq // kb
    for g in reversed(range(blocks_per_q // ATT_DIAG_GROUP)):
        offs = [(g * ATT_DIAG_GROUP + b) * kb for b in range(ATT_DIAG_GROUP)]
        key_group(qi * blocks_per_q + g * ATT_DIAG_GROUP, offs, offs[0])

    def body(s, c):
        key_group(qi * blocks_per_q - (s + 1) * ATT_GROUP, [None] * ATT_GROUP, 0)
        return c
    lax.fori_loop(0, qi * (blocks_per_q // ATT_GROUP), body, 0)
    o_ref[0] = jnp.where(head_of_lane == 0, acc_ref[0], acc_ref[1]).astype(BF16)


def _attention(q, k, v, b, s):
    d = q.shape[-1]
    q3, k3, v3 = (a.reshape(b, s, d) for a in (q, k, v))
    pairs = d // LANES
    o = pl.pallas_call(
        _attn_kernel,
        grid=(b, pairs, s // ATT_Q),
        in_specs=[
            pl.BlockSpec((1, ATT_Q, LANES), lambda bi, hp, i: (bi, i, hp)),
            pl.BlockSpec((1, s, LANES), lambda bi, hp, i: (bi, 0, hp)),
            pl.BlockSpec((1, s, LANES), lambda bi, hp, i: (bi, 0, hp)),
        ],
        out_specs=pl.BlockSpec((1, ATT_Q, LANES), lambda bi, hp, i: (bi, i, hp)),
        out_shape=jax.ShapeDtypeStruct((b, s, d), BF16),
        scratch_shapes=[pltpu.VMEM((2, ATT_Q, LANES), F32), pltpu.VMEM((2, ATT_Q, LANES), F32)],
        compiler_params=pltpu.CompilerParams(
            dimension_semantics=("arbitrary", "arbitrary", "arbitrary"), vmem_limit_bytes=VMEM_LIMIT),
        name="attn_core",
    )(q3, k3, v3)
    return o.reshape(b * s, d)


def _proj_route_kernel(x_ref, o_ref, w_ref, rg_ref, wt_hi_ref, wt_lo_ref, rb_ref,
                       y_ref, hb_ref, slot_ref, gate_ref, cnt_ref):
    x2 = x_ref[...] + _dot(o_ref[...], w_ref[...])
    y_ref[...] = x2
    _route_rows(x2, rg_ref, wt_hi_ref, wt_lo_ref, rb_ref, hb_ref, slot_ref, gate_ref, cnt_ref)


def _proj_route(x2d, o, w_o, ffn_g, router_w, router_b):
    t, d = x2d.shape
    route_in, route_out = _route_specs(d, lambda c: c)
    outs = pl.pallas_call(
        _proj_route_kernel,
        grid=(t // TOK_TILE,),
        in_specs=[
            pl.BlockSpec((TOK_TILE, d), lambda c: (c, 0)),
            pl.BlockSpec((TOK_TILE, d), lambda c: (c, 0)),
            pl.BlockSpec((d, d), lambda c: (0, 0)),
        ] + route_in,
        out_specs=[pl.BlockSpec((TOK_TILE, d), lambda c: (c, 0))] + route_out,
        out_shape=[jax.ShapeDtypeStruct((t, d), F32)] + _route_shapes(t, d),
        compiler_params=pltpu.CompilerParams(dimension_semantics=("arbitrary",), vmem_limit_bytes=VMEM_LIMIT),
        name="attn_proj_route",
    )(x2d, o, w_o.astype(BF16), *_route_args(ffn_g, router_w, router_b))
    return outs[0], outs[1:]


def _attention_route_layer(x2d, b, s, g, w_qkv, q_gain, k_gain, w_o, ffn_g, router_w, router_b):
    q, k, v = _qkv(x2d, g, w_qkv, q_gain, k_gain)
    o = _attention(q, k, v, b, s)
    return _proj_route(x2d, o, w_o, ffn_g, router_w, router_b)


def kernel(x, mix_norm, pool_w, pool_scale, w_qkv, q_norm, k_norm, w_o, ffn_norm, router_w, router_b,
           w_gate_up, b_gate_up, w_down, b_down):
    b, s, d = x.shape
    assert d == D_MODEL and s % ATT_Q == 0 and s % TOK_TILE == 0 and TOK_TILE % CHUNK == 0
    depth = mix_norm.shape[0]
    for i in range(depth):
        j = i // 2
        if i % 2 == 0:
            x2d, routed = _pool_route_layer(x, mix_norm[i], pool_w[j], pool_scale[j],
                                            ffn_norm[i], router_w[i], router_b[i])
        else:
            x2d, routed = _attention_route_layer(x.reshape(b * s, d), b, s, mix_norm[i], w_qkv[j], q_norm[j],
                                                 k_norm[j], w_o[j], ffn_norm[i], router_w[i], router_b[i])
        x2d = _moe_layer(x2d, routed, i, w_gate_up, b_gate_up, w_down, b_down)
        x = x2d.reshape(b, s, d)
    return x
```

```python
import functools

import jax
import jax.numpy as jnp
from jax import lax
from jax.experimental import pallas as pl
from jax.experimental.pallas import tpu as pltpu

D_MODEL = 1024
POOL_WINDOWS = (2, 4, 8, 16)
POOL_GROUP_DIM = D_MODEL // len(POOL_WINDOWS)
POOL_HALO = 16
HEAD_DIM = 64
N_HEADS = D_MODEL // HEAD_DIM
N_EXPERTS = 32
TOP_K = 4
D_FF = D_MODEL
SWIGLU_LIMIT = 7.0
SWIGLU_ALPHA = 1.702
RMS_EPS = 1e-6
LOG2_E = 1.4426950408889634

LANES = 128
SUBLANES = 8
VMEM_LIMIT = 56 * 1024 * 1024

CHUNK = 256
SEG_ALIGN = 2 * SUBLANES
CHUNK_ROWS = CHUNK * TOP_K + N_EXPERTS * SEG_ALIGN
GEMM_TILE = 512
GEMM_SUB = 256
TOK_TILE = 512
ATT_Q = 512
ATT_K = LANES
ATT_GROUP = 4
ATT_DIAG_GROUP = 2

F32 = jnp.float32
BF16 = jnp.bfloat16


def _dot(a, b):
    return jnp.dot(a, b, preferred_element_type=F32)


def _dot_nt(a, b):
    return lax.dot_general(a, b, (((1,), (1,)), ((), ())), preferred_element_type=F32)


def _split_bf16(v):
    hi = v.astype(BF16)
    lo = (v - hi.astype(F32)).astype(BF16)
    return hi, lo


def _rmsnorm(v, g):
    return v * lax.rsqrt(jnp.mean(v * v, axis=-1, keepdims=True) + RMS_EPS) * g


def _pool_route_kernel(x_ref, halo_ref, g_ref, w_ref, scale_ref, rg_ref, wt_hi_ref, wt_lo_ref, rb_ref,
                       o_ref, hb_ref, slot_ref, gate_ref, cnt_ref):
    x1 = _pool_tile(x_ref[0], halo_ref[0], pl.program_id(1), g_ref, w_ref, scale_ref)
    o_ref[0] = x1
    _route_rows(x1, rg_ref, wt_hi_ref, wt_lo_ref, rb_ref, hb_ref, slot_ref, gate_ref, cnt_ref)


def _pool_tile(x, halo, i, g_ref, w_ref, scale_ref):
    g = g_ref[...]
    h = _rmsnorm(x, g)
    hh = _rmsnorm(halo, g)
    hh = jnp.where(i > 0, hh, 0.0)
    a = jnp.concatenate([hh, h], axis=0)
    ts = x.shape[0]
    pos = i * ts + lax.broadcasted_iota(jnp.int32, (ts, 1), 0)
    outs = []
    for gi, w in enumerate(POOL_WINDOWS):
        lo, hi = gi * POOL_GROUP_DIM, (gi + 1) * POOL_GROUP_DIM
        s = a[:, lo:hi]
        span = 1
        while span < w:
            s = s + pltpu.roll(s, span, axis=0)
            span *= 2
        cnt = jnp.minimum(pos + 1, w).astype(F32)
        pooled = s[POOL_HALO:, :] / cnt - h[:, lo:hi]
        outs.append(_dot(pooled.astype(BF16), w_ref[gi]))
    return x + jnp.concatenate(outs, axis=-1) * scale_ref[...]


def _route_specs(d, step_of):
    fixed2 = lambda *_: (0, 0)
    per_step = TOK_TILE // CHUNK
    in_specs = [
        pl.BlockSpec((1, d), fixed2),
        pl.BlockSpec((N_EXPERTS, d), fixed2),
        pl.BlockSpec((N_EXPERTS, d), fixed2),
        pl.BlockSpec((N_EXPERTS, 1), fixed2),
    ]
    out_specs = [
        pl.BlockSpec((TOK_TILE, d), lambda *ids: (step_of(*ids), 0)),
        pl.BlockSpec((per_step, SUBLANES, CHUNK), lambda *ids: (step_of(*ids), 0, 0)),
        pl.BlockSpec((per_step, SUBLANES, CHUNK), lambda *ids: (step_of(*ids), 0, 0)),
        pl.BlockSpec((per_step, N_EXPERTS, LANES), lambda *ids: (step_of(*ids), 0, 0)),
    ]
    return in_specs, out_specs


def _route_args(g, router_w, router_b):
    wt_hi, wt_lo = _split_bf16(router_w.T)
    return g.reshape(1, -1), wt_hi, wt_lo, router_b.reshape(N_EXPERTS, 1)


def _route_shapes(t, d):
    nc = t // CHUNK
    return [
        jax.ShapeDtypeStruct((t, d), BF16),
        jax.ShapeDtypeStruct((nc, SUBLANES, CHUNK), jnp.int32),
        jax.ShapeDtypeStruct((nc, SUBLANES, CHUNK), F32),
        jax.ShapeDtypeStruct((nc, N_EXPERTS, LANES), jnp.int32),
    ]


def _pool_route_layer(x, g, w, scale, ffn_g, router_w, router_b):
    b, s, d = x.shape
    ts = TOK_TILE
    tiles_per_seq = s // ts
    blocks_per_tile = ts // POOL_HALO
    route_in, route_out = _route_specs(d, lambda bi, i: bi * tiles_per_seq + i)
    outs = pl.pallas_call(
        _pool_route_kernel,
        grid=(b, tiles_per_seq),
        in_specs=[
            pl.BlockSpec((1, ts, d), lambda bi, i: (bi, i, 0)),
            pl.BlockSpec((1, POOL_HALO, d), lambda bi, i: (bi, jnp.maximum(i * blocks_per_tile - 1, 0), 0)),
            pl.BlockSpec((1, d), lambda bi, i: (0, 0)),
            pl.BlockSpec((len(POOL_WINDOWS), POOL_GROUP_DIM, POOL_GROUP_DIM), lambda bi, i: (0, 0, 0)),
            pl.BlockSpec((1, d), lambda bi, i: (0, 0)),
        ] + route_in,
        out_specs=[pl.BlockSpec((1, ts, d), lambda bi, i: (bi, i, 0))] + route_out,
        out_shape=[jax.ShapeDtypeStruct(x.shape, F32)] + _route_shapes(b * s, d),
        compiler_params=pltpu.CompilerParams(
            dimension_semantics=("arbitrary", "arbitrary"), vmem_limit_bytes=VMEM_LIMIT),
        name="pool_route",
    )(x, x, g.reshape(1, d), w.astype(BF16), scale.reshape(1, d), *_route_args(ffn_g, router_w, router_b))
    return outs[0].reshape(b * s, d), outs[1:]


def _route_rows(x, g_ref, wt_hi_ref, wt_lo_ref, b_ref, hb_ref, slot_ref, gate_ref, cnt_ref):
    for ci in range(TOK_TILE // CHUNK):
        _route_chunk(ci, x[ci * CHUNK:(ci + 1) * CHUNK, :], g_ref, wt_hi_ref, wt_lo_ref, b_ref,
                     hb_ref, slot_ref, gate_ref, cnt_ref)


def _route_chunk(ci, x, g_ref, wt_hi_ref, wt_lo_ref, b_ref, hb_ref, slot_ref, gate_ref, cnt_ref):
    c = CHUNK
    h = _rmsnorm(x, g_ref[...])
    h_hi, h_lo = _split_bf16(h)
    hb_ref[pl.ds(ci * CHUNK, CHUNK), :] = h_hi
    logits = (_dot_nt(wt_hi_ref[...], h_hi) + _dot_nt(wt_lo_ref[...], h_hi)
              + _dot_nt(wt_hi_ref[...], h_lo) + b_ref[...])
    eio = lax.broadcasted_iota(jnp.int32, (N_EXPERTS, c), 0).astype(F32)
    vals, idxs = [], []
    l = logits
    for _ in range(TOP_K):
        m = jnp.max(l, axis=0, keepdims=True)
        idx = jnp.min(jnp.where(l == m, eio, float(N_EXPERTS)), axis=0, keepdims=True)
        vals.append(m)
        idxs.append(idx)
        l = jnp.where(eio == idx, -jnp.inf, l)
    es = [jnp.exp(v - vals[0]) for v in vals]
    denom = es[0] + es[1] + es[2] + es[3]
    sel = jnp.zeros((N_EXPERTS, c), F32)
    for idx in idxs:
        sel = sel + jnp.where(eio == idx, 1.0, 0.0)
    upper = jnp.where(lax.broadcasted_iota(jnp.int32, (c, c), 0) <= lax.broadcasted_iota(jnp.int32, (c, c), 1),
                      1.0, 0.0).astype(BF16)
    rank = _dot(sel.astype(BF16), upper)
    n = jnp.broadcast_to(rank[:, c - 1:c], (N_EXPERTS, c))
    n_units = jnp.floor((n + (SEG_ALIGN - 1)) * (1.0 / SEG_ALIGN))
    strict_lower = jnp.where(lax.broadcasted_iota(jnp.int32, (N_EXPERTS, N_EXPERTS), 1)
                             < lax.broadcasted_iota(jnp.int32, (N_EXPERTS, N_EXPERTS), 0), 1.0, 0.0).astype(BF16)
    off = _dot(strict_lower, n_units.astype(BF16)) * SEG_ALIGN
    dest = off + rank - 1.0
    for k in range(TOP_K):
        slot = jnp.sum(jnp.where(eio == idxs[k], dest, 0.0), axis=0, keepdims=True)
        slot_ref[ci, k:k + 1, :] = slot.astype(jnp.int32)
        gate_ref[ci, k:k + 1, :] = es[k] / denom
    slot_ref[ci, TOP_K:, :] = jnp.zeros((SUBLANES - TOP_K, c), jnp.int32)
    gate_ref[ci, TOP_K:, :] = jnp.zeros((SUBLANES - TOP_K, c), F32)
    cnt_ref[ci] = (n_units[:, :LANES] * SEG_ALIGN).astype(jnp.int32)


def _segment_tables(cnt, n_tiles_max):
    seg_n = cnt[:, :, 0]
    seg_loc = jnp.cumsum(seg_n, axis=1) - seg_n
    tot = jnp.sum(seg_n, axis=0)
    tiles = (tot + GEMM_TILE - 1) // GEMM_TILE
    tile_end = jnp.cumsum(tiles)
    tile_start = tile_end - tiles
    front_pad = tiles * GEMM_TILE - tot
    seg_dst = (tile_start * GEMM_TILE + front_pad)[None, :] + jnp.cumsum(seg_n, axis=0) - seg_n
    used = tile_end[-1]
    tile_ids = jnp.arange(n_tiles_max, dtype=jnp.int32)
    tile_expert = jnp.sum(tile_ids[:, None] >= tile_end[None, :], axis=1).astype(jnp.int32)
    last_expert = jnp.sum(jnp.maximum(used - 1, 0) >= tile_end).astype(jnp.int32)
    tile_expert = jnp.where(tile_ids < used, tile_expert, last_expert)
    owner = tile_expert[:, None] == jnp.arange(N_EXPERTS, dtype=jnp.int32)[None, :]
    pad_here = jnp.sum(jnp.where(owner, jnp.where(tile_ids[:, None] == tile_start[None, :], front_pad[None, :], 0), 0),
                       axis=1)
    tile_valid = jnp.where(tile_ids < used, GEMM_TILE - pad_here, 0)
    experts = jnp.arange(N_EXPERTS, dtype=jnp.int32)
    ordinal = jnp.cumsum((tiles > 0).astype(jnp.int32)) - 1
    later = jnp.where(jnp.logical_and(experts[None, :] > experts[:, None], (tiles > 0)[None, :]), experts[None, :],
                      N_EXPERTS)
    next_expert = jnp.min(later, axis=1)
    next_expert = jnp.where(next_expert < N_EXPERTS, next_expert, -1)
    tile_buf = jnp.sum(jnp.where(owner, (ordinal % 2)[None, :], 0), axis=1)
    tile_next = jnp.sum(jnp.where(owner, next_expert[None, :], 0), axis=1)
    i32 = lambda v: v.astype(jnp.int32).reshape(-1)
    gap = jnp.concatenate([tile_start * GEMM_TILE, front_pad, used[None]])
    seg_tot = jnp.sum(seg_n, axis=1)
    tiles_info = (tile_expert, i32(tile_valid), i32(used), i32(tile_buf), i32(tile_next))
    return i32(seg_loc), i32(seg_n), i32(seg_dst), i32(seg_tot), tiles_info, i32(gap)


def _segment_copy(loc_ref, n_ref, dst_ref, c, e, local_buf, sorted_hbm, sem, to_sorted):
    j = c * N_EXPERTS + e
    n = pl.multiple_of(n_ref[j], SEG_ALIGN)
    local = local_buf.at[pl.ds(pl.multiple_of(loc_ref[j], SEG_ALIGN), n)]
    remote = sorted_hbm.at[pl.ds(pl.multiple_of(dst_ref[j], SEG_ALIGN), n)]
    return n, (pltpu.make_async_copy(local, remote, sem) if to_sorted
               else pltpu.make_async_copy(remote, local, sem))


def _for_each_segment(loc_ref, n_ref, dst_ref, c, local_buf, sorted_hbm, sem, to_sorted, action):
    def body(e, carry):
        n, cp = _segment_copy(loc_ref, n_ref, dst_ref, c, e, local_buf, sorted_hbm, sem, to_sorted)

        @pl.when(n > 0)
        def _():
            action(cp)
        return carry
    lax.fori_loop(0, N_EXPERTS, body, 0)


def _wait_chunk(tot_ref, c, local_buf, sorted_hbm, sem, to_sorted):
    n = pl.multiple_of(tot_ref[c], SEG_ALIGN)
    local = local_buf.at[pl.ds(0, n)]
    remote = sorted_hbm.at[pl.ds(0, n)]
    (pltpu.make_async_copy(local, remote, sem) if to_sorted else pltpu.make_async_copy(remote, local, sem)).wait()


def _zero_unused_rows(gap_ref, xs_hbm, zeros, sem, action):
    def expert_gap(e, carry):
        n = pl.multiple_of(gap_ref[N_EXPERTS + e], SEG_ALIGN)

        @pl.when(n > 0)
        def _():
            action(pltpu.make_async_copy(
                zeros.at[pl.ds(0, n)], xs_hbm.at[pl.ds(pl.multiple_of(gap_ref[e], SEG_ALIGN), n)], sem))
        return carry
    lax.fori_loop(0, N_EXPERTS, expert_gap, 0)

    def unused_tile(i, carry):
        action(pltpu.make_async_copy(zeros, xs_hbm.at[pl.ds(pl.multiple_of(i * GEMM_TILE, GEMM_TILE), GEMM_TILE)], sem))
        return carry
    lax.fori_loop(gap_ref[2 * N_EXPERTS], xs_hbm.shape[0] // GEMM_TILE, unused_tile, 0)


def _dispatch_kernel(loc_ref, n_ref, dst_ref, tot_ref, gap_ref, hb_ref, slot_ref, xs_hbm, buf, zeros, sems):
    c = pl.program_id(0)
    nc = pl.num_programs(0)
    par = c % 2
    start = lambda cp: cp.start()
    wait = lambda cp: cp.wait()

    @pl.when(c == 0)
    def _():
        zeros[...] = jnp.zeros(zeros.shape, BF16)
        _zero_unused_rows(gap_ref, xs_hbm, zeros, sems.at[2], start)

    @pl.when(c >= 2)
    def _():
        _wait_chunk(tot_ref, c - 2, buf.at[par], xs_hbm, sems.at[par], True)

    rows = lax.broadcasted_iota(jnp.int32, (CHUNK_ROWS, CHUNK), 0)
    hit = rows == slot_ref[0, 0:1, :]
    for k in range(1, TOP_K):
        hit = jnp.logical_or(hit, rows == slot_ref[0, k:k + 1, :])
    onehot = jnp.where(hit, 1.0, 0.0).astype(BF16)
    buf[par] = _dot(onehot, hb_ref[...]).astype(BF16)
    _for_each_segment(loc_ref, n_ref, dst_ref, c, buf.at[par], xs_hbm, sems.at[par], True, start)

    @pl.when(c == nc - 1)
    def _():
        _zero_unused_rows(gap_ref, xs_hbm, zeros, sems.at[2], wait)

        @pl.when(c >= 1)
        def _():
            _wait_chunk(tot_ref, c - 1, buf.at[1 - par], xs_hbm, sems.at[1 - par], True)
        _wait_chunk(tot_ref, c, buf.at[par], xs_hbm, sems.at[par], True)


def _dispatch(hb, slots, seg_loc, seg_n, seg_dst, seg_tot, gap, n_rows):
    t, d = hb.shape
    nc = t // CHUNK
    return pl.pallas_call(
        _dispatch_kernel,
        grid_spec=pltpu.PrefetchScalarGridSpec(
            num_scalar_prefetch=5,
            grid=(nc,),
            in_specs=[
                pl.BlockSpec((CHUNK, d), lambda c, *_: (c, 0)),
                pl.BlockSpec((1, SUBLANES, CHUNK), lambda c, *_: (c, 0, 0)),
            ],
            out_specs=pl.BlockSpec(memory_space=pl.ANY),
            scratch_shapes=[pltpu.VMEM((2, CHUNK_ROWS, d), BF16), pltpu.VMEM((GEMM_TILE, d), BF16),
                            pltpu.SemaphoreType.DMA((3,))],
        ),
        out_shape=jax.ShapeDtypeStruct((n_rows, d), BF16),
        compiler_params=pltpu.CompilerParams(dimension_semantics=("arbitrary",), vmem_limit_bytes=VMEM_LIMIT),
        name="moe_dispatch",
    )(seg_loc, seg_n, seg_dst, seg_tot, gap, hb, slots)


def _expert_kernel(layer, te_ref, valid_ref, used_ref, buf_ref, next_ref, x_ref, wgu_hbm, bgu_ref, wd_hbm, bd_ref,
                   o_ref, wgu_f32, wd_f32, wgu_bf, wd_bf, sems):
    i = pl.program_id(0)
    prev = te_ref[jnp.maximum(i - 1, 0)]
    buf = buf_ref[i]

    def weight_copies(expert, b):
        e = layer * N_EXPERTS + expert
        return (pltpu.make_async_copy(wgu_hbm.at[e], wgu_f32.at[b], sems.at[b]),
                pltpu.make_async_copy(wd_hbm.at[e], wd_f32.at[b], sems.at[b]))

    @pl.when(i == 0)
    def _():
        for cp in weight_copies(te_ref[0], buf):
            cp.start()

    @pl.when(jnp.logical_or(i == 0, te_ref[i] != prev))
    def _():
        for cp in weight_copies(te_ref[i], buf):
            cp.wait()

        @pl.when(next_ref[i] >= 0)
        def _():
            for cp in weight_copies(next_ref[i], 1 - buf):
                cp.start()
        wgu_bf[...] = wgu_f32[buf].astype(BF16)
        wd_bf[...] = wd_f32[buf].astype(BF16)

    for sub in range(GEMM_TILE // GEMM_SUB):
        rows = pl.ds(sub * GEMM_SUB, GEMM_SUB)
        rows_after = GEMM_TILE - (sub + 1) * GEMM_SUB

        @pl.when(valid_ref[i] > rows_after)
        def _():
            gu = _dot(x_ref[rows, :], wgu_bf[...]) + bgu_ref[0]
            gate = jnp.minimum(gu[:, :D_FF], SWIGLU_LIMIT)
            up = jnp.clip(gu[:, D_FF:], -SWIGLU_LIMIT, SWIGLU_LIMIT)
            act = (up + 1.0) * (gate * (1.0 / (1.0 + jnp.exp(-SWIGLU_ALPHA * gate))))
            out = _dot(act.astype(BF16), wd_bf[...]) + bd_ref[0]
            o_ref[rows, :] = out.astype(BF16)

        @pl.when(valid_ref[i] <= rows_after)
        def _():
            o_ref[rows, :] = jnp.zeros((GEMM_SUB, D_MODEL), BF16)


def _experts(xs, tiles_info, layer, w_gate_up, b_gate_up, w_down, b_down):
    n_rows, d = xs.shape
    n_tiles = n_rows // GEMM_TILE
    n_stacked = w_gate_up.shape[0] * N_EXPERTS
    row_map = lambda i, te, tv, u, *_: (jnp.minimum(i, jnp.maximum(u[0] - 1, 0)), 0)
    exp_map = lambda i, te, *_: (layer * N_EXPERTS + te[i], 0, 0)
    return pl.pallas_call(
        functools.partial(_expert_kernel, layer),
        grid_spec=pltpu.PrefetchScalarGridSpec(
            num_scalar_prefetch=5,
            grid=(n_tiles,),
            in_specs=[
                pl.BlockSpec((GEMM_TILE, d), row_map),
                pl.BlockSpec(memory_space=pl.ANY),
                pl.BlockSpec((1, 1, 2 * D_FF), exp_map),
                pl.BlockSpec(memory_space=pl.ANY),
                pl.BlockSpec((1, 1, d), exp_map),
            ],
            out_specs=pl.BlockSpec((GEMM_TILE, d), lambda i, *_: (i, 0)),
            scratch_shapes=[pltpu.VMEM((2, d, 2 * D_FF), F32), pltpu.VMEM((2, D_FF, d), F32),
                            pltpu.VMEM((d, 2 * D_FF), BF16), pltpu.VMEM((D_FF, d), BF16),
                            pltpu.SemaphoreType.DMA((2,))],
        ),
        out_shape=jax.ShapeDtypeStruct((n_rows, d), BF16),
        compiler_params=pltpu.CompilerParams(dimension_semantics=("arbitrary",), vmem_limit_bytes=VMEM_LIMIT),
        name="moe_experts",
    )(*tiles_info, xs, w_gate_up.reshape(n_stacked, d, 2 * D_FF),
      b_gate_up.reshape(n_stacked, 1, 2 * D_FF), w_down.reshape(n_stacked, D_FF, d), b_down.reshape(n_stacked, 1, d))


def _combine_kernel(loc_ref, n_ref, dst_ref, tot_ref, x_ref, slot_t_ref, gate_t_ref, ys_hbm, o_ref, buf, sems):
    c = pl.program_id(0)
    nc = pl.num_programs(0)
    par = c % 2
    start = lambda cp: cp.start()
    wait = lambda cp: cp.wait()

    @pl.when(c == 0)
    def _():
        buf[...] = jnp.zeros(buf.shape, BF16)
        _for_each_segment(loc_ref, n_ref, dst_ref, c, buf.at[0], ys_hbm, sems.at[0], False, start)

    @pl.when(c + 1 < nc)
    def _():
        _for_each_segment(loc_ref, n_ref, dst_ref, c + 1, buf.at[1 - par], ys_hbm, sems.at[1 - par], False, start)

    cols = lax.broadcasted_iota(jnp.int32, (CHUNK, CHUNK_ROWS), 1)
    st = slot_t_ref[0]
    gt = gate_t_ref[0]
    back = jnp.zeros((CHUNK, CHUNK_ROWS), F32)
    for k in range(TOP_K):
        back = jnp.where(cols == st[:, k:k + 1], gt[:, k:k + 1], back)
    back = back.astype(BF16)

    _wait_chunk(tot_ref, c, buf.at[par], ys_hbm, sems.at[par], False)
    ys = buf[par]
    o_ref[...] = x_ref[...] + _dot(back, ys)


def _combine(x2d, ys, slots_t, gates_t, seg_loc, seg_n, seg_dst, seg_tot):
    t, d = x2d.shape
    nc = t // CHUNK
    return pl.pallas_call(
        _combine_kernel,
        grid_spec=pltpu.PrefetchScalarGridSpec(
            num_scalar_prefetch=4,
            grid=(nc,),
            in_specs=[
                pl.BlockSpec((CHUNK, d), lambda c, *_: (c, 0)),
                pl.BlockSpec((1, CHUNK, TOP_K), lambda c, *_: (c, 0, 0)),
                pl.BlockSpec((1, CHUNK, TOP_K), lambda c, *_: (c, 0, 0)),
                pl.BlockSpec(memory_space=pl.ANY),
            ],
            out_specs=pl.BlockSpec((CHUNK, d), lambda c, *_: (c, 0)),
            scratch_shapes=[pltpu.VMEM((2, CHUNK_ROWS, d), BF16), pltpu.SemaphoreType.DMA((2,))],
        ),
        out_shape=jax.ShapeDtypeStruct((t, d), F32),
        compiler_params=pltpu.CompilerParams(dimension_semantics=("arbitrary",), vmem_limit_bytes=VMEM_LIMIT),
        name="moe_combine",
    )(seg_loc, seg_n, seg_dst, seg_tot, x2d, slots_t, gates_t, ys)


def _moe_layer(x2d, routed, layer, w_gate_up, b_gate_up, w_down, b_down):
    t, d = x2d.shape
    nc = t // CHUNK
    n_tiles_max = (t * TOP_K + nc * N_EXPERTS * (SEG_ALIGN - 1)) // GEMM_TILE + N_EXPERTS
    hb, slots, gates, cnt = routed
    seg_loc, seg_n, seg_dst, seg_tot, tiles_info, gap = _segment_tables(cnt, n_tiles_max)
    xs = _dispatch(hb, slots, seg_loc, seg_n, seg_dst, seg_tot, gap, n_tiles_max * GEMM_TILE)
    ys = _experts(xs, tiles_info, layer, w_gate_up, b_gate_up, w_down, b_down)
    slots_t = jnp.swapaxes(slots[:, :TOP_K, :], 1, 2)
    gates_t = jnp.swapaxes(gates[:, :TOP_K, :], 1, 2)
    return _combine(x2d, ys, slots_t, gates_t, seg_loc, seg_n, seg_dst, seg_tot)


def _qkv_kernel(x_ref, g_ref, w_ref, qg_ref, kg_ref, q_ref, k_ref, v_ref):
    d = x_ref.shape[1]
    h = _rmsnorm(x_ref[...], g_ref[...]).astype(BF16)
    qkv = _dot(h, w_ref[...])
    width = 2 * LANES
    li = lax.broadcasted_iota(jnp.int32, (width, width), 0) // HEAD_DIM
    lj = lax.broadcasted_iota(jnp.int32, (width, width), 1) // HEAD_DIM
    same_head = jnp.where(li == lj, 1.0, 0.0).astype(BF16)

    def head_norm(v, gain, out_scale):
        parts = []
        for j in range(d // width):
            vj = v[:, j * width:(j + 1) * width]
            ss = _dot((vj * vj).astype(BF16), same_head)
            parts.append(vj * lax.rsqrt(ss * (1.0 / HEAD_DIM) + RMS_EPS))
        return (jnp.concatenate(parts, axis=-1) * gain * out_scale).astype(BF16)

    q_ref[...] = head_norm(qkv[:, :d], qg_ref[...], HEAD_DIM ** -0.5 * LOG2_E)
    k_ref[...] = head_norm(qkv[:, d:2 * d], kg_ref[...], 1.0)
    v_ref[...] = qkv[:, 2 * d:].astype(BF16)


def _qkv(x2d, g, w_qkv, q_gain, k_gain):
    t, d = x2d.shape
    tt = min(TOK_TILE, t)
    row = lambda i: (i, 0)
    fixed = lambda i: (0, 0)
    return pl.pallas_call(
        _qkv_kernel,
        grid=(t // tt,),
        in_specs=[
            pl.BlockSpec((tt, d), row),
            pl.BlockSpec((1, d), fixed),
            pl.BlockSpec((d, 3 * d), fixed),
            pl.BlockSpec((1, d), fixed),
            pl.BlockSpec((1, d), fixed),
        ],
        out_specs=[pl.BlockSpec((tt, d), row)] * 3,
        out_shape=[jax.ShapeDtypeStruct((t, d), BF16)] * 3,
        compiler_params=pltpu.CompilerParams(dimension_semantics=("arbitrary",), vmem_limit_bytes=VMEM_LIMIT),
        name="attn_qkv",
    )(x2d, g.reshape(1, d), w_qkv.astype(BF16), jnp.tile(q_gain, N_HEADS).reshape(1, d),
      jnp.tile(k_gain, N_HEADS).reshape(1, d))


def _attn_kernel(q_ref, k_ref, v_ref, o_ref, acc_ref, carry_ref):
    qi = pl.program_id(2)
    tq, kb = ATT_Q, ATT_K
    q = q_ref[0]
    head_of_lane = lax.broadcasted_iota(jnp.int32, (tq, LANES), 1) // HEAD_DIM
    q_heads = [jnp.where(head_of_lane == hd, q, jnp.zeros_like(q)) for hd in range(2)]
    r2 = lax.broadcasted_iota(jnp.int32, (2 * kb, 2 * kb), 0)
    c2 = lax.broadcasted_iota(jnp.int32, (2 * kb, 2 * kb), 1)
    suffix2 = jnp.where(jnp.logical_and(r2 // kb == c2 // kb, r2 >= c2), 1.0, 0.0).astype(BF16)

    acc_ref[...] = jnp.zeros(acc_ref.shape, F32)
    carry_ref[...] = jnp.zeros(carry_ref.shape, F32)

    def key_group(j0, diag_offsets, r0):
        nb = len(diag_offsets)
        rows = tq - r0
        start = pl.multiple_of(j0 * kb, kb)
        kk = k_ref[0, pl.ds(start, nb * kb), :]
        vv = v_ref[0, pl.ds(start, nb * kb), :]
        row = lax.broadcasted_iota(jnp.int32, (rows, kb), 0) + r0
        key = lax.broadcasted_iota(jnp.int32, (rows, kb), 1)
        z_all = [_dot_nt(q_heads[hd][r0:, :], kk) for hd in range(2)]
        carry = [carry_ref[hd, r0:, :] for hd in range(2)]
        probs = [[None] * nb for _ in range(2)]
        for b in reversed(range(nb)):
            causal = None if diag_offsets[b] is None else key + diag_offsets[b] < row
            zs, log_keeps = [], []
            for hd in range(2):
                z = z_all[hd][:, b * kb:(b + 1) * kb]
                neg = -z
                log_keep = jnp.minimum(neg, 0.0) - jnp.log(1.0 + jnp.exp2(jnp.minimum(z, neg))) * LOG2_E
                if causal is not None:
                    log_keep = jnp.where(causal, log_keep, 0.0)
                zs.append(z)
                log_keeps.append(log_keep.astype(BF16))
            sums = _dot(jnp.concatenate(log_keeps, axis=1), suffix2)
            for hd in range(2):
                s_in = sums[:, hd * kb:(hd + 1) * kb]
                a = jnp.exp2(zs[hd] + s_in + carry[hd])
                if causal is not None:
                    a = jnp.where(causal, a, 0.0)
                probs[hd][b] = a.astype(BF16)
                carry[hd] = carry[hd] + jnp.broadcast_to(s_in[:, 0:1], (rows, kb))
        for hd in range(2):
            carry_ref[hd, r0:, :] = carry[hd]
            acc_ref[hd, r0:, :] += _dot(jnp.concatenate(probs[hd], axis=1), vv)

    blocks_per_q = tq // kb
    for g in reversed(range(blocks_per_q // ATT_DIAG_GROUP)):
        offs = [(g * ATT_DIAG_GROUP + b) * kb for b in range(ATT_DIAG_GROUP)]
        key_group(qi * blocks_per_q + g * ATT_DIAG_GROUP, offs, offs[0])

    def body(s, c):
        key_group(qi * blocks_per_q - (s + 1) * ATT_GROUP, [None] * ATT_GROUP, 0)
        return c
    lax.fori_loop(0, qi * (blocks_per_q // ATT_GROUP), body, 0)
    o_ref[0] = jnp.where(head_of_lane == 0, acc_ref[0], acc_ref[1]).astype(BF16)


def _attention(q, k, v, b, s):
    d = q.shape[-1]
    q3, k3, v3 = (a.reshape(b, s, d) for a in (q, k, v))
    pairs = d // LANES
    o = pl.pallas_call(
        _attn_kernel,
        grid=(b, pairs, s // ATT_Q),
        in_specs=[
            pl.BlockSpec((1, ATT_Q, LANES), lambda bi, hp, i: (bi, i, hp)),
            pl.BlockSpec((1, s, LANES), lambda bi, hp, i: (bi, 0, hp)),
            pl.BlockSpec((1, s, LANES), lambda bi, hp, i: (bi, 0, hp)),
        ],
        out_specs=pl.BlockSpec((1, ATT_Q, LANES), lambda bi, hp, i: (bi, i, hp)),
        out_shape=jax.ShapeDtypeStruct((b, s, d), BF16),
        scratch_shapes=[pltpu.VMEM((2, ATT_Q, LANES), F32), pltpu.VMEM((2, ATT_Q, LANES), F32)],
        compiler_params=pltpu.CompilerParams(
            dimension_semantics=("arbitrary", "arbitrary", "arbitrary"), vmem_limit_bytes=VMEM_LIMIT),
        name="attn_core",
    )(q3, k3, v3)
    return o.reshape(b * s, d)


def _proj_route_kernel(x_ref, o_ref, w_ref, rg_ref, wt_hi_ref, wt_lo_ref, rb_ref,
                       y_ref, hb_ref, slot_ref, gate_ref, cnt_ref):
    x2 = x_ref[...] + _dot(o_ref[...], w_ref[...])
    y_ref[...] = x2
    _route_rows(x2, rg_ref, wt_hi_ref, wt_lo_ref, rb_ref, hb_ref, slot_ref, gate_ref, cnt_ref)


def _proj_route(x2d, o, w_o, ffn_g, router_w, router_b):
    t, d = x2d.shape
    route_in, route_out = _route_specs(d, lambda c: c)
    outs = pl.pallas_call(
        _proj_route_kernel,
        grid=(t // TOK_TILE,),
        in_specs=[
            pl.BlockSpec((TOK_TILE, d), lambda c: (c, 0)),
            pl.BlockSpec((TOK_TILE, d), lambda c: (c, 0)),
            pl.BlockSpec((d, d), lambda c: (0, 0)),
        ] + route_in,
        out_specs=[pl.BlockSpec((TOK_TILE, d), lambda c: (c, 0))] + route_out,
        out_shape=[jax.ShapeDtypeStruct((t, d), F32)] + _route_shapes(t, d),
        compiler_params=pltpu.CompilerParams(dimension_semantics=("arbitrary",), vmem_limit_bytes=VMEM_LIMIT),
        name="attn_proj_route",
    )(x2d, o, w_o.astype(BF16), *_route_args(ffn_g, router_w, router_b))
    return outs[0], outs[1:]


def _attention_route_layer(x2d, b, s, g, w_qkv, q_gain, k_gain, w_o, ffn_g, router_w, router_b):
    q, k, v = _qkv(x2d, g, w_qkv, q_gain, k_gain)
    o = _attention(q, k, v, b, s)
    return _proj_route(x2d, o, w_o, ffn_g, router_w, router_b)


def kernel(x, mix_norm, pool_w, pool_scale, w_qkv, q_norm, k_norm, w_o, ffn_norm, router_w, router_b,
           w_gate_up, b_gate_up, w_down, b_down):
    b, s, d = x.shape
    assert d == D_MODEL and s % ATT_Q == 0 and s % TOK_TILE == 0 and TOK_TILE % CHUNK == 0
    depth = mix_norm.shape[0]
    for i in range(depth):
        j = i // 2
        if i % 2 == 0:
            x2d, routed = _pool_route_layer(x, mix_norm[i], pool_w[j], pool_scale[j],
                                            ffn_norm[i], router_w[i], router_b[i])
        else:
            x2d, routed = _attention_route_layer(x.reshape(b * s, d), b, s, mix_norm[i], w_qkv[j], q_norm[j],
                                                 k_norm[j], w_o[j], ffn_norm[i], router_w[i], router_b[i])
        x2d = _moe_layer(x2d, routed, i, w_gate_up, b_gate_up, w_down, b_down)
        x = x2d.reshape(b, s, d)
    return x
```

```python
import functools

import jax
import jax.numpy as jnp
from jax import lax
from jax.experimental import pallas as pl
from jax.experimental.pallas import tpu as pltpu

D_MODEL = 1024
POOL_WINDOWS = (2, 4, 8, 16)
POOL_GROUP_DIM = D_MODEL // len(POOL_WINDOWS)
POOL_HALO = 16
HEAD_DIM = 64
N_HEADS = D_MODEL // HEAD_DIM
N_EXPERTS = 32
TOP_K = 4
D_FF = D_MODEL
SWIGLU_LIMIT = 7.0
SWIGLU_ALPHA = 1.702
RMS_EPS = 1e-6
LOG2_E = 1.4426950408889634

LANES = 128
SUBLANES = 8
VMEM_LIMIT = 56 * 1024 * 1024

CHUNK = 256
SEG_ALIGN = 2 * SUBLANES
CHUNK_ROWS = CHUNK * TOP_K + N_EXPERTS * SEG_ALIGN
GEMM_TILE = 512
GEMM_SUB = 256
TOK_TILE = 512
ATT_Q = 512
ATT_K = LANES
ATT_GROUP = 4
ATT_DIAG_GROUP = 2

F32 = jnp.float32
BF16 = jnp.bfloat16


def _dot(a, b):
    return jnp.dot(a, b, preferred_element_type=F32)


def _dot_nt(a, b):
    return lax.dot_general(a, b, (((1,), (1,)), ((), ())), preferred_element_type=F32)


def _split_bf16(v):
    hi = v.astype(BF16)
    lo = (v - hi.astype(F32)).astype(BF16)
    return hi, lo


def _rmsnorm(v, g):
    return v * lax.rsqrt(jnp.mean(v * v, axis=-1, keepdims=True) + RMS_EPS) * g


def _pool_route_kernel(x_ref, halo_ref, g_ref, w_ref, scale_ref, rg_ref, wt_hi_ref, wt_lo_ref, rb_ref,
                       o_ref, hb_ref, slot_ref, gate_ref, cnt_ref):
    x1 = _pool_tile(x_ref[0], halo_ref[0], pl.program_id(1), g_ref, w_ref, scale_ref)
    o_ref[0] = x1
    _route_rows(x1, rg_ref, wt_hi_ref, wt_lo_ref, rb_ref, hb_ref, slot_ref, gate_ref, cnt_ref)


def _pool_tile(x, halo, i, g_ref, w_ref, scale_ref):
    g = g_ref[...]
    h = _rmsnorm(x, g)
    hh = _rmsnorm(halo, g)
    hh = jnp.where(i > 0, hh, 0.0)
    a = jnp.concatenate([hh, h], axis=0)
    ts = x.shape[0]
    pos = i * ts + lax.broadcasted_iota(jnp.int32, (ts, 1), 0)
    outs = []
    for gi, w in enumerate(POOL_WINDOWS):
        lo, hi = gi * POOL_GROUP_DIM, (gi + 1) * POOL_GROUP_DIM
        s = a[:, lo:hi]
        span = 1
        while span < w:
            s = s + pltpu.roll(s, span, axis=0)
            span *= 2
        cnt = jnp.minimum(pos + 1, w).astype(F32)
        pooled = s[POOL_HALO:, :] / cnt - h[:, lo:hi]
        outs.append(_dot(pooled.astype(BF16), w_ref[gi]))
    return x + jnp.concatenate(outs, axis=-1) * scale_ref[...]


def _route_specs(d, step_of):
    fixed2 = lambda *_: (0, 0)
    per_step = TOK_TILE // CHUNK
    in_specs = [
        pl.BlockSpec((1, d), fixed2),
        pl.BlockSpec((N_EXPERTS, d), fixed2),
        pl.BlockSpec((N_EXPERTS, d), fixed2),
        pl.BlockSpec((N_EXPERTS, 1), fixed2),
    ]
    out_specs = [
        pl.BlockSpec((TOK_TILE, d), lambda *ids: (step_of(*ids), 0)),
        pl.BlockSpec((per_step, SUBLANES, CHUNK), lambda *ids: (step_of(*ids), 0, 0)),
        pl.BlockSpec((per_step, SUBLANES, CHUNK), lambda *ids: (step_of(*ids), 0, 0)),
        pl.BlockSpec((per_step, N_EXPERTS, LANES), lambda *ids: (step_of(*ids), 0, 0)),
    ]
    return in_specs, out_specs


def _route_args(g, router_w, router_b):
    wt_hi, wt_lo = _split_bf16(router_w.T)
    return g.reshape(1, -1), wt_hi, wt_lo, router_b.reshape(N_EXPERTS, 1)


def _route_shapes(t, d):
    nc = t // CHUNK
    return [
        jax.ShapeDtypeStruct((t, d), BF16),
        jax.ShapeDtypeStruct((nc, SUBLANES, CHUNK), jnp.int32),
        jax.ShapeDtypeStruct((nc, SUBLANES, CHUNK), F32),
        jax.ShapeDtypeStruct((nc, N_EXPERTS, LANES), jnp.int32),
    ]


def _pool_route_layer(x, g, w, scale, ffn_g, router_w, router_b):
    b, s, d = x.shape
    ts = TOK_TILE
    tiles_per_seq = s // ts
    blocks_per_tile = ts // POOL_HALO
    route_in, route_out = _route_specs(d, lambda bi, i: bi * tiles_per_seq + i)
    outs = pl.pallas_call(
        _pool_route_kernel,
        grid=(b, tiles_per_seq),
        in_specs=[
            pl.BlockSpec((1, ts, d), lambda bi, i: (bi, i, 0)),
            pl.BlockSpec((1, POOL_HALO, d), lambda bi, i: (bi, jnp.maximum(i * blocks_per_tile - 1, 0), 0)),
            pl.BlockSpec((1, d), lambda bi, i: (0, 0)),
            pl.BlockSpec((len(POOL_WINDOWS), POOL_GROUP_DIM, POOL_GROUP_DIM), lambda bi, i: (0, 0, 0)),
            pl.BlockSpec((1, d), lambda bi, i: (0, 0)),
        ] + route_in,
        out_specs=[pl.BlockSpec((1, ts, d), lambda bi, i: (bi, i, 0))] + route_out,
        out_shape=[jax.ShapeDtypeStruct(x.shape, F32)] + _route_shapes(b * s, d),
        compiler_params=pltpu.CompilerParams(
            dimension_semantics=("arbitrary", "arbitrary"), vmem_limit_bytes=VMEM_LIMIT),
        name="pool_route",
    )(x, x, g.reshape(1, d), w.astype(BF16), scale.reshape(1, d), *_route_args(ffn_g, router_w, router_b))
    return outs[0].reshape(b * s, d), outs[1:]


def _route_rows(x, g_ref, wt_hi_ref, wt_lo_ref, b_ref, hb_ref, slot_ref, gate_ref, cnt_ref):
    t = x.shape[0]
    h = _rmsnorm(x, g_ref[...])
    h_hi, h_lo = _split_bf16(h)
    hb_ref[...] = h_hi
    logits = (_dot_nt(wt_hi_ref[...], h_hi) + _dot_nt(wt_lo_ref[...], h_hi)
              + _dot_nt(wt_hi_ref[...], h_lo) + b_ref[...])
    eio = lax.broadcasted_iota(jnp.int32, (N_EXPERTS, t), 0).astype(F32)
    vals, idxs = [], []
    l = logits
    for _ in range(TOP_K):
        m = jnp.max(l, axis=0, keepdims=True)
        idx = jnp.min(jnp.where(l == m, eio, float(N_EXPERTS)), axis=0, keepdims=True)
        vals.append(m)
        idxs.append(idx)
        l = jnp.where(eio == idx, -jnp.inf, l)
    es = [jnp.exp(v - vals[0]) for v in vals]
    denom = es[0] + es[1] + es[2] + es[3]
    sel = jnp.zeros((N_EXPERTS, t), F32)
    for idx in idxs:
        sel = sel + jnp.where(eio == idx, 1.0, 0.0)
    ti = lax.broadcasted_iota(jnp.int32, (t, t), 0)
    tj = lax.broadcasted_iota(jnp.int32, (t, t), 1)
    upper = jnp.where(jnp.logical_and(ti <= tj, ti // CHUNK == tj // CHUNK), 1.0, 0.0).astype(BF16)
    rank = _dot(sel.astype(BF16), upper)
    chunk_of_lane = lax.broadcasted_iota(jnp.int32, (N_EXPERTS, t), 1) // CHUNK
    n = jnp.zeros((N_EXPERTS, t), F32)
    for ci in range(t // CHUNK):
        last = (ci + 1) * CHUNK - 1
        n = jnp.where(chunk_of_lane == ci, rank[:, last:last + 1], n)
    n_units = jnp.floor((n + (SEG_ALIGN - 1)) * (1.0 / SEG_ALIGN))
    strict_lower = jnp.where(lax.broadcasted_iota(jnp.int32, (N_EXPERTS, N_EXPERTS), 1)
                             < lax.broadcasted_iota(jnp.int32, (N_EXPERTS, N_EXPERTS), 0), 1.0, 0.0).astype(BF16)
    off = _dot(strict_lower, n_units.astype(BF16)) * SEG_ALIGN
    dest = off + rank - 1.0
    slots = [jnp.sum(jnp.where(eio == idxs[k], dest, 0.0), axis=0, keepdims=True).astype(jnp.int32)
             for k in range(TOP_K)]
    for ci in range(t // CHUNK):
        lanes = slice(ci * CHUNK, (ci + 1) * CHUNK)
        for k in range(TOP_K):
            slot_ref[ci, k:k + 1, :] = slots[k][:, lanes]
            gate_ref[ci, k:k + 1, :] = (es[k] / denom)[:, lanes]
        slot_ref[ci, TOP_K:, :] = jnp.zeros((SUBLANES - TOP_K, CHUNK), jnp.int32)
        gate_ref[ci, TOP_K:, :] = jnp.zeros((SUBLANES - TOP_K, CHUNK), F32)
        cnt_ref[ci] = (n_units[:, ci * CHUNK:ci * CHUNK + LANES] * SEG_ALIGN).astype(jnp.int32)


def _segment_tables(cnt, n_tiles_max):
    seg_n = cnt[:, :, 0]
    seg_loc = jnp.cumsum(seg_n, axis=1) - seg_n
    tot = jnp.sum(seg_n, axis=0)
    tiles = (tot + GEMM_TILE - 1) // GEMM_TILE
    tile_end = jnp.cumsum(tiles)
    tile_start = tile_end - tiles
    front_pad = tiles * GEMM_TILE - tot
    seg_dst = (tile_start * GEMM_TILE + front_pad)[None, :] + jnp.cumsum(seg_n, axis=0) - seg_n
    used = tile_end[-1]
    tile_ids = jnp.arange(n_tiles_max, dtype=jnp.int32)
    tile_expert = jnp.sum(tile_ids[:, None] >= tile_end[None, :], axis=1).astype(jnp.int32)
    last_expert = jnp.sum(jnp.maximum(used - 1, 0) >= tile_end).astype(jnp.int32)
    tile_expert = jnp.where(tile_ids < used, tile_expert, last_expert)
    owner = tile_expert[:, None] == jnp.arange(N_EXPERTS, dtype=jnp.int32)[None, :]
    pad_here = jnp.sum(jnp.where(owner, jnp.where(tile_ids[:, None] == tile_start[None, :], front_pad[None, :], 0), 0),
                       axis=1)
    tile_valid = jnp.where(tile_ids < used, GEMM_TILE - pad_here, 0)
    experts = jnp.arange(N_EXPERTS, dtype=jnp.int32)
    ordinal = jnp.cumsum((tiles > 0).astype(jnp.int32)) - 1
    later = jnp.where(jnp.logical_and(experts[None, :] > experts[:, None], (tiles > 0)[None, :]), experts[None, :],
                      N_EXPERTS)
    next_expert = jnp.min(later, axis=1)
    next_expert = jnp.where(next_expert < N_EXPERTS, next_expert, -1)
    tile_buf = jnp.sum(jnp.where(owner, (ordinal % 2)[None, :], 0), axis=1)
    tile_next = jnp.sum(jnp.where(owner, next_expert[None, :], 0), axis=1)
    i32 = lambda v: v.astype(jnp.int32).reshape(-1)
    gap = jnp.concatenate([tile_start * GEMM_TILE, front_pad, used[None]])
    seg_tot = jnp.sum(seg_n, axis=1)
    tiles_info = (tile_expert, i32(tile_valid), i32(used), i32(tile_buf), i32(tile_next))
    return i32(seg_loc), i32(seg_n), i32(seg_dst), i32(seg_tot), tiles_info, i32(gap)


def _segment_copy(loc_ref, n_ref, dst_ref, c, e, local_buf, sorted_hbm, sem, to_sorted):
    j = c * N_EXPERTS + e
    n = pl.multiple_of(n_ref[j], SEG_ALIGN)
    local = local_buf.at[pl.ds(pl.multiple_of(loc_ref[j], SEG_ALIGN), n)]
    remote = sorted_hbm.at[pl.ds(pl.multiple_of(dst_ref[j], SEG_ALIGN), n)]
    return n, (pltpu.make_async_copy(local, remote, sem) if to_sorted
               else pltpu.make_async_copy(remote, local, sem))


def _for_each_segment(loc_ref, n_ref, dst_ref, c, local_buf, sorted_hbm, sem, to_sorted, action):
    def body(e, carry):
        n, cp = _segment_copy(loc_ref, n_ref, dst_ref, c, e, local_buf, sorted_hbm, sem, to_sorted)

        @pl.when(n > 0)
        def _():
            action(cp)
        return carry
    lax.fori_loop(0, N_EXPERTS, body, 0)


def _wait_chunk(tot_ref, c, local_buf, sorted_hbm, sem, to_sorted):
    n = pl.multiple_of(tot_ref[c], SEG_ALIGN)
    local = local_buf.at[pl.ds(0, n)]
    remote = sorted_hbm.at[pl.ds(0, n)]
    (pltpu.make_async_copy(local, remote, sem) if to_sorted else pltpu.make_async_copy(remote, local, sem)).wait()


def _zero_unused_rows(gap_ref, xs_hbm, zeros, sem, action):
    def expert_gap(e, carry):
        n = pl.multiple_of(gap_ref[N_EXPERTS + e], SEG_ALIGN)

        @pl.when(n > 0)
        def _():
            action(pltpu.make_async_copy(
                zeros.at[pl.ds(0, n)], xs_hbm.at[pl.ds(pl.multiple_of(gap_ref[e], SEG_ALIGN), n)], sem))
        return carry
    lax.fori_loop(0, N_EXPERTS, expert_gap, 0)

    def unused_tile(i, carry):
        action(pltpu.make_async_copy(zeros, xs_hbm.at[pl.ds(pl.multiple_of(i * GEMM_TILE, GEMM_TILE), GEMM_TILE)], sem))
        return carry
    lax.fori_loop(gap_ref[2 * N_EXPERTS], xs_hbm.shape[0] // GEMM_TILE, unused_tile, 0)


def _dispatch_kernel(loc_ref, n_ref, dst_ref, tot_ref, gap_ref, hb_ref, slot_ref, xs_hbm, buf, zeros, sems):
    c = pl.program_id(0)
    nc = pl.num_programs(0)
    par = c % 2
    start = lambda cp: cp.start()
    wait = lambda cp: cp.wait()

    @pl.when(c == 0)
    def _():
        zeros[...] = jnp.zeros(zeros.shape, BF16)
        _zero_unused_rows(gap_ref, xs_hbm, zeros, sems.at[2], start)

    @pl.when(c >= 2)
    def _():
        _wait_chunk(tot_ref, c - 2, buf.at[par], xs_hbm, sems.at[par], True)

    rows = lax.broadcasted_iota(jnp.int32, (CHUNK_ROWS, CHUNK), 0)
    hit = rows == slot_ref[0, 0:1, :]
    for k in range(1, TOP_K):
        hit = jnp.logical_or(hit, rows == slot_ref[0, k:k + 1, :])
    onehot = jnp.where(hit, 1.0, 0.0).astype(BF16)
    buf[par] = _dot(onehot, hb_ref[...]).astype(BF16)
    _for_each_segment(loc_ref, n_ref, dst_ref, c, buf.at[par], xs_hbm, sems.at[par], True, start)

    @pl.when(c == nc - 1)
    def _():
        _zero_unused_rows(gap_ref, xs_hbm, zeros, sems.at[2], wait)

        @pl.when(c >= 1)
        def _():
            _wait_chunk(tot_ref, c - 1, buf.at[1 - par], xs_hbm, sems.at[1 - par], True)
        _wait_chunk(tot_ref, c, buf.at[par], xs_hbm, sems.at[par], True)


def _dispatch(hb, slots, seg_loc, seg_n, seg_dst, seg_tot, gap, n_rows):
    t, d = hb.shape
    nc = t // CHUNK
    return pl.pallas_call(
        _dispatch_kernel,
        grid_spec=pltpu.PrefetchScalarGridSpec(
            num_scalar_prefetch=5,
            grid=(nc,),
            in_specs=[
                pl.BlockSpec((CHUNK, d), lambda c, *_: (c, 0)),
                pl.BlockSpec((1, SUBLANES, CHUNK), lambda c, *_: (c, 0, 0)),
            ],
            out_specs=pl.BlockSpec(memory_space=pl.ANY),
            scratch_shapes=[pltpu.VMEM((2, CHUNK_ROWS, d), BF16), pltpu.VMEM((GEMM_TILE, d), BF16),
                            pltpu.SemaphoreType.DMA((3,))],
        ),
        out_shape=jax.ShapeDtypeStruct((n_rows, d), BF16),
        compiler_params=pltpu.CompilerParams(dimension_semantics=("arbitrary",), vmem_limit_bytes=VMEM_LIMIT),
        name="moe_dispatch",
    )(seg_loc, seg_n, seg_dst, seg_tot, gap, hb, slots)


def _expert_kernel(layer, te_ref, valid_ref, used_ref, buf_ref, next_ref, x_ref, wgu_hbm, bgu_ref, wd_hbm, bd_ref,
                   o_ref, wgu_f32, wd_f32, wgu_bf, wd_bf, sems):
    i = pl.program_id(0)
    prev = te_ref[jnp.maximum(i - 1, 0)]
    buf = buf_ref[i]

    def weight_copies(expert, b):
        e = layer * N_EXPERTS + expert
        return (pltpu.make_async_copy(wgu_hbm.at[e], wgu_f32.at[b], sems.at[b]),
                pltpu.make_async_copy(wd_hbm.at[e], wd_f32.at[b], sems.at[b]))

    @pl.when(i == 0)
    def _():
        for cp in weight_copies(te_ref[0], buf):
            cp.start()

    @pl.when(jnp.logical_or(i == 0, te_ref[i] != prev))
    def _():
        for cp in weight_copies(te_ref[i], buf):
            cp.wait()

        @pl.when(next_ref[i] >= 0)
        def _():
            for cp in weight_copies(next_ref[i], 1 - buf):
                cp.start()
        wgu_bf[...] = wgu_f32[buf].astype(BF16)
        wd_bf[...] = wd_f32[buf].astype(BF16)

    def expert_mlp(rows):
        gu = _dot(x_ref[rows, :], wgu_bf[...]) + bgu_ref[0]
        gate = jnp.minimum(gu[:, :D_FF], SWIGLU_LIMIT)
        up = jnp.clip(gu[:, D_FF:], -SWIGLU_LIMIT, SWIGLU_LIMIT)
        act = (up + 1.0) * (gate * (1.0 / (1.0 + jnp.exp(-SWIGLU_ALPHA * gate))))
        out = _dot(act.astype(BF16), wd_bf[...]) + bd_ref[0]
        o_ref[rows, :] = out.astype(BF16)

    n_sub = GEMM_TILE // GEMM_SUB

    @pl.when(valid_ref[i] > GEMM_TILE - GEMM_SUB)
    def _():
        expert_mlp(pl.ds(0, GEMM_TILE))

    @pl.when(valid_ref[i] <= GEMM_TILE - GEMM_SUB)
    def _():
        o_ref[pl.ds(0, GEMM_SUB), :] = jnp.zeros((GEMM_SUB, D_MODEL), BF16)
        for sub in range(1, n_sub):
            rows = pl.ds(sub * GEMM_SUB, GEMM_SUB)

            @pl.when(valid_ref[i] > GEMM_TILE - (sub + 1) * GEMM_SUB)
            def _():
                expert_mlp(rows)

            @pl.when(valid_ref[i] <= GEMM_TILE - (sub + 1) * GEMM_SUB)
            def _():
                o_ref[rows, :] = jnp.zeros((GEMM_SUB, D_MODEL), BF16)


def _experts(xs, tiles_info, layer, w_gate_up, b_gate_up, w_down, b_down):
    n_rows, d = xs.shape
    n_tiles = n_rows // GEMM_TILE
    n_stacked = w_gate_up.shape[0] * N_EXPERTS
    row_map = lambda i, te, tv, u, *_: (jnp.minimum(i, jnp.maximum(u[0] - 1, 0)), 0)
    exp_map = lambda i, te, *_: (layer * N_EXPERTS + te[i], 0, 0)
    return pl.pallas_call(
        functools.partial(_expert_kernel, layer),
        grid_spec=pltpu.PrefetchScalarGridSpec(
            num_scalar_prefetch=5,
            grid=(n_tiles,),
            in_specs=[
                pl.BlockSpec((GEMM_TILE, d), row_map),
                pl.BlockSpec(memory_space=pl.ANY),
                pl.BlockSpec((1, 1, 2 * D_FF), exp_map),
                pl.BlockSpec(memory_space=pl.ANY),
                pl.BlockSpec((1, 1, d), exp_map),
            ],
            out_specs=pl.BlockSpec((GEMM_TILE, d), lambda i, *_: (i, 0)),
            scratch_shapes=[pltpu.VMEM((2, d, 2 * D_FF), F32), pltpu.VMEM((2, D_FF, d), F32),
                            pltpu.VMEM((d, 2 * D_FF), BF16), pltpu.VMEM((D_FF, d), BF16),
                            pltpu.SemaphoreType.DMA((2,))],
        ),
        out_shape=jax.ShapeDtypeStruct((n_rows, d), BF16),
        compiler_params=pltpu.CompilerParams(dimension_semantics=("arbitrary",), vmem_limit_bytes=VMEM_LIMIT),
        name="moe_experts",
    )(*tiles_info, xs, w_gate_up.reshape(n_stacked, d, 2 * D_FF),
      b_gate_up.reshape(n_stacked, 1, 2 * D_FF), w_down.reshape(n_stacked, D_FF, d), b_down.reshape(n_stacked, 1, d))


def _combine_kernel(loc_ref, n_ref, dst_ref, tot_ref, x_ref, slot_t_ref, gate_t_ref, ys_hbm, o_ref, buf, sems):
    c = pl.program_id(0)
    nc = pl.num_programs(0)
    par = c % 2
    start = lambda cp: cp.start()

    @pl.when(c == 0)
    def _():
        buf[...] = jnp.zeros(buf.shape, BF16)
        _for_each_segment(loc_ref, n_ref, dst_ref, c, buf.at[0], ys_hbm, sems.at[0], False, start)

    @pl.when(c + 1 < nc)
    def _():
        _for_each_segment(loc_ref, n_ref, dst_ref, c + 1, buf.at[1 - par], ys_hbm, sems.at[1 - par], False, start)

    _wait_chunk(tot_ref, c, buf.at[par], ys_hbm, sems.at[par], False)

    cols = lax.broadcasted_iota(jnp.int32, (CHUNK, CHUNK_ROWS), 1)
    st = slot_t_ref[0]
    gt = gate_t_ref[0]
    back = jnp.zeros((CHUNK, CHUNK_ROWS), F32)
    for k in range(TOP_K):
        back = jnp.where(cols == st[:, k:k + 1], gt[:, k:k + 1], back)
    back = back.astype(BF16)
    o_ref[...] = x_ref[...] + _dot(back, buf[par])


def _combine(x2d, ys, slots_t, gates_t, seg_loc, seg_n, seg_dst, seg_tot):
    t, d = x2d.shape
    nc = t // CHUNK
    return pl.pallas_call(
        _combine_kernel,
        grid_spec=pltpu.PrefetchScalarGridSpec(
            num_scalar_prefetch=4,
            grid=(nc,),
            in_specs=[
                pl.BlockSpec((CHUNK, d), lambda c, *_: (c, 0)),
                pl.BlockSpec((1, CHUNK, TOP_K), lambda c, *_: (c, 0, 0)),
                pl.BlockSpec((1, CHUNK, TOP_K), lambda c, *_: (c, 0, 0)),
                pl.BlockSpec(memory_space=pl.ANY),
            ],
            out_specs=pl.BlockSpec((CHUNK, d), lambda c, *_: (c, 0)),
            scratch_shapes=[pltpu.VMEM((2, CHUNK_ROWS, d), BF16), pltpu.SemaphoreType.DMA((2,))],
        ),
        out_shape=jax.ShapeDtypeStruct((t, d), F32),
        compiler_params=pltpu.CompilerParams(dimension_semantics=("arbitrary",), vmem_limit_bytes=VMEM_LIMIT),
        name="moe_combine",
    )(seg_loc, seg_n, seg_dst, seg_tot, x2d, slots_t, gates_t, ys)


def _moe_layer(x2d, routed, layer, w_gate_up, b_gate_up, w_down, b_down):
    t, d = x2d.shape
    nc = t // CHUNK
    n_tiles_max = (t * TOP_K + nc * N_EXPERTS * (SEG_ALIGN - 1)) // GEMM_TILE + N_EXPERTS
    hb, slots, gates, cnt = routed
    seg_loc, seg_n, seg_dst, seg_tot, tiles_info, gap = _segment_tables(cnt, n_tiles_max)
    xs = _dispatch(hb, slots, seg_loc, seg_n, seg_dst, seg_tot, gap, n_tiles_max * GEMM_TILE)
    ys = _experts(xs, tiles_info, layer, w_gate_up, b_gate_up, w_down, b_down)
    slots_t = jnp.swapaxes(slots[:, :TOP_K, :], 1, 2)
    gates_t = jnp.swapaxes(gates[:, :TOP_K, :], 1, 2)
    return _combine(x2d, ys, slots_t, gates_t, seg_loc, seg_n, seg_dst, seg_tot)


def _qkv_kernel(x_ref, g_ref, w_ref, qg_ref, kg_ref, q_ref, k_ref, v_ref):
    d = x_ref.shape[1]
    h = _rmsnorm(x_ref[...], g_ref[...]).astype(BF16)
    qkv = _dot(h, w_ref[...])
    width = 2 * LANES
    li = lax.broadcasted_iota(jnp.int32, (width, width), 0) // HEAD_DIM
    lj = lax.broadcasted_iota(jnp.int32, (width, width), 1) // HEAD_DIM
    same_head = jnp.where(li == lj, 1.0, 0.0).astype(BF16)

    def head_norm(v, gain, out_scale):
        parts = []
        for j in range(d // width):
            vj = v[:, j * width:(j + 1) * width]
            ss = _dot((vj * vj).astype(BF16), same_head)
            parts.append(vj * lax.rsqrt(ss * (1.0 / HEAD_DIM) + RMS_EPS))
        return (jnp.concatenate(parts, axis=-1) * gain * out_scale).astype(BF16)

    q_ref[...] = head_norm(qkv[:, :d], qg_ref[...], HEAD_DIM ** -0.5 * LOG2_E)
    k_ref[...] = head_norm(qkv[:, d:2 * d], kg_ref[...], 1.0)
    v_ref[...] = qkv[:, 2 * d:].astype(BF16)


def _qkv(x2d, g, w_qkv, q_gain, k_gain):
    t, d = x2d.shape
    tt = min(TOK_TILE, t)
    row = lambda i: (i, 0)
    fixed = lambda i: (0, 0)
    return pl.pallas_call(
        _qkv_kernel,
        grid=(t // tt,),
        in_specs=[
            pl.BlockSpec((tt, d), row),
            pl.BlockSpec((1, d), fixed),
            pl.BlockSpec((d, 3 * d), fixed),
            pl.BlockSpec((1, d), fixed),
            pl.BlockSpec((1, d), fixed),
        ],
        out_specs=[pl.BlockSpec((tt, d), row)] * 3,
        out_shape=[jax.ShapeDtypeStruct((t, d), BF16)] * 3,
        compiler_params=pltpu.CompilerParams(dimension_semantics=("arbitrary",), vmem_limit_bytes=VMEM_LIMIT),
        name="attn_qkv",
    )(x2d, g.reshape(1, d), w_qkv.astype(BF16), jnp.tile(q_gain, N_HEADS).reshape(1, d),
      jnp.tile(k_gain, N_HEADS).reshape(1, d))


def _attn_kernel(q_ref, k_ref, v_ref, o_ref, acc_ref, carry_ref):
    qi = pl.program_id(2)
    tq, kb = ATT_Q, ATT_K
    q = q_ref[0]
    head_of_lane = lax.broadcasted_iota(jnp.int32, (tq, LANES), 1) // HEAD_DIM
    q_heads = [jnp.where(head_of_lane == hd, q, jnp.zeros_like(q)) for hd in range(2)]
    r2 = lax.broadcasted_iota(jnp.int32, (2 * kb, 2 * kb), 0)
    c2 = lax.broadcasted_iota(jnp.int32, (2 * kb, 2 * kb), 1)
    suffix2 = jnp.where(jnp.logical_and(r2 // kb == c2 // kb, r2 >= c2), 1.0, 0.0).astype(BF16)

    acc_ref[...] = jnp.zeros(acc_ref.shape, F32)
    carry_ref[...] = jnp.zeros(carry_ref.shape, F32)

    def key_group(j0, diag_offsets, r0):
        nb = len(diag_offsets)
        rows = tq - r0
        start = pl.multiple_of(j0 * kb, kb)
        kk = k_ref[0, pl.ds(start, nb * kb), :]
        vv = v_ref[0, pl.ds(start, nb * kb), :]
        row = lax.broadcasted_iota(jnp.int32, (rows, kb), 0) + r0
        key = lax.broadcasted_iota(jnp.int32, (rows, kb), 1)
        z_all = [_dot_nt(q_heads[hd][r0:, :], kk) for hd in range(2)]
        carry = [carry_ref[hd, r0:, :] for hd in range(2)]
        probs = [[None] * nb for _ in range(2)]
        for b in reversed(range(nb)):
            causal = None if diag_offsets[b] is None else key + diag_offsets[b] < row
            zs, log_keeps = [], []
            for hd in range(2):
                z = z_all[hd][:, b * kb:(b + 1) * kb]
                neg = -z
                log_keep = jnp.minimum(neg, 0.0) - jnp.log(1.0 + jnp.exp2(jnp.minimum(z, neg))) * LOG2_E
                if causal is not None:
                    log_keep = jnp.where(causal, log_keep, 0.0)
                zs.append(z)
                log_keeps.append(log_keep.astype(BF16))
            sums = _dot(jnp.concatenate(log_keeps, axis=1), suffix2)
            for hd in range(2):
                s_in = sums[:, hd * kb:(hd + 1) * kb]
                a = jnp.exp2(zs[hd] + s_in + carry[hd])
                if causal is not None:
                    a = jnp.where(causal, a, 0.0)
                probs[hd][b] = a.astype(BF16)
                carry[hd] = carry[hd] + jnp.broadcast_to(s_in[:, 0:1], (rows, kb))
        for hd in range(2):
            carry_ref[hd, r0:, :] = carry[hd]
            acc_ref[hd, r0:, :] += _dot(jnp.concatenate(probs[hd], axis=1), vv)

    blocks_per_q = tq // kb
    for g in reversed(range(blocks_per_q // ATT_DIAG_GROUP)):
        offs = [(g * ATT_DIAG_GROUP + b) * kb for b in range(ATT_DIAG_GROUP)]
        key_group(qi * blocks_per_q + g * ATT_DIAG_GROUP, offs, offs[0])

    def body(s, c):
        key_group(qi * blocks_per_q - (s + 1) * ATT_GROUP, [None] * ATT_GROUP, 0)
        return c
    lax.fori_loop(0, qi * (blocks_per_q // ATT_GROUP), body, 0)
    o_ref[0] = jnp.where(head_of_lane == 0, acc_ref[0], acc_ref[1]).astype(BF16)


def _attention(q, k, v, b, s):
    d = q.shape[-1]
    q3, k3, v3 = (a.reshape(b, s, d) for a in (q, k, v))
    pairs = d // LANES
    o = pl.pallas_call(
        _attn_kernel,
        grid=(b, pairs, s // ATT_Q),
        in_specs=[
            pl.BlockSpec((1, ATT_Q, LANES), lambda bi, hp, i: (bi, i, hp)),
            pl.BlockSpec((1, s, LANES), lambda bi, hp, i: (bi, 0, hp)),
            pl.BlockSpec((1, s, LANES), lambda bi, hp, i: (bi, 0, hp)),
        ],
        out_specs=pl.BlockSpec((1, ATT_Q, LANES), lambda bi, hp, i: (bi, i, hp)),
        out_shape=jax.ShapeDtypeStruct((b, s, d), BF16),
        scratch_shapes=[pltpu.VMEM((2, ATT_Q, LANES), F32), pltpu.VMEM((2, ATT_Q, LANES), F32)],
        compiler_params=pltpu.CompilerParams(
            dimension_semantics=("arbitrary", "arbitrary", "arbitrary"), vmem_limit_bytes=VMEM_LIMIT),
        name="attn_core",
    )(q3, k3, v3)
    return o.reshape(b * s, d)


def _proj_route_kernel(x_ref, o_ref, w_ref, rg_ref, wt_hi_ref, wt_lo_ref, rb_ref,
                       y_ref, hb_ref, slot_ref, gate_ref, cnt_ref):
    x2 = x_ref[...] + _dot(o_ref[...], w_ref[...])
    y_ref[...] = x2
    _route_rows(x2, rg_ref, wt_hi_ref, wt_lo_ref, rb_ref, hb_ref, slot_ref, gate_ref, cnt_ref)


def _proj_route(x2d, o, w_o, ffn_g, router_w, router_b):
    t, d = x2d.shape
    route_in, route_out = _route_specs(d, lambda c: c)
    outs = pl.pallas_call(
        _proj_route_kernel,
        grid=(t // TOK_TILE,),
        in_specs=[
            pl.BlockSpec((TOK_TILE, d), lambda c: (c, 0)),
            pl.BlockSpec((TOK_TILE, d), lambda c: (c, 0)),
            pl.BlockSpec((d, d), lambda c: (0, 0)),
        ] + route_in,
        out_specs=[pl.BlockSpec((TOK_TILE, d), lambda c: (c, 0))] + route_out,
        out_shape=[jax.ShapeDtypeStruct((t, d), F32)] + _route_shapes(t, d),
        compiler_params=pltpu.CompilerParams(dimension_semantics=("arbitrary",), vmem_limit_bytes=VMEM_LIMIT),
        name="attn_proj_route",
    )(x2d, o, w_o.astype(BF16), *_route_args(ffn_g, router_w, router_b))
    return outs[0], outs[1:]


def _attention_route_layer(x2d, b, s, g, w_qkv, q_gain, k_gain, w_o, ffn_g, router_w, router_b):
    q, k, v = _qkv(x2d, g, w_qkv, q_gain, k_gain)
    o = _attention(q, k, v, b, s)
    return _proj_route(x2d, o, w_o, ffn_g, router_w, router_b)


def kernel(x, mix_norm, pool_w, pool_scale, w_qkv, q_norm, k_norm, w_o, ffn_norm, router_w, router_b,
           w_gate_up, b_gate_up, w_down, b_down):
    b, s, d = x.shape
    assert d == D_MODEL and s % ATT_Q == 0 and s % TOK_TILE == 0 and TOK_TILE % CHUNK == 0
    depth = mix_norm.shape[0]
    for i in range(depth):
        j = i // 2
        if i % 2 == 0:
            x2d, routed = _pool_route_layer(x, mix_norm[i], pool_w[j], pool_scale[j],
                                            ffn_norm[i], router_w[i], router_b[i])
        else:
            x2d, routed = _attention_route_layer(x.reshape(b * s, d), b, s, mix_norm[i], w_qkv[j], q_norm[j],
                                                 k_norm[j], w_o[j], ffn_norm[i], router_w[i], router_b[i])
        x2d = _moe_layer(x2d, routed, i, w_gate_up, b_gate_up, w_down, b_down)
        x = x2d.reshape(b, s, d)
    return x
```

```python
import functools

import jax
import jax.numpy as jnp
from jax import lax
from jax.experimental import pallas as pl
from jax.experimental.pallas import tpu as pltpu

D_MODEL = 1024
POOL_WINDOWS = (2, 4, 8, 16)
POOL_GROUP_DIM = D_MODEL // len(POOL_WINDOWS)
POOL_HALO = 16
HEAD_DIM = 64
N_HEADS = D_MODEL // HEAD_DIM
N_EXPERTS = 32
TOP_K = 4
D_FF = D_MODEL
SWIGLU_LIMIT = 7.0
SWIGLU_ALPHA = 1.702
RMS_EPS = 1e-6
LOG2_E = 1.4426950408889634

LANES = 128
SUBLANES = 8
VMEM_LIMIT = 56 * 1024 * 1024

CHUNK = 512
SEG_ALIGN = 2 * SUBLANES
CHUNK_ROWS = CHUNK * TOP_K + N_EXPERTS * SEG_ALIGN
GEMM_TILE = 512
GEMM_SUB = 256
TOK_TILE = 512
ATT_Q = 512
ATT_K = LANES
ATT_GROUP = 4
ATT_DIAG_GROUP = 2

F32 = jnp.float32
BF16 = jnp.bfloat16


def _dot(a, b):
    return jnp.dot(a, b, preferred_element_type=F32)


def _dot_nt(a, b):
    return lax.dot_general(a, b, (((1,), (1,)), ((), ())), preferred_element_type=F32)


def _split_bf16(v):
    hi = v.astype(BF16)
    lo = (v - hi.astype(F32)).astype(BF16)
    return hi, lo


def _rmsnorm(v, g):
    return v * lax.rsqrt(jnp.mean(v * v, axis=-1, keepdims=True) + RMS_EPS) * g


def _pool_route_kernel(x_ref, halo_ref, g_ref, w_ref, scale_ref, rg_ref, wt_hi_ref, wt_lo_ref, rb_ref,
                       o_ref, hb_ref, slot_ref, gate_ref, cnt_ref):
    x1 = _pool_tile(x_ref[0], halo_ref[0], pl.program_id(1), g_ref, w_ref, scale_ref)
    o_ref[0] = x1
    _route_rows(x1, rg_ref, wt_hi_ref, wt_lo_ref, rb_ref, hb_ref, slot_ref, gate_ref, cnt_ref)


def _pool_tile(x, halo, i, g_ref, w_ref, scale_ref):
    g = g_ref[...]
    h = _rmsnorm(x, g)
    hh = _rmsnorm(halo, g)
    hh = jnp.where(i > 0, hh, 0.0)
    a = jnp.concatenate([hh, h], axis=0)
    ts = x.shape[0]
    pos = i * ts + lax.broadcasted_iota(jnp.int32, (ts, 1), 0)
    outs = []
    for gi, w in enumerate(POOL_WINDOWS):
        lo, hi = gi * POOL_GROUP_DIM, (gi + 1) * POOL_GROUP_DIM
        s = a[:, lo:hi]
        span = 1
        while span < w:
            s = s + pltpu.roll(s, span, axis=0)
            span *= 2
        cnt = jnp.minimum(pos + 1, w).astype(F32)
        pooled = s[POOL_HALO:, :] / cnt - h[:, lo:hi]
        outs.append(_dot(pooled.astype(BF16), w_ref[gi]))
    return x + jnp.concatenate(outs, axis=-1) * scale_ref[...]


def _route_specs(d, step_of):
    fixed2 = lambda *_: (0, 0)
    per_step = TOK_TILE // CHUNK
    in_specs = [
        pl.BlockSpec((1, d), fixed2),
        pl.BlockSpec((N_EXPERTS, d), fixed2),
        pl.BlockSpec((N_EXPERTS, d), fixed2),
        pl.BlockSpec((N_EXPERTS, 1), fixed2),
    ]
    out_specs = [
        pl.BlockSpec((TOK_TILE, d), lambda *ids: (step_of(*ids), 0)),
        pl.BlockSpec((per_step, SUBLANES, CHUNK), lambda *ids: (step_of(*ids), 0, 0)),
        pl.BlockSpec((per_step, SUBLANES, CHUNK), lambda *ids: (step_of(*ids), 0, 0)),
        pl.BlockSpec((per_step, N_EXPERTS, LANES), lambda *ids: (step_of(*ids), 0, 0)),
    ]
    return in_specs, out_specs


def _route_args(g, router_w, router_b):
    wt_hi, wt_lo = _split_bf16(router_w.T)
    return g.reshape(1, -1), wt_hi, wt_lo, router_b.reshape(N_EXPERTS, 1)


def _route_shapes(t, d):
    nc = t // CHUNK
    return [
        jax.ShapeDtypeStruct((t, d), BF16),
        jax.ShapeDtypeStruct((nc, SUBLANES, CHUNK), jnp.int32),
        jax.ShapeDtypeStruct((nc, SUBLANES, CHUNK), F32),
        jax.ShapeDtypeStruct((nc, N_EXPERTS, LANES), jnp.int32),
    ]


def _pool_route_layer(x, g, w, scale, ffn_g, router_w, router_b):
    b, s, d = x.shape
    ts = TOK_TILE
    tiles_per_seq = s // ts
    blocks_per_tile = ts // POOL_HALO
    route_in, route_out = _route_specs(d, lambda bi, i: bi * tiles_per_seq + i)
    outs = pl.pallas_call(
        _pool_route_kernel,
        grid=(b, tiles_per_seq),
        in_specs=[
            pl.BlockSpec((1, ts, d), lambda bi, i: (bi, i, 0)),
            pl.BlockSpec((1, POOL_HALO, d), lambda bi, i: (bi, jnp.maximum(i * blocks_per_tile - 1, 0), 0)),
            pl.BlockSpec((1, d), lambda bi, i: (0, 0)),
            pl.BlockSpec((len(POOL_WINDOWS), POOL_GROUP_DIM, POOL_GROUP_DIM), lambda bi, i: (0, 0, 0)),
            pl.BlockSpec((1, d), lambda bi, i: (0, 0)),
        ] + route_in,
        out_specs=[pl.BlockSpec((1, ts, d), lambda bi, i: (bi, i, 0))] + route_out,
        out_shape=[jax.ShapeDtypeStruct(x.shape, F32)] + _route_shapes(b * s, d),
        compiler_params=pltpu.CompilerParams(
            dimension_semantics=("arbitrary", "arbitrary"), vmem_limit_bytes=VMEM_LIMIT),
        name="pool_route",
    )(x, x, g.reshape(1, d), w.astype(BF16), scale.reshape(1, d), *_route_args(ffn_g, router_w, router_b))
    return outs[0].reshape(b * s, d), outs[1:]


def _route_rows(x, g_ref, wt_hi_ref, wt_lo_ref, b_ref, hb_ref, slot_ref, gate_ref, cnt_ref):
    t = x.shape[0]
    h = _rmsnorm(x, g_ref[...])
    h_hi, h_lo = _split_bf16(h)
    hb_ref[...] = h_hi
    logits = (_dot_nt(wt_hi_ref[...], h_hi) + _dot_nt(wt_lo_ref[...], h_hi)
              + _dot_nt(wt_hi_ref[...], h_lo) + b_ref[...])
    eio = lax.broadcasted_iota(jnp.int32, (N_EXPERTS, t), 0).astype(F32)
    vals, idxs = [], []
    l = logits
    for _ in range(TOP_K):
        m = jnp.max(l, axis=0, keepdims=True)
        idx = jnp.min(jnp.where(l == m, eio, float(N_EXPERTS)), axis=0, keepdims=True)
        vals.append(m)
        idxs.append(idx)
        l = jnp.where(eio == idx, -jnp.inf, l)
    es = [jnp.exp(v - vals[0]) for v in vals]
    denom = es[0] + es[1] + es[2] + es[3]
    sel = jnp.zeros((N_EXPERTS, t), F32)
    for idx in idxs:
        sel = sel + jnp.where(eio == idx, 1.0, 0.0)
    ti = lax.broadcasted_iota(jnp.int32, (t, t), 0)
    tj = lax.broadcasted_iota(jnp.int32, (t, t), 1)
    upper = jnp.where(jnp.logical_and(ti <= tj, ti // CHUNK == tj // CHUNK), 1.0, 0.0).astype(BF16)
    rank = _dot(sel.astype(BF16), upper)
    chunk_of_lane = lax.broadcasted_iota(jnp.int32, (N_EXPERTS, t), 1) // CHUNK
    n = jnp.zeros((N_EXPERTS, t), F32)
    for ci in range(t // CHUNK):
        last = (ci + 1) * CHUNK - 1
        n = jnp.where(chunk_of_lane == ci, rank[:, last:last + 1], n)
    n_units = jnp.floor((n + (SEG_ALIGN - 1)) * (1.0 / SEG_ALIGN))
    strict_lower = jnp.where(lax.broadcasted_iota(jnp.int32, (N_EXPERTS, N_EXPERTS), 1)
                             < lax.broadcasted_iota(jnp.int32, (N_EXPERTS, N_EXPERTS), 0), 1.0, 0.0).astype(BF16)
    off = _dot(strict_lower, n_units.astype(BF16)) * SEG_ALIGN
    dest = off + rank - 1.0
    slots = [jnp.sum(jnp.where(eio == idxs[k], dest, 0.0), axis=0, keepdims=True).astype(jnp.int32)
             for k in range(TOP_K)]
    for ci in range(t // CHUNK):
        lanes = slice(ci * CHUNK, (ci + 1) * CHUNK)
        for k in range(TOP_K):
            slot_ref[ci, k:k + 1, :] = slots[k][:, lanes]
            gate_ref[ci, k:k + 1, :] = (es[k] / denom)[:, lanes]
        slot_ref[ci, TOP_K:, :] = jnp.zeros((SUBLANES - TOP_K, CHUNK), jnp.int32)
        gate_ref[ci, TOP_K:, :] = jnp.zeros((SUBLANES - TOP_K, CHUNK), F32)
        cnt_ref[ci] = (n_units[:, ci * CHUNK:ci * CHUNK + LANES] * SEG_ALIGN).astype(jnp.int32)


def _segment_tables(cnt, n_tiles_max):
    seg_n = cnt[:, :, 0]
    seg_loc = jnp.cumsum(seg_n, axis=1) - seg_n
    tot = jnp.sum(seg_n, axis=0)
    tiles = (tot + GEMM_TILE - 1) // GEMM_TILE
    tile_end = jnp.cumsum(tiles)
    tile_start = tile_end - tiles
    front_pad = tiles * GEMM_TILE - tot
    seg_dst = (tile_start * GEMM_TILE + front_pad)[None, :] + jnp.cumsum(seg_n, axis=0) - seg_n
    used = tile_end[-1]
    tile_ids = jnp.arange(n_tiles_max, dtype=jnp.int32)
    tile_expert = jnp.sum(tile_ids[:, None] >= tile_end[None, :], axis=1).astype(jnp.int32)
    last_expert = jnp.sum(jnp.maximum(used - 1, 0) >= tile_end).astype(jnp.int32)
    tile_expert = jnp.where(tile_ids < used, tile_expert, last_expert)
    owner = tile_expert[:, None] == jnp.arange(N_EXPERTS, dtype=jnp.int32)[None, :]
    pad_here = jnp.sum(jnp.where(owner, jnp.where(tile_ids[:, None] == tile_start[None, :], front_pad[None, :], 0), 0),
                       axis=1)
    tile_valid = jnp.where(tile_ids < used, GEMM_TILE - pad_here, 0)
    experts = jnp.arange(N_EXPERTS, dtype=jnp.int32)
    ordinal = jnp.cumsum((tiles > 0).astype(jnp.int32)) - 1
    later = jnp.where(jnp.logical_and(experts[None, :] > experts[:, None], (tiles > 0)[None, :]), experts[None, :],
                      N_EXPERTS)
    next_expert = jnp.min(later, axis=1)
    next_expert = jnp.where(next_expert < N_EXPERTS, next_expert, -1)
    tile_buf = jnp.sum(jnp.where(owner, (ordinal % 2)[None, :], 0), axis=1)
    tile_next = jnp.sum(jnp.where(owner, next_expert[None, :], 0), axis=1)
    i32 = lambda v: v.astype(jnp.int32).reshape(-1)
    gap = jnp.concatenate([tile_start * GEMM_TILE, front_pad, used[None]])
    seg_tot = jnp.sum(seg_n, axis=1)
    tiles_info = (tile_expert, i32(tile_valid), i32(used), i32(tile_buf), i32(tile_next))
    return i32(seg_loc), i32(seg_n), i32(seg_dst), i32(seg_tot), tiles_info, i32(gap)


def _segment_copy(loc_ref, n_ref, dst_ref, c, e, local_buf, sorted_hbm, sem, to_sorted):
    j = c * N_EXPERTS + e
    n = pl.multiple_of(n_ref[j], SEG_ALIGN)
    local = local_buf.at[pl.ds(pl.multiple_of(loc_ref[j], SEG_ALIGN), n)]
    remote = sorted_hbm.at[pl.ds(pl.multiple_of(dst_ref[j], SEG_ALIGN), n)]
    return n, (pltpu.make_async_copy(local, remote, sem) if to_sorted
               else pltpu.make_async_copy(remote, local, sem))


def _for_each_segment(loc_ref, n_ref, dst_ref, c, local_buf, sorted_hbm, sem, to_sorted, action):
    def body(e, carry):
        n, cp = _segment_copy(loc_ref, n_ref, dst_ref, c, e, local_buf, sorted_hbm, sem, to_sorted)

        @pl.when(n > 0)
        def _():
            action(cp)
        return carry
    lax.fori_loop(0, N_EXPERTS, body, 0)


def _wait_chunk(tot_ref, c, local_buf, sorted_hbm, sem, to_sorted):
    n = pl.multiple_of(tot_ref[c], SEG_ALIGN)
    local = local_buf.at[pl.ds(0, n)]
    remote = sorted_hbm.at[pl.ds(0, n)]
    (pltpu.make_async_copy(local, remote, sem) if to_sorted else pltpu.make_async_copy(remote, local, sem)).wait()


def _zero_unused_rows(gap_ref, xs_hbm, zeros, sem, action):
    def expert_gap(e, carry):
        n = pl.multiple_of(gap_ref[N_EXPERTS + e], SEG_ALIGN)

        @pl.when(n > 0)
        def _():
            action(pltpu.make_async_copy(
                zeros.at[pl.ds(0, n)], xs_hbm.at[pl.ds(pl.multiple_of(gap_ref[e], SEG_ALIGN), n)], sem))
        return carry
    lax.fori_loop(0, N_EXPERTS, expert_gap, 0)

    def unused_tile(i, carry):
        action(pltpu.make_async_copy(zeros, xs_hbm.at[pl.ds(pl.multiple_of(i * GEMM_TILE, GEMM_TILE), GEMM_TILE)], sem))
        return carry
    lax.fori_loop(gap_ref[2 * N_EXPERTS], xs_hbm.shape[0] // GEMM_TILE, unused_tile, 0)


def _dispatch_kernel(loc_ref, n_ref, dst_ref, tot_ref, gap_ref, hb_ref, slot_ref, xs_hbm, buf, zeros, sems):
    c = pl.program_id(0)
    nc = pl.num_programs(0)
    par = c % 2
    start = lambda cp: cp.start()
    wait = lambda cp: cp.wait()

    @pl.when(c == 0)
    def _():
        zeros[...] = jnp.zeros(zeros.shape, BF16)
        _zero_unused_rows(gap_ref, xs_hbm, zeros, sems.at[2], start)

    @pl.when(c >= 2)
    def _():
        _wait_chunk(tot_ref, c - 2, buf.at[par], xs_hbm, sems.at[par], True)

    rows = lax.broadcasted_iota(jnp.int32, (CHUNK_ROWS, CHUNK), 0)
    hit = rows == slot_ref[0, 0:1, :]
    for k in range(1, TOP_K):
        hit = jnp.logical_or(hit, rows == slot_ref[0, k:k + 1, :])
    onehot = jnp.where(hit, 1.0, 0.0).astype(BF16)
    buf[par] = _dot(onehot, hb_ref[...]).astype(BF16)
    _for_each_segment(loc_ref, n_ref, dst_ref, c, buf.at[par], xs_hbm, sems.at[par], True, start)

    @pl.when(c == nc - 1)
    def _():
        _zero_unused_rows(gap_ref, xs_hbm, zeros, sems.at[2], wait)

        @pl.when(c >= 1)
        def _():
            _wait_chunk(tot_ref, c - 1, buf.at[1 - par], xs_hbm, sems.at[1 - par], True)
        _wait_chunk(tot_ref, c, buf.at[par], xs_hbm, sems.at[par], True)


def _dispatch(hb, slots, seg_loc, seg_n, seg_dst, seg_tot, gap, n_rows):
    t, d = hb.shape
    nc = t // CHUNK
    return pl.pallas_call(
        _dispatch_kernel,
        grid_spec=pltpu.PrefetchScalarGridSpec(
            num_scalar_prefetch=5,
            grid=(nc,),
            in_specs=[
                pl.BlockSpec((CHUNK, d), lambda c, *_: (c, 0)),
                pl.BlockSpec((1, SUBLANES, CHUNK), lambda c, *_: (c, 0, 0)),
            ],
            out_specs=pl.BlockSpec(memory_space=pl.ANY),
            scratch_shapes=[pltpu.VMEM((2, CHUNK_ROWS, d), BF16), pltpu.VMEM((GEMM_TILE, d), BF16),
                            pltpu.SemaphoreType.DMA((3,))],
        ),
        out_shape=jax.ShapeDtypeStruct((n_rows, d), BF16),
        compiler_params=pltpu.CompilerParams(dimension_semantics=("arbitrary",), vmem_limit_bytes=VMEM_LIMIT),
        name="moe_dispatch",
    )(seg_loc, seg_n, seg_dst, seg_tot, gap, hb, slots)


def _expert_kernel(layer, te_ref, valid_ref, used_ref, buf_ref, next_ref, x_ref, wgu_hbm, bgu_ref, wd_hbm, bd_ref,
                   o_ref, wgu_f32, wd_f32, wgu_bf, wd_bf, sems):
    i = pl.program_id(0)
    prev = te_ref[jnp.maximum(i - 1, 0)]
    buf = buf_ref[i]

    def weight_copies(expert, b):
        e = layer * N_EXPERTS + expert
        return (pltpu.make_async_copy(wgu_hbm.at[e], wgu_f32.at[b], sems.at[b]),
                pltpu.make_async_copy(wd_hbm.at[e], wd_f32.at[b], sems.at[b]))

    @pl.when(i == 0)
    def _():
        for cp in weight_copies(te_ref[0], buf):
            cp.start()

    @pl.when(jnp.logical_or(i == 0, te_ref[i] != prev))
    def _():
        for cp in weight_copies(te_ref[i], buf):
            cp.wait()

        @pl.when(next_ref[i] >= 0)
        def _():
            for cp in weight_copies(next_ref[i], 1 - buf):
                cp.start()
        wgu_bf[...] = wgu_f32[buf].astype(BF16)
        wd_bf[...] = wd_f32[buf].astype(BF16)

    def expert_mlp(rows):
        gu = _dot(x_ref[rows, :], wgu_bf[...]) + bgu_ref[0]
        gate = jnp.minimum(gu[:, :D_FF], SWIGLU_LIMIT)
        up = jnp.clip(gu[:, D_FF:], -SWIGLU_LIMIT, SWIGLU_LIMIT)
        act = (up + 1.0) * (gate * (1.0 / (1.0 + jnp.exp(-SWIGLU_ALPHA * gate))))
        out = _dot(act.astype(BF16), wd_bf[...]) + bd_ref[0]
        o_ref[rows, :] = out.astype(BF16)

    n_sub = GEMM_TILE // GEMM_SUB

    @pl.when(valid_ref[i] > GEMM_TILE - GEMM_SUB)
    def _():
        expert_mlp(pl.ds(0, GEMM_TILE))

    @pl.when(valid_ref[i] <= GEMM_TILE - GEMM_SUB)
    def _():
        o_ref[pl.ds(0, GEMM_SUB), :] = jnp.zeros((GEMM_SUB, D_MODEL), BF16)
        for sub in range(1, n_sub):
            rows = pl.ds(sub * GEMM_SUB, GEMM_SUB)

            @pl.when(valid_ref[i] > GEMM_TILE - (sub + 1) * GEMM_SUB)
            def _():
                expert_mlp(rows)

            @pl.when(valid_ref[i] <= GEMM_TILE - (sub + 1) * GEMM_SUB)
            def _():
                o_ref[rows, :] = jnp.zeros((GEMM_SUB, D_MODEL), BF16)


def _experts(xs, tiles_info, layer, w_gate_up, b_gate_up, w_down, b_down):
    n_rows, d = xs.shape
    n_tiles = n_rows // GEMM_TILE
    n_stacked = w_gate_up.shape[0] * N_EXPERTS
    row_map = lambda i, te, tv, u, *_: (jnp.minimum(i, jnp.maximum(u[0] - 1, 0)), 0)
    exp_map = lambda i, te, *_: (layer * N_EXPERTS + te[i], 0, 0)
    return pl.pallas_call(
        functools.partial(_expert_kernel, layer),
        grid_spec=pltpu.PrefetchScalarGridSpec(
            num_scalar_prefetch=5,
            grid=(n_tiles,),
            in_specs=[
                pl.BlockSpec((GEMM_TILE, d), row_map),
                pl.BlockSpec(memory_space=pl.ANY),
                pl.BlockSpec((1, 1, 2 * D_FF), exp_map),
                pl.BlockSpec(memory_space=pl.ANY),
                pl.BlockSpec((1, 1, d), exp_map),
            ],
            out_specs=pl.BlockSpec((GEMM_TILE, d), lambda i, *_: (i, 0)),
            scratch_shapes=[pltpu.VMEM((2, d, 2 * D_FF), F32), pltpu.VMEM((2, D_FF, d), F32),
                            pltpu.VMEM((d, 2 * D_FF), BF16), pltpu.VMEM((D_FF, d), BF16),
                            pltpu.SemaphoreType.DMA((2,))],
        ),
        out_shape=jax.ShapeDtypeStruct((n_rows, d), BF16),
        compiler_params=pltpu.CompilerParams(dimension_semantics=("arbitrary",), vmem_limit_bytes=VMEM_LIMIT),
        name="moe_experts",
    )(*tiles_info, xs, w_gate_up.reshape(n_stacked, d, 2 * D_FF),
      b_gate_up.reshape(n_stacked, 1, 2 * D_FF), w_down.reshape(n_stacked, D_FF, d), b_down.reshape(n_stacked, 1, d))


def _combine_kernel(loc_ref, n_ref, dst_ref, tot_ref, x_ref, slot_t_ref, gate_t_ref, ys_hbm, o_ref, buf, sems):
    c = pl.program_id(0)
    nc = pl.num_programs(0)
    par = c % 2
    start = lambda cp: cp.start()

    @pl.when(c == 0)
    def _():
        buf[...] = jnp.zeros(buf.shape, BF16)
        _for_each_segment(loc_ref, n_ref, dst_ref, c, buf.at[0], ys_hbm, sems.at[0], False, start)

    @pl.when(c + 1 < nc)
    def _():
        _for_each_segment(loc_ref, n_ref, dst_ref, c + 1, buf.at[1 - par], ys_hbm, sems.at[1 - par], False, start)

    _wait_chunk(tot_ref, c, buf.at[par], ys_hbm, sems.at[par], False)

    cols = lax.broadcasted_iota(jnp.int32, (CHUNK, CHUNK_ROWS), 1)
    st = slot_t_ref[0]
    gt = gate_t_ref[0]
    back = jnp.zeros((CHUNK, CHUNK_ROWS), F32)
    for k in range(TOP_K):
        back = jnp.where(cols == st[:, k:k + 1], gt[:, k:k + 1], back)
    back = back.astype(BF16)
    o_ref[...] = x_ref[...] + _dot(back, buf[par])


def _combine(x2d, ys, slots_t, gates_t, seg_loc, seg_n, seg_dst, seg_tot):
    t, d = x2d.shape
    nc = t // CHUNK
    return pl.pallas_call(
        _combine_kernel,
        grid_spec=pltpu.PrefetchScalarGridSpec(
            num_scalar_prefetch=4,
            grid=(nc,),
            in_specs=[
                pl.BlockSpec((CHUNK, d), lambda c, *_: (c, 0)),
                pl.BlockSpec((1, CHUNK, TOP_K), lambda c, *_: (c, 0, 0)),
                pl.BlockSpec((1, CHUNK, TOP_K), lambda c, *_: (c, 0, 0)),
                pl.BlockSpec(memory_space=pl.ANY),
            ],
            out_specs=pl.BlockSpec((CHUNK, d), lambda c, *_: (c, 0)),
            scratch_shapes=[pltpu.VMEM((2, CHUNK_ROWS, d), BF16), pltpu.SemaphoreType.DMA((2,))],
        ),
        out_shape=jax.ShapeDtypeStruct((t, d), F32),
        compiler_params=pltpu.CompilerParams(dimension_semantics=("arbitrary",), vmem_limit_bytes=VMEM_LIMIT),
        name="moe_combine",
    )(seg_loc, seg_n, seg_dst, seg_tot, x2d, slots_t, gates_t, ys)


def _moe_layer(x2d, routed, layer, w_gate_up, b_gate_up, w_down, b_down):
    t, d = x2d.shape
    nc = t // CHUNK
    n_tiles_max = (t * TOP_K + nc * N_EXPERTS * (SEG_ALIGN - 1)) // GEMM_TILE + N_EXPERTS
    hb, slots, gates, cnt = routed
    seg_loc, seg_n, seg_dst, seg_tot, tiles_info, gap = _segment_tables(cnt, n_tiles_max)
    xs = _dispatch(hb, slots, seg_loc, seg_n, seg_dst, seg_tot, gap, n_tiles_max * GEMM_TILE)
    ys = _experts(xs, tiles_info, layer, w_gate_up, b_gate_up, w_down, b_down)
    slots_t = jnp.swapaxes(slots[:, :TOP_K, :], 1, 2)
    gates_t = jnp.swapaxes(gates[:, :TOP_K, :], 1, 2)
    return _combine(x2d, ys, slots_t, gates_t, seg_loc, seg_n, seg_dst, seg_tot)


def _qkv_kernel(x_ref, g_ref, w_ref, qg_ref, kg_ref, q_ref, k_ref, v_ref):
    d = x_ref.shape[1]
    h = _rmsnorm(x_ref[...], g_ref[...]).astype(BF16)
    qkv = _dot(h, w_ref[...])
    width = 2 * LANES
    li = lax.broadcasted_iota(jnp.int32, (width, width), 0) // HEAD_DIM
    lj = lax.broadcasted_iota(jnp.int32, (width, width), 1) // HEAD_DIM
    same_head = jnp.where(li == lj, 1.0, 0.0).astype(BF16)

    def head_norm(v, gain, out_scale):
        parts = []
        for j in range(d // width):
            vj = v[:, j * width:(j + 1) * width]
            ss = _dot((vj * vj).astype(BF16), same_head)
            parts.append(vj * lax.rsqrt(ss * (1.0 / HEAD_DIM) + RMS_EPS))
        return (jnp.concatenate(parts, axis=-1) * gain * out_scale).astype(BF16)

    q_ref[...] = head_norm(qkv[:, :d], qg_ref[...], HEAD_DIM ** -0.5 * LOG2_E)
    k_ref[...] = head_norm(qkv[:, d:2 * d], kg_ref[...], 1.0)
    v_ref[...] = qkv[:, 2 * d:].astype(BF16)


def _qkv(x2d, g, w_qkv, q_gain, k_gain):
    t, d = x2d.shape
    tt = min(TOK_TILE, t)
    row = lambda i: (i, 0)
    fixed = lambda i: (0, 0)
    return pl.pallas_call(
        _qkv_kernel,
        grid=(t // tt,),
        in_specs=[
            pl.BlockSpec((tt, d), row),
            pl.BlockSpec((1, d), fixed),
            pl.BlockSpec((d, 3 * d), fixed),
            pl.BlockSpec((1, d), fixed),
            pl.BlockSpec((1, d), fixed),
        ],
        out_specs=[pl.BlockSpec((tt, d), row)] * 3,
        out_shape=[jax.ShapeDtypeStruct((t, d), BF16)] * 3,
        compiler_params=pltpu.CompilerParams(dimension_semantics=("arbitrary",), vmem_limit_bytes=VMEM_LIMIT),
        name="attn_qkv",
    )(x2d, g.reshape(1, d), w_qkv.astype(BF16), jnp.tile(q_gain, N_HEADS).reshape(1, d),
      jnp.tile(k_gain, N_HEADS).reshape(1, d))


def _attn_kernel(q_ref, k_ref, v_ref, o_ref, acc_ref, carry_ref):
    qi = pl.program_id(2)
    tq, kb = ATT_Q, ATT_K
    q = q_ref[0]
    head_of_lane = lax.broadcasted_iota(jnp.int32, (tq, LANES), 1) // HEAD_DIM
    q_heads = [jnp.where(head_of_lane == hd, q, jnp.zeros_like(q)) for hd in range(2)]
    r2 = lax.broadcasted_iota(jnp.int32, (2 * kb, 2 * kb), 0)
    c2 = lax.broadcasted_iota(jnp.int32, (2 * kb, 2 * kb), 1)
    suffix2 = jnp.where(jnp.logical_and(r2 // kb == c2 // kb, r2 >= c2), 1.0, 0.0).astype(BF16)

    acc_ref[...] = jnp.zeros(acc_ref.shape, F32)
    carry_ref[...] = jnp.zeros(carry_ref.shape, F32)

    def key_group(j0, diag_offsets, r0):
        nb = len(diag_offsets)
        rows = tq - r0
        start = pl.multiple_of(j0 * kb, kb)
        kk = k_ref[0, pl.ds(start, nb * kb), :]
        vv = v_ref[0, pl.ds(start, nb * kb), :]
        row = lax.broadcasted_iota(jnp.int32, (rows, kb), 0) + r0
        key = lax.broadcasted_iota(jnp.int32, (rows, kb), 1)
        z_all = [_dot_nt(q_heads[hd][r0:, :], kk) for hd in range(2)]
        carry = [carry_ref[hd, r0:, :] for hd in range(2)]
        probs = [[None] * nb for _ in range(2)]
        for b in reversed(range(nb)):
            causal = None if diag_offsets[b] is None else key + diag_offsets[b] < row
            zs, log_keeps = [], []
            for hd in range(2):
                z = z_all[hd][:, b * kb:(b + 1) * kb]
                neg = -z
                log_keep = jnp.minimum(neg, 0.0) - jnp.log(1.0 + jnp.exp2(jnp.minimum(z, neg))) * LOG2_E
                if causal is not None:
                    log_keep = jnp.where(causal, log_keep, 0.0)
                zs.append(z)
                log_keeps.append(log_keep.astype(BF16))
            sums = _dot(jnp.concatenate(log_keeps, axis=1), suffix2)
            for hd in range(2):
                s_in = sums[:, hd * kb:(hd + 1) * kb]
                a = jnp.exp2(zs[hd] + s_in + carry[hd])
                if causal is not None:
                    a = jnp.where(causal, a, 0.0)
                probs[hd][b] = a.astype(BF16)
                carry[hd] = carry[hd] + jnp.broadcast_to(s_in[:, 0:1], (rows, kb))
        for hd in range(2):
            carry_ref[hd, r0:, :] = carry[hd]
            acc_ref[hd, r0:, :] += _dot(jnp.concatenate(probs[hd], axis=1), vv)

    blocks_per_q = tq // kb
    for g in reversed(range(blocks_per_q // ATT_DIAG_GROUP)):
        offs = [(g * ATT_DIAG_GROUP + b) * kb for b in range(ATT_DIAG_GROUP)]
        key_group(qi * blocks_per_q + g * ATT_DIAG_GROUP, offs, offs[0])

    def body(s, c):
        key_group(qi * blocks_per_q - (s + 1) * ATT_GROUP, [None] * ATT_GROUP, 0)
        return c
    lax.fori_loop(0, qi * (blocks_per_q // ATT_GROUP), body, 0)
    o_ref[0] = jnp.where(head_of_lane == 0, acc_ref[0], acc_ref[1]).astype(BF16)


def _attention(q, k, v, b, s):
    d = q.shape[-1]
    q3, k3, v3 = (a.reshape(b, s, d) for a in (q, k, v))
    pairs = d // LANES
    o = pl.pallas_call(
        _attn_kernel,
        grid=(b, pairs, s // ATT_Q),
        in_specs=[
            pl.BlockSpec((1, ATT_Q, LANES), lambda bi, hp, i: (bi, i, hp)),
            pl.BlockSpec((1, s, LANES), lambda bi, hp, i: (bi, 0, hp)),
            pl.BlockSpec((1, s, LANES), lambda bi, hp, i: (bi, 0, hp)),
        ],
        out_specs=pl.BlockSpec((1, ATT_Q, LANES), lambda bi, hp, i: (bi, i, hp)),
        out_shape=jax.ShapeDtypeStruct((b, s, d), BF16),
        scratch_shapes=[pltpu.VMEM((2, ATT_Q, LANES), F32), pltpu.VMEM((2, ATT_Q, LANES), F32)],
        compiler_params=pltpu.CompilerParams(
            dimension_semantics=("arbitrary", "arbitrary", "arbitrary"), vmem_limit_bytes=VMEM_LIMIT),
        name="attn_core",
    )(q3, k3, v3)
    return o.reshape(b * s, d)


def _proj_route_kernel(x_ref, o_ref, w_ref, rg_ref, wt_hi_ref, wt_lo_ref, rb_ref,
                       y_ref, hb_ref, slot_ref, gate_ref, cnt_ref):
    x2 = x_ref[...] + _dot(o_ref[...], w_ref[...])
    y_ref[...] = x2
    _route_rows(x2, rg_ref, wt_hi_ref, wt_lo_ref, rb_ref, hb_ref, slot_ref, gate_ref, cnt_ref)


def _proj_route(x2d, o, w_o, ffn_g, router_w, router_b):
    t, d = x2d.shape
    route_in, route_out = _route_specs(d, lambda c: c)
    outs = pl.pallas_call(
        _proj_route_kernel,
        grid=(t // TOK_TILE,),
        in_specs=[
            pl.BlockSpec((TOK_TILE, d), lambda c: (c, 0)),
            pl.BlockSpec((TOK_TILE, d), lambda c: (c, 0)),
            pl.BlockSpec((d, d), lambda c: (0, 0)),
        ] + route_in,
        out_specs=[pl.BlockSpec((TOK_TILE, d), lambda c: (c, 0))] + route_out,
        out_shape=[jax.ShapeDtypeStruct((t, d), F32)] + _route_shapes(t, d),
        compiler_params=pltpu.CompilerParams(dimension_semantics=("arbitrary",), vmem_limit_bytes=VMEM_LIMIT),
        name="attn_proj_route",
    )(x2d, o, w_o.astype(BF16), *_route_args(ffn_g, router_w, router_b))
    return outs[0], outs[1:]


def _attention_route_layer(x2d, b, s, g, w_qkv, q_gain, k_gain, w_o, ffn_g, router_w, router_b):
    q, k, v = _qkv(x2d, g, w_qkv, q_gain, k_gain)
    o = _attention(q, k, v, b, s)
    return _proj_route(x2d, o, w_o, ffn_g, router_w, router_b)


def kernel(x, mix_norm, pool_w, pool_scale, w_qkv, q_norm, k_norm, w_o, ffn_norm, router_w, router_b,
           w_gate_up, b_gate_up, w_down, b_down):
    b, s, d = x.shape
    assert d == D_MODEL and s % ATT_Q == 0 and s % TOK_TILE == 0 and TOK_TILE % CHUNK == 0
    depth = mix_norm.shape[0]
    for i in range(depth):
        j = i // 2
        if i % 2 == 0:
            x2d, routed = _pool_route_layer(x, mix_norm[i], pool_w[j], pool_scale[j],
                                            ffn_norm[i], router_w[i], router_b[i])
        else:
            x2d, routed = _attention_route_layer(x.reshape(b * s, d), b, s, mix_norm[i], w_qkv[j], q_norm[j],
                                                 k_norm[j], w_o[j], ffn_norm[i], router_w[i], router_b[i])
        x2d = _moe_layer(x2d, routed, i, w_gate_up, b_gate_up, w_down, b_down)
        x = x2d.reshape(b, s, d)
    return x
```

```python
import functools

import jax
import jax.numpy as jnp
from jax import lax
from jax.experimental import pallas as pl
from jax.experimental.pallas import tpu as pltpu

D_MODEL = 1024
POOL_WINDOWS = (2, 4, 8, 16)
POOL_GROUP_DIM = D_MODEL // len(POOL_WINDOWS)
POOL_HALO = 16
HEAD_DIM = 64
N_HEADS = D_MODEL // HEAD_DIM
N_EXPERTS = 32
TOP_K = 4
D_FF = D_MODEL
SWIGLU_LIMIT = 7.0
SWIGLU_ALPHA = 1.702
RMS_EPS = 1e-6
LOG2_E = 1.4426950408889634

LANES = 128
SUBLANES = 8
VMEM_LIMIT = 56 * 1024 * 1024

CHUNK = 512
SEG_ALIGN = 2 * SUBLANES
CHUNK_ROWS = CHUNK * TOP_K + N_EXPERTS * SEG_ALIGN
GEMM_TILE = 512
GEMM_SUB = 256
TOK_TILE = 512
ATT_Q = 1024
ATT_K = LANES
ATT_GROUP = 4
ATT_DIAG_GROUP = 2

F32 = jnp.float32
BF16 = jnp.bfloat16


def _dot(a, b):
    return jnp.dot(a, b, preferred_element_type=F32)


def _dot_nt(a, b):
    return lax.dot_general(a, b, (((1,), (1,)), ((), ())), preferred_element_type=F32)


def _split_bf16(v):
    hi = v.astype(BF16)
    lo = (v - hi.astype(F32)).astype(BF16)
    return hi, lo


def _rmsnorm(v, g):
    return v * lax.rsqrt(jnp.mean(v * v, axis=-1, keepdims=True) + RMS_EPS) * g


def _pool_route_kernel(x_ref, halo_ref, g_ref, w_ref, scale_ref, rg_ref, wt_hi_ref, wt_lo_ref, rb_ref,
                       o_ref, hb_ref, slot_ref, gate_ref, cnt_ref):
    x1 = _pool_tile(x_ref[0], halo_ref[0], pl.program_id(1), g_ref, w_ref, scale_ref)
    o_ref[0] = x1
    _route_rows(x1, rg_ref, wt_hi_ref, wt_lo_ref, rb_ref, hb_ref, slot_ref, gate_ref, cnt_ref)


def _pool_tile(x, halo, i, g_ref, w_ref, scale_ref):
    g = g_ref[...]
    h = _rmsnorm(x, g)
    hh = _rmsnorm(halo, g)
    hh = jnp.where(i > 0, hh, 0.0)
    a = jnp.concatenate([hh, h], axis=0)
    ts = x.shape[0]
    pos = i * ts + lax.broadcasted_iota(jnp.int32, (ts, 1), 0)
    outs = []
    for gi, w in enumerate(POOL_WINDOWS):
        lo, hi = gi * POOL_GROUP_DIM, (gi + 1) * POOL_GROUP_DIM
        s = a[:, lo:hi]
        span = 1
        while span < w:
            s = s + pltpu.roll(s, span, axis=0)
            span *= 2
        cnt = jnp.minimum(pos + 1, w).astype(F32)
        pooled = s[POOL_HALO:, :] / cnt - h[:, lo:hi]
        outs.append(_dot(pooled.astype(BF16), w_ref[gi]))
    return x + jnp.concatenate(outs, axis=-1) * scale_ref[...]


def _route_specs(d, step_of):
    fixed2 = lambda *_: (0, 0)
    per_step = TOK_TILE // CHUNK
    in_specs = [
        pl.BlockSpec((1, d), fixed2),
        pl.BlockSpec((N_EXPERTS, d), fixed2),
        pl.BlockSpec((N_EXPERTS, d), fixed2),
        pl.BlockSpec((N_EXPERTS, 1), fixed2),
    ]
    out_specs = [
        pl.BlockSpec((TOK_TILE, d), lambda *ids: (step_of(*ids), 0)),
        pl.BlockSpec((per_step, SUBLANES, CHUNK), lambda *ids: (step_of(*ids), 0, 0)),
        pl.BlockSpec((per_step, SUBLANES, CHUNK), lambda *ids: (step_of(*ids), 0, 0)),
        pl.BlockSpec((per_step, N_EXPERTS, LANES), lambda *ids: (step_of(*ids), 0, 0)),
    ]
    return in_specs, out_specs


def _route_args(g, router_w, router_b):
    wt_hi, wt_lo = _split_bf16(router_w.T)
    return g.reshape(1, -1), wt_hi, wt_lo, router_b.reshape(N_EXPERTS, 1)


def _route_shapes(t, d):
    nc = t // CHUNK
    return [
        jax.ShapeDtypeStruct((t, d), BF16),
        jax.ShapeDtypeStruct((nc, SUBLANES, CHUNK), jnp.int32),
        jax.ShapeDtypeStruct((nc, SUBLANES, CHUNK), F32),
        jax.ShapeDtypeStruct((nc, N_EXPERTS, LANES), jnp.int32),
    ]


def _pool_route_layer(x, g, w, scale, ffn_g, router_w, router_b):
    b, s, d = x.shape
    ts = TOK_TILE
    tiles_per_seq = s // ts
    blocks_per_tile = ts // POOL_HALO
    route_in, route_out = _route_specs(d, lambda bi, i: bi * tiles_per_seq + i)
    outs = pl.pallas_call(
        _pool_route_kernel,
        grid=(b, tiles_per_seq),
        in_specs=[
            pl.BlockSpec((1, ts, d), lambda bi, i: (bi, i, 0)),
            pl.BlockSpec((1, POOL_HALO, d), lambda bi, i: (bi, jnp.maximum(i * blocks_per_tile - 1, 0), 0)),
            pl.BlockSpec((1, d), lambda bi, i: (0, 0)),
            pl.BlockSpec((len(POOL_WINDOWS), POOL_GROUP_DIM, POOL_GROUP_DIM), lambda bi, i: (0, 0, 0)),
            pl.BlockSpec((1, d), lambda bi, i: (0, 0)),
        ] + route_in,
        out_specs=[pl.BlockSpec((1, ts, d), lambda bi, i: (bi, i, 0))] + route_out,
        out_shape=[jax.ShapeDtypeStruct(x.shape, F32)] + _route_shapes(b * s, d),
        compiler_params=pltpu.CompilerParams(
            dimension_semantics=("arbitrary", "arbitrary"), vmem_limit_bytes=VMEM_LIMIT),
        name="pool_route",
    )(x, x, g.reshape(1, d), w.astype(BF16), scale.reshape(1, d), *_route_args(ffn_g, router_w, router_b))
    return outs[0].reshape(b * s, d), outs[1:]


def _route_rows(x, g_ref, wt_hi_ref, wt_lo_ref, b_ref, hb_ref, slot_ref, gate_ref, cnt_ref):
    t = x.shape[0]
    h = _rmsnorm(x, g_ref[...])
    h_hi, h_lo = _split_bf16(h)
    hb_ref[...] = h_hi
    logits = (_dot_nt(wt_hi_ref[...], h_hi) + _dot_nt(wt_lo_ref[...], h_hi)
              + _dot_nt(wt_hi_ref[...], h_lo) + b_ref[...])
    eio = lax.broadcasted_iota(jnp.int32, (N_EXPERTS, t), 0).astype(F32)
    vals, idxs = [], []
    l = logits
    for _ in range(TOP_K):
        m = jnp.max(l, axis=0, keepdims=True)
        idx = jnp.min(jnp.where(l == m, eio, float(N_EXPERTS)), axis=0, keepdims=True)
        vals.append(m)
        idxs.append(idx)
        l = jnp.where(eio == idx, -jnp.inf, l)
    es = [jnp.exp(v - vals[0]) for v in vals]
    denom = es[0] + es[1] + es[2] + es[3]
    sel = jnp.zeros((N_EXPERTS, t), F32)
    for idx in idxs:
        sel = sel + jnp.where(eio == idx, 1.0, 0.0)
    ti = lax.broadcasted_iota(jnp.int32, (t, t), 0)
    tj = lax.broadcasted_iota(jnp.int32, (t, t), 1)
    upper = jnp.where(jnp.logical_and(ti <= tj, ti // CHUNK == tj // CHUNK), 1.0, 0.0).astype(BF16)
    rank = _dot(sel.astype(BF16), upper)
    chunk_of_lane = lax.broadcasted_iota(jnp.int32, (N_EXPERTS, t), 1) // CHUNK
    n = jnp.zeros((N_EXPERTS, t), F32)
    for ci in range(t // CHUNK):
        last = (ci + 1) * CHUNK - 1
        n = jnp.where(chunk_of_lane == ci, rank[:, last:last + 1], n)
    n_units = jnp.floor((n + (SEG_ALIGN - 1)) * (1.0 / SEG_ALIGN))
    strict_lower = jnp.where(lax.broadcasted_iota(jnp.int32, (N_EXPERTS, N_EXPERTS), 1)
                             < lax.broadcasted_iota(jnp.int32, (N_EXPERTS, N_EXPERTS), 0), 1.0, 0.0).astype(BF16)
    off = _dot(strict_lower, n_units.astype(BF16)) * SEG_ALIGN
    dest = off + rank - 1.0
    slots = [jnp.sum(jnp.where(eio == idxs[k], dest, 0.0), axis=0, keepdims=True).astype(jnp.int32)
             for k in range(TOP_K)]
    for ci in range(t // CHUNK):
        lanes = slice(ci * CHUNK, (ci + 1) * CHUNK)
        for k in range(TOP_K):
            slot_ref[ci, k:k + 1, :] = slots[k][:, lanes]
            gate_ref[ci, k:k + 1, :] = (es[k] / denom)[:, lanes]
        slot_ref[ci, TOP_K:, :] = jnp.zeros((SUBLANES - TOP_K, CHUNK), jnp.int32)
        gate_ref[ci, TOP_K:, :] = jnp.zeros((SUBLANES - TOP_K, CHUNK), F32)
        cnt_ref[ci] = (n_units[:, ci * CHUNK:ci * CHUNK + LANES] * SEG_ALIGN).astype(jnp.int32)


def _segment_tables(cnt, n_tiles_max):
    seg_n = cnt[:, :, 0]
    seg_loc = jnp.cumsum(seg_n, axis=1) - seg_n
    tot = jnp.sum(seg_n, axis=0)
    tiles = (tot + GEMM_TILE - 1) // GEMM_TILE
    tile_end = jnp.cumsum(tiles)
    tile_start = tile_end - tiles
    front_pad = tiles * GEMM_TILE - tot
    seg_dst = (tile_start * GEMM_TILE + front_pad)[None, :] + jnp.cumsum(seg_n, axis=0) - seg_n
    used = tile_end[-1]
    tile_ids = jnp.arange(n_tiles_max, dtype=jnp.int32)
    tile_expert = jnp.sum(tile_ids[:, None] >= tile_end[None, :], axis=1).astype(jnp.int32)
    last_expert = jnp.sum(jnp.maximum(used - 1, 0) >= tile_end).astype(jnp.int32)
    tile_expert = jnp.where(tile_ids < used, tile_expert, last_expert)
    owner = tile_expert[:, None] == jnp.arange(N_EXPERTS, dtype=jnp.int32)[None, :]
    pad_here = jnp.sum(jnp.where(owner, jnp.where(tile_ids[:, None] == tile_start[None, :], front_pad[None, :], 0), 0),
                       axis=1)
    tile_valid = jnp.where(tile_ids < used, GEMM_TILE - pad_here, 0)
    experts = jnp.arange(N_EXPERTS, dtype=jnp.int32)
    ordinal = jnp.cumsum((tiles > 0).astype(jnp.int32)) - 1
    later = jnp.where(jnp.logical_and(experts[None, :] > experts[:, None], (tiles > 0)[None, :]), experts[None, :],
                      N_EXPERTS)
    next_expert = jnp.min(later, axis=1)
    next_expert = jnp.where(next_expert < N_EXPERTS, next_expert, -1)
    tile_buf = jnp.sum(jnp.where(owner, (ordinal % 2)[None, :], 0), axis=1)
    tile_next = jnp.sum(jnp.where(owner, next_expert[None, :], 0), axis=1)
    i32 = lambda v: v.astype(jnp.int32).reshape(-1)
    gap = jnp.concatenate([tile_start * GEMM_TILE, front_pad, used[None]])
    seg_tot = jnp.sum(seg_n, axis=1)
    tiles_info = (tile_expert, i32(tile_valid), i32(used), i32(tile_buf), i32(tile_next))
    return i32(seg_loc), i32(seg_n), i32(seg_dst), i32(seg_tot), tiles_info, i32(gap)


def _segment_copy(loc_ref, n_ref, dst_ref, c, e, local_buf, sorted_hbm, sem, to_sorted):
    j = c * N_EXPERTS + e
    n = pl.multiple_of(n_ref[j], SEG_ALIGN)
    local = local_buf.at[pl.ds(pl.multiple_of(loc_ref[j], SEG_ALIGN), n)]
    remote = sorted_hbm.at[pl.ds(pl.multiple_of(dst_ref[j], SEG_ALIGN), n)]
    return n, (pltpu.make_async_copy(local, remote, sem) if to_sorted
               else pltpu.make_async_copy(remote, local, sem))


def _for_each_segment(loc_ref, n_ref, dst_ref, c, local_buf, sorted_hbm, sem, to_sorted, action):
    def body(e, carry):
        n, cp = _segment_copy(loc_ref, n_ref, dst_ref, c, e, local_buf, sorted_hbm, sem, to_sorted)

        @pl.when(n > 0)
        def _():
            action(cp)
        return carry
    lax.fori_loop(0, N_EXPERTS, body, 0)


def _wait_chunk(tot_ref, c, local_buf, sorted_hbm, sem, to_sorted):
    n = pl.multiple_of(tot_ref[c], SEG_ALIGN)
    local = local_buf.at[pl.ds(0, n)]
    remote = sorted_hbm.at[pl.ds(0, n)]
    (pltpu.make_async_copy(local, remote, sem) if to_sorted else pltpu.make_async_copy(remote, local, sem)).wait()


def _zero_unused_rows(gap_ref, xs_hbm, zeros, sem, action):
    def expert_gap(e, carry):
        n = pl.multiple_of(gap_ref[N_EXPERTS + e], SEG_ALIGN)

        @pl.when(n > 0)
        def _():
            action(pltpu.make_async_copy(
                zeros.at[pl.ds(0, n)], xs_hbm.at[pl.ds(pl.multiple_of(gap_ref[e], SEG_ALIGN), n)], sem))
        return carry
    lax.fori_loop(0, N_EXPERTS, expert_gap, 0)

    def unused_tile(i, carry):
        action(pltpu.make_async_copy(zeros, xs_hbm.at[pl.ds(pl.multiple_of(i * GEMM_TILE, GEMM_TILE), GEMM_TILE)], sem))
        return carry
    lax.fori_loop(gap_ref[2 * N_EXPERTS], xs_hbm.shape[0] // GEMM_TILE, unused_tile, 0)


def _dispatch_kernel(loc_ref, n_ref, dst_ref, tot_ref, gap_ref, hb_ref, slot_ref, xs_hbm, buf, zeros, sems):
    c = pl.program_id(0)
    nc = pl.num_programs(0)
    par = c % 2
    start = lambda cp: cp.start()
    wait = lambda cp: cp.wait()

    @pl.when(c == 0)
    def _():
        zeros[...] = jnp.zeros(zeros.shape, BF16)
        _zero_unused_rows(gap_ref, xs_hbm, zeros, sems.at[2], start)

    @pl.when(c >= 2)
    def _():
        _wait_chunk(tot_ref, c - 2, buf.at[par], xs_hbm, sems.at[par], True)

    rows = lax.broadcasted_iota(jnp.int32, (CHUNK_ROWS, CHUNK), 0)
    hit = rows == slot_ref[0, 0:1, :]
    for k in range(1, TOP_K):
        hit = jnp.logical_or(hit, rows == slot_ref[0, k:k + 1, :])
    onehot = jnp.where(hit, 1.0, 0.0).astype(BF16)
    buf[par] = _dot(onehot, hb_ref[...]).astype(BF16)
    _for_each_segment(loc_ref, n_ref, dst_ref, c, buf.at[par], xs_hbm, sems.at[par], True, start)

    @pl.when(c == nc - 1)
    def _():
        _zero_unused_rows(gap_ref, xs_hbm, zeros, sems.at[2], wait)

        @pl.when(c >= 1)
        def _():
            _wait_chunk(tot_ref, c - 1, buf.at[1 - par], xs_hbm, sems.at[1 - par], True)
        _wait_chunk(tot_ref, c, buf.at[par], xs_hbm, sems.at[par], True)


def _dispatch(hb, slots, seg_loc, seg_n, seg_dst, seg_tot, gap, n_rows):
    t, d = hb.shape
    nc = t // CHUNK
    return pl.pallas_call(
        _dispatch_kernel,
        grid_spec=pltpu.PrefetchScalarGridSpec(
            num_scalar_prefetch=5,
            grid=(nc,),
            in_specs=[
                pl.BlockSpec((CHUNK, d), lambda c, *_: (c, 0)),
                pl.BlockSpec((1, SUBLANES, CHUNK), lambda c, *_: (c, 0, 0)),
            ],
            out_specs=pl.BlockSpec(memory_space=pl.ANY),
            scratch_shapes=[pltpu.VMEM((2, CHUNK_ROWS, d), BF16), pltpu.VMEM((GEMM_TILE, d), BF16),
                            pltpu.SemaphoreType.DMA((3,))],
        ),
        out_shape=jax.ShapeDtypeStruct((n_rows, d), BF16),
        compiler_params=pltpu.CompilerParams(dimension_semantics=("arbitrary",), vmem_limit_bytes=VMEM_LIMIT),
        name="moe_dispatch",
    )(seg_loc, seg_n, seg_dst, seg_tot, gap, hb, slots)


def _expert_kernel(layer, te_ref, valid_ref, used_ref, buf_ref, next_ref, x_ref, wgu_hbm, bgu_ref, wd_hbm, bd_ref,
                   o_ref, wgu_f32, wd_f32, wgu_bf, wd_bf, sems):
    i = pl.program_id(0)
    prev = te_ref[jnp.maximum(i - 1, 0)]
    buf = buf_ref[i]

    def weight_copies(expert, b):
        e = layer * N_EXPERTS + expert
        return (pltpu.make_async_copy(wgu_hbm.at[e], wgu_f32.at[b], sems.at[b]),
                pltpu.make_async_copy(wd_hbm.at[e], wd_f32.at[b], sems.at[b]))

    @pl.when(i == 0)
    def _():
        for cp in weight_copies(te_ref[0], buf):
            cp.start()

    @pl.when(jnp.logical_or(i == 0, te_ref[i] != prev))
    def _():
        for cp in weight_copies(te_ref[i], buf):
            cp.wait()

        @pl.when(next_ref[i] >= 0)
        def _():
            for cp in weight_copies(next_ref[i], 1 - buf):
                cp.start()
        wgu_bf[...] = wgu_f32[buf].astype(BF16)
        wd_bf[...] = wd_f32[buf].astype(BF16)

    def expert_mlp(rows):
        gu = _dot(x_ref[rows, :], wgu_bf[...]) + bgu_ref[0]
        gate = jnp.minimum(gu[:, :D_FF], SWIGLU_LIMIT)
        up = jnp.clip(gu[:, D_FF:], -SWIGLU_LIMIT, SWIGLU_LIMIT)
        act = (up + 1.0) * (gate * (1.0 / (1.0 + jnp.exp(-SWIGLU_ALPHA * gate))))
        out = _dot(act.astype(BF16), wd_bf[...]) + bd_ref[0]
        o_ref[rows, :] = out.astype(BF16)

    n_sub = GEMM_TILE // GEMM_SUB

    @pl.when(valid_ref[i] > GEMM_TILE - GEMM_SUB)
    def _():
        expert_mlp(pl.ds(0, GEMM_TILE))

    @pl.when(valid_ref[i] <= GEMM_TILE - GEMM_SUB)
    def _():
        o_ref[pl.ds(0, GEMM_SUB), :] = jnp.zeros((GEMM_SUB, D_MODEL), BF16)
        for sub in range(1, n_sub):
            rows = pl.ds(sub * GEMM_SUB, GEMM_SUB)

            @pl.when(valid_ref[i] > GEMM_TILE - (sub + 1) * GEMM_SUB)
            def _():
                expert_mlp(rows)

            @pl.when(valid_ref[i] <= GEMM_TILE - (sub + 1) * GEMM_SUB)
            def _():
                o_ref[rows, :] = jnp.zeros((GEMM_SUB, D_MODEL), BF16)


def _experts(xs, tiles_info, layer, w_gate_up, b_gate_up, w_down, b_down):
    n_rows, d = xs.shape
    n_tiles = n_rows // GEMM_TILE
    n_stacked = w_gate_up.shape[0] * N_EXPERTS
    row_map = lambda i, te, tv, u, *_: (jnp.minimum(i, jnp.maximum(u[0] - 1, 0)), 0)
    exp_map = lambda i, te, *_: (layer * N_EXPERTS + te[i], 0, 0)
    return pl.pallas_call(
        functools.partial(_expert_kernel, layer),
        grid_spec=pltpu.PrefetchScalarGridSpec(
            num_scalar_prefetch=5,
            grid=(n_tiles,),
            in_specs=[
                pl.BlockSpec((GEMM_TILE, d), row_map),
                pl.BlockSpec(memory_space=pl.ANY),
                pl.BlockSpec((1, 1, 2 * D_FF), exp_map),
                pl.BlockSpec(memory_space=pl.ANY),
                pl.BlockSpec((1, 1, d), exp_map),
            ],
            out_specs=pl.BlockSpec((GEMM_TILE, d), lambda i, *_: (i, 0)),
            scratch_shapes=[pltpu.VMEM((2, d, 2 * D_FF), F32), pltpu.VMEM((2, D_FF, d), F32),
                            pltpu.VMEM((d, 2 * D_FF), BF16), pltpu.VMEM((D_FF, d), BF16),
                            pltpu.SemaphoreType.DMA((2,))],
        ),
        out_shape=jax.ShapeDtypeStruct((n_rows, d), BF16),
        compiler_params=pltpu.CompilerParams(dimension_semantics=("arbitrary",), vmem_limit_bytes=VMEM_LIMIT),
        name="moe_experts",
    )(*tiles_info, xs, w_gate_up.reshape(n_stacked, d, 2 * D_FF),
      b_gate_up.reshape(n_stacked, 1, 2 * D_FF), w_down.reshape(n_stacked, D_FF, d), b_down.reshape(n_stacked, 1, d))


def _combine_kernel(loc_ref, n_ref, dst_ref, tot_ref, x_ref, slot_t_ref, gate_t_ref, ys_hbm, o_ref, buf, sems):
    c = pl.program_id(0)
    nc = pl.num_programs(0)
    par = c % 2
    start = lambda cp: cp.start()

    @pl.when(c == 0)
    def _():
        buf[...] = jnp.zeros(buf.shape, BF16)
        _for_each_segment(loc_ref, n_ref, dst_ref, c, buf.at[0], ys_hbm, sems.at[0], False, start)

    @pl.when(c + 1 < nc)
    def _():
        _for_each_segment(loc_ref, n_ref, dst_ref, c + 1, buf.at[1 - par], ys_hbm, sems.at[1 - par], False, start)

    _wait_chunk(tot_ref, c, buf.at[par], ys_hbm, sems.at[par], False)

    cols = lax.broadcasted_iota(jnp.int32, (CHUNK, CHUNK_ROWS), 1)
    st = slot_t_ref[0]
    gt = gate_t_ref[0]
    back = jnp.zeros((CHUNK, CHUNK_ROWS), F32)
    for k in range(TOP_K):
        back = jnp.where(cols == st[:, k:k + 1], gt[:, k:k + 1], back)
    back = back.astype(BF16)
    o_ref[...] = x_ref[...] + _dot(back, buf[par])


def _combine(x2d, ys, slots_t, gates_t, seg_loc, seg_n, seg_dst, seg_tot):
    t, d = x2d.shape
    nc = t // CHUNK
    return pl.pallas_call(
        _combine_kernel,
        grid_spec=pltpu.PrefetchScalarGridSpec(
            num_scalar_prefetch=4,
            grid=(nc,),
            in_specs=[
                pl.BlockSpec((CHUNK, d), lambda c, *_: (c, 0)),
                pl.BlockSpec((1, CHUNK, TOP_K), lambda c, *_: (c, 0, 0)),
                pl.BlockSpec((1, CHUNK, TOP_K), lambda c, *_: (c, 0, 0)),
                pl.BlockSpec(memory_space=pl.ANY),
            ],
            out_specs=pl.BlockSpec((CHUNK, d), lambda c, *_: (c, 0)),
            scratch_shapes=[pltpu.VMEM((2, CHUNK_ROWS, d), BF16), pltpu.SemaphoreType.DMA((2,))],
        ),
        out_shape=jax.ShapeDtypeStruct((t, d), F32),
        compiler_params=pltpu.CompilerParams(dimension_semantics=("arbitrary",), vmem_limit_bytes=VMEM_LIMIT),
        name="moe_combine",
    )(seg_loc, seg_n, seg_dst, seg_tot, x2d, slots_t, gates_t, ys)


def _moe_layer(x2d, routed, layer, w_gate_up, b_gate_up, w_down, b_down):
    t, d = x2d.shape
    nc = t // CHUNK
    n_tiles_max = (t * TOP_K + nc * N_EXPERTS * (SEG_ALIGN - 1)) // GEMM_TILE + N_EXPERTS
    hb, slots, gates, cnt = routed
    seg_loc, seg_n, seg_dst, seg_tot, tiles_info, gap = _segment_tables(cnt, n_tiles_max)
    xs = _dispatch(hb, slots, seg_loc, seg_n, seg_dst, seg_tot, gap, n_tiles_max * GEMM_TILE)
    ys = _experts(xs, tiles_info, layer, w_gate_up, b_gate_up, w_down, b_down)
    slots_t = jnp.swapaxes(slots[:, :TOP_K, :], 1, 2)
    gates_t = jnp.swapaxes(gates[:, :TOP_K, :], 1, 2)
    return _combine(x2d, ys, slots_t, gates_t, seg_loc, seg_n, seg_dst, seg_tot)


def _qkv_kernel(x_ref, g_ref, w_ref, qg_ref, kg_ref, q_ref, k_ref, v_ref):
    d = x_ref.shape[1]
    h = _rmsnorm(x_ref[...], g_ref[...]).astype(BF16)
    qkv = _dot(h, w_ref[...])
    width = 2 * LANES
    li = lax.broadcasted_iota(jnp.int32, (width, width), 0) // HEAD_DIM
    lj = lax.broadcasted_iota(jnp.int32, (width, width), 1) // HEAD_DIM
    same_head = jnp.where(li == lj, 1.0, 0.0).astype(BF16)

    def head_norm(v, gain, out_scale):
        parts = []
        for j in range(d // width):
            vj = v[:, j * width:(j + 1) * width]
            ss = _dot((vj * vj).astype(BF16), same_head)
            parts.append(vj * lax.rsqrt(ss * (1.0 / HEAD_DIM) + RMS_EPS))
        return (jnp.concatenate(parts, axis=-1) * gain * out_scale).astype(BF16)

    q_ref[...] = head_norm(qkv[:, :d], qg_ref[...], HEAD_DIM ** -0.5 * LOG2_E)
    k_ref[...] = head_norm(qkv[:, d:2 * d], kg_ref[...], 1.0)
    v_ref[...] = qkv[:, 2 * d:].astype(BF16)


def _qkv(x2d, g, w_qkv, q_gain, k_gain):
    t, d = x2d.shape
    tt = min(TOK_TILE, t)
    row = lambda i: (i, 0)
    fixed = lambda i: (0, 0)
    return pl.pallas_call(
        _qkv_kernel,
        grid=(t // tt,),
        in_specs=[
            pl.BlockSpec((tt, d), row),
            pl.BlockSpec((1, d), fixed),
            pl.BlockSpec((d, 3 * d), fixed),
            pl.BlockSpec((1, d), fixed),
            pl.BlockSpec((1, d), fixed),
        ],
        out_specs=[pl.BlockSpec((tt, d), row)] * 3,
        out_shape=[jax.ShapeDtypeStruct((t, d), BF16)] * 3,
        compiler_params=pltpu.CompilerParams(dimension_semantics=("arbitrary",), vmem_limit_bytes=VMEM_LIMIT),
        name="attn_qkv",
    )(x2d, g.reshape(1, d), w_qkv.astype(BF16), jnp.tile(q_gain, N_HEADS).reshape(1, d),
      jnp.tile(k_gain, N_HEADS).reshape(1, d))


def _attn_kernel(q_ref, k_ref, v_ref, o_ref, acc_ref, carry_ref):
    qi = pl.program_id(2)
    tq, kb = ATT_Q, ATT_K
    q = q_ref[0]
    head_of_lane = lax.broadcasted_iota(jnp.int32, (tq, LANES), 1) // HEAD_DIM
    q_heads = [jnp.where(head_of_lane == hd, q, jnp.zeros_like(q)) for hd in range(2)]
    r2 = lax.broadcasted_iota(jnp.int32, (2 * kb, 2 * kb), 0)
    c2 = lax.broadcasted_iota(jnp.int32, (2 * kb, 2 * kb), 1)
    suffix2 = jnp.where(jnp.logical_and(r2 // kb == c2 // kb, r2 >= c2), 1.0, 0.0).astype(BF16)

    acc_ref[...] = jnp.zeros(acc_ref.shape, F32)
    carry_ref[...] = jnp.zeros(carry_ref.shape, F32)

    def key_group(j0, diag_offsets, r0):
        nb = len(diag_offsets)
        rows = tq - r0
        start = pl.multiple_of(j0 * kb, kb)
        kk = k_ref[0, pl.ds(start, nb * kb), :]
        vv = v_ref[0, pl.ds(start, nb * kb), :]
        row = lax.broadcasted_iota(jnp.int32, (rows, kb), 0) + r0
        key = lax.broadcasted_iota(jnp.int32, (rows, kb), 1)
        z_all = [_dot_nt(q_heads[hd][r0:, :], kk) for hd in range(2)]
        carry = [carry_ref[hd, r0:, :] for hd in range(2)]
        probs = [[None] * nb for _ in range(2)]
        for b in reversed(range(nb)):
            causal = None if diag_offsets[b] is None else key + diag_offsets[b] < row
            zs, log_keeps = [], []
            for hd in range(2):
                z = z_all[hd][:, b * kb:(b + 1) * kb]
                neg = -z
                log_keep = jnp.minimum(neg, 0.0) - jnp.log(1.0 + jnp.exp2(jnp.minimum(z, neg))) * LOG2_E
                if causal is not None:
                    log_keep = jnp.where(causal, log_keep, 0.0)
                zs.append(z)
                log_keeps.append(log_keep.astype(BF16))
            sums = _dot(jnp.concatenate(log_keeps, axis=1), suffix2)
            for hd in range(2):
                s_in = sums[:, hd * kb:(hd + 1) * kb]
                a = jnp.exp2(zs[hd] + s_in + carry[hd])
                if causal is not None:
                    a = jnp.where(causal, a, 0.0)
                probs[hd][b] = a.astype(BF16)
                carry[hd] = carry[hd] + jnp.broadcast_to(s_in[:, 0:1], (rows, kb))
        for hd in range(2):
            carry_ref[hd, r0:, :] = carry[hd]
            acc_ref[hd, r0:, :] += _dot(jnp.concatenate(probs[hd], axis=1), vv)

    blocks_per_q = tq // kb
    for g in reversed(range(blocks_per_q // ATT_DIAG_GROUP)):
        offs = [(g * ATT_DIAG_GROUP + b) * kb for b in range(ATT_DIAG_GROUP)]
        key_group(qi * blocks_per_q + g * ATT_DIAG_GROUP, offs, offs[0])

    def body(s, c):
        key_group(qi * blocks_per_q - (s + 1) * ATT_GROUP, [None] * ATT_GROUP, 0)
        return c
    lax.fori_loop(0, qi * (blocks_per_q // ATT_GROUP), body, 0)
    o_ref[0] = jnp.where(head_of_lane == 0, acc_ref[0], acc_ref[1]).astype(BF16)


def _attention(q, k, v, b, s):
    d = q.shape[-1]
    q3, k3, v3 = (a.reshape(b, s, d) for a in (q, k, v))
    pairs = d // LANES
    o = pl.pallas_call(
        _attn_kernel,
        grid=(b, pairs, s // ATT_Q),
        in_specs=[
            pl.BlockSpec((1, ATT_Q, LANES), lambda bi, hp, i: (bi, i, hp)),
            pl.BlockSpec((1, s, LANES), lambda bi, hp, i: (bi, 0, hp)),
            pl.BlockSpec((1, s, LANES), lambda bi, hp, i: (bi, 0, hp)),
        ],
        out_specs=pl.BlockSpec((1, ATT_Q, LANES), lambda bi, hp, i: (bi, i, hp)),
        out_shape=jax.ShapeDtypeStruct((b, s, d), BF16),
        scratch_shapes=[pltpu.VMEM((2, ATT_Q, LANES), F32), pltpu.VMEM((2, ATT_Q, LANES), F32)],
        compiler_params=pltpu.CompilerParams(
            dimension_semantics=("arbitrary", "arbitrary", "arbitrary"), vmem_limit_bytes=VMEM_LIMIT),
        name="attn_core",
    )(q3, k3, v3)
    return o.reshape(b * s, d)


def _proj_route_kernel(x_ref, o_ref, w_ref, rg_ref, wt_hi_ref, wt_lo_ref, rb_ref,
                       y_ref, hb_ref, slot_ref, gate_ref, cnt_ref):
    x2 = x_ref[...] + _dot(o_ref[...], w_ref[...])
    y_ref[...] = x2
    _route_rows(x2, rg_ref, wt_hi_ref, wt_lo_ref, rb_ref, hb_ref, slot_ref, gate_ref, cnt_ref)


def _proj_route(x2d, o, w_o, ffn_g, router_w, router_b):
    t, d = x2d.shape
    route_in, route_out = _route_specs(d, lambda c: c)
    outs = pl.pallas_call(
        _proj_route_kernel,
        grid=(t // TOK_TILE,),
        in_specs=[
            pl.BlockSpec((TOK_TILE, d), lambda c: (c, 0)),
            pl.BlockSpec((TOK_TILE, d), lambda c: (c, 0)),
            pl.BlockSpec((d, d), lambda c: (0, 0)),
        ] + route_in,
        out_specs=[pl.BlockSpec((TOK_TILE, d), lambda c: (c, 0))] + route_out,
        out_shape=[jax.ShapeDtypeStruct((t, d), F32)] + _route_shapes(t, d),
        compiler_params=pltpu.CompilerParams(dimension_semantics=("arbitrary",), vmem_limit_bytes=VMEM_LIMIT),
        name="attn_proj_route",
    )(x2d, o, w_o.astype(BF16), *_route_args(ffn_g, router_w, router_b))
    return outs[0], outs[1:]


def _attention_route_layer(x2d, b, s, g, w_qkv, q_gain, k_gain, w_o, ffn_g, router_w, router_b):
    q, k, v = _qkv(x2d, g, w_qkv, q_gain, k_gain)
    o = _attention(q, k, v, b, s)
    return _proj_route(x2d, o, w_o, ffn_g, router_w, router_b)


def kernel(x, mix_norm, pool_w, pool_scale, w_qkv, q_norm, k_norm, w_o, ffn_norm, router_w, router_b,
           w_gate_up, b_gate_up, w_down, b_down):
    b, s, d = x.shape
    assert d == D_MODEL and s % ATT_Q == 0 and s % TOK_TILE == 0 and TOK_TILE % CHUNK == 0
    depth = mix_norm.shape[0]
    for i in range(depth):
        j = i // 2
        if i % 2 == 0:
            x2d, routed = _pool_route_layer(x, mix_norm[i], pool_w[j], pool_scale[j],
                                            ffn_norm[i], router_w[i], router_b[i])
        else:
            x2d, routed = _attention_route_layer(x.reshape(b * s, d), b, s, mix_norm[i], w_qkv[j], q_norm[j],
                                                 k_norm[j], w_o[j], ffn_norm[i], router_w[i], router_b[i])
        x2d = _moe_layer(x2d, routed, i, w_gate_up, b_gate_up, w_down, b_down)
        x = x2d.reshape(b, s, d)
    return x
```

```python
import functools

import jax
import jax.numpy as jnp
from jax import lax
from jax.experimental import pallas as pl
from jax.experimental.pallas import tpu as pltpu

D_MODEL = 1024
POOL_WINDOWS = (2, 4, 8, 16)
POOL_GROUP_DIM = D_MODEL // len(POOL_WINDOWS)
POOL_HALO = 16
HEAD_DIM = 64
N_HEADS = D_MODEL // HEAD_DIM
N_EXPERTS = 32
TOP_K = 4
D_FF = D_MODEL
SWIGLU_LIMIT = 7.0
SWIGLU_ALPHA = 1.702
RMS_EPS = 1e-6
LOG2_E = 1.4426950408889634

LANES = 128
SUBLANES = 8
VMEM_LIMIT = 56 * 1024 * 1024

CHUNK = 512
SEG_ALIGN = 2 * SUBLANES
CHUNK_ROWS = CHUNK * TOP_K + N_EXPERTS * SEG_ALIGN
GEMM_TILE = 512
GEMM_SUB = 256
TOK_TILE = 512
ATT_Q = 2048
ATT_K = LANES
ATT_GROUP = 4
ATT_DIAG_GROUP = 2

F32 = jnp.float32
BF16 = jnp.bfloat16


def _dot(a, b):
    return jnp.dot(a, b, preferred_element_type=F32)


def _dot_nt(a, b):
    return lax.dot_general(a, b, (((1,), (1,)), ((), ())), preferred_element_type=F32)


def _split_bf16(v):
    hi = v.astype(BF16)
    lo = (v - hi.astype(F32)).astype(BF16)
    return hi, lo


def _rmsnorm(v, g):
    return v * lax.rsqrt(jnp.mean(v * v, axis=-1, keepdims=True) + RMS_EPS) * g


def _pool_route_kernel(x_ref, halo_ref, g_ref, w_ref, scale_ref, rg_ref, wt_hi_ref, wt_lo_ref, rb_ref,
                       o_ref, hb_ref, slot_ref, gate_ref, cnt_ref):
    x1 = _pool_tile(x_ref[0], halo_ref[0], pl.program_id(1), g_ref, w_ref, scale_ref)
    o_ref[0] = x1
    _route_rows(x1, rg_ref, wt_hi_ref, wt_lo_ref, rb_ref, hb_ref, slot_ref, gate_ref, cnt_ref)


def _pool_tile(x, halo, i, g_ref, w_ref, scale_ref):
    g = g_ref[...]
    h = _rmsnorm(x, g)
    hh = _rmsnorm(halo, g)
    hh = jnp.where(i > 0, hh, 0.0)
    a = jnp.concatenate([hh, h], axis=0)
    ts = x.shape[0]
    pos = i * ts + lax.broadcasted_iota(jnp.int32, (ts, 1), 0)
    outs = []
    for gi, w in enumerate(POOL_WINDOWS):
        lo, hi = gi * POOL_GROUP_DIM, (gi + 1) * POOL_GROUP_DIM
        s = a[:, lo:hi]
        span = 1
        while span < w:
            s = s + pltpu.roll(s, span, axis=0)
            span *= 2
        cnt = jnp.minimum(pos + 1, w).astype(F32)
        pooled = s[POOL_HALO:, :] / cnt - h[:, lo:hi]
        outs.append(_dot(pooled.astype(BF16), w_ref[gi]))
    return x + jnp.concatenate(outs, axis=-1) * scale_ref[...]


def _route_specs(d, step_of):
    fixed2 = lambda *_: (0, 0)
    per_step = TOK_TILE // CHUNK
    in_specs = [
        pl.BlockSpec((1, d), fixed2),
        pl.BlockSpec((N_EXPERTS, d), fixed2),
        pl.BlockSpec((N_EXPERTS, d), fixed2),
        pl.BlockSpec((N_EXPERTS, 1), fixed2),
    ]
    out_specs = [
        pl.BlockSpec((TOK_TILE, d), lambda *ids: (step_of(*ids), 0)),
        pl.BlockSpec((per_step, SUBLANES, CHUNK), lambda *ids: (step_of(*ids), 0, 0)),
        pl.BlockSpec((per_step, SUBLANES, CHUNK), lambda *ids: (step_of(*ids), 0, 0)),
        pl.BlockSpec((per_step, N_EXPERTS, LANES), lambda *ids: (step_of(*ids), 0, 0)),
    ]
    return in_specs, out_specs


def _route_args(g, router_w, router_b):
    wt_hi, wt_lo = _split_bf16(router_w.T)
    return g.reshape(1, -1), wt_hi, wt_lo, router_b.reshape(N_EXPERTS, 1)


def _route_shapes(t, d):
    nc = t // CHUNK
    return [
        jax.ShapeDtypeStruct((t, d), BF16),
        jax.ShapeDtypeStruct((nc, SUBLANES, CHUNK), jnp.int32),
        jax.ShapeDtypeStruct((nc, SUBLANES, CHUNK), F32),
        jax.ShapeDtypeStruct((nc, N_EXPERTS, LANES), jnp.int32),
    ]


def _pool_route_layer(x, g, w, scale, ffn_g, router_w, router_b):
    b, s, d = x.shape
    ts = TOK_TILE
    tiles_per_seq = s // ts
    blocks_per_tile = ts // POOL_HALO
    route_in, route_out = _route_specs(d, lambda bi, i: bi * tiles_per_seq + i)
    outs = pl.pallas_call(
        _pool_route_kernel,
        grid=(b, tiles_per_seq),
        in_specs=[
            pl.BlockSpec((1, ts, d), lambda bi, i: (bi, i, 0)),
            pl.BlockSpec((1, POOL_HALO, d), lambda bi, i: (bi, jnp.maximum(i * blocks_per_tile - 1, 0), 0)),
            pl.BlockSpec((1, d), lambda bi, i: (0, 0)),
            pl.BlockSpec((len(POOL_WINDOWS), POOL_GROUP_DIM, POOL_GROUP_DIM), lambda bi, i: (0, 0, 0)),
            pl.BlockSpec((1, d), lambda bi, i: (0, 0)),
        ] + route_in,
        out_specs=[pl.BlockSpec((1, ts, d), lambda bi, i: (bi, i, 0))] + route_out,
        out_shape=[jax.ShapeDtypeStruct(x.shape, F32)] + _route_shapes(b * s, d),
        compiler_params=pltpu.CompilerParams(
            dimension_semantics=("arbitrary", "arbitrary"), vmem_limit_bytes=VMEM_LIMIT),
        name="pool_route",
    )(x, x, g.reshape(1, d), w.astype(BF16), scale.reshape(1, d), *_route_args(ffn_g, router_w, router_b))
    return outs[0].reshape(b * s, d), outs[1:]


def _route_rows(x, g_ref, wt_hi_ref, wt_lo_ref, b_ref, hb_ref, slot_ref, gate_ref, cnt_ref):
    t = x.shape[0]
    h = _rmsnorm(x, g_ref[...])
    h_hi, h_lo = _split_bf16(h)
    hb_ref[...] = h_hi
    logits = (_dot_nt(wt_hi_ref[...], h_hi) + _dot_nt(wt_lo_ref[...], h_hi)
              + _dot_nt(wt_hi_ref[...], h_lo) + b_ref[...])
    eio = lax.broadcasted_iota(jnp.int32, (N_EXPERTS, t), 0).astype(F32)
    vals, idxs = [], []
    l = logits
    for _ in range(TOP_K):
        m = jnp.max(l, axis=0, keepdims=True)
        idx = jnp.min(jnp.where(l == m, eio, float(N_EXPERTS)), axis=0, keepdims=True)
        vals.append(m)
        idxs.append(idx)
        l = jnp.where(eio == idx, -jnp.inf, l)
    es = [jnp.exp(v - vals[0]) for v in vals]
    denom = es[0] + es[1] + es[2] + es[3]
    sel = jnp.zeros((N_EXPERTS, t), F32)
    for idx in idxs:
        sel = sel + jnp.where(eio == idx, 1.0, 0.0)
    ti = lax.broadcasted_iota(jnp.int32, (t, t), 0)
    tj = lax.broadcasted_iota(jnp.int32, (t, t), 1)
    upper = jnp.where(jnp.logical_and(ti <= tj, ti // CHUNK == tj // CHUNK), 1.0, 0.0).astype(BF16)
    rank = _dot(sel.astype(BF16), upper)
    chunk_of_lane = lax.broadcasted_iota(jnp.int32, (N_EXPERTS, t), 1) // CHUNK
    n = jnp.zeros((N_EXPERTS, t), F32)
    for ci in range(t // CHUNK):
        last = (ci + 1) * CHUNK - 1
        n = jnp.where(chunk_of_lane == ci, rank[:, last:last + 1], n)
    n_units = jnp.floor((n + (SEG_ALIGN - 1)) * (1.0 / SEG_ALIGN))
    strict_lower = jnp.where(lax.broadcasted_iota(jnp.int32, (N_EXPERTS, N_EXPERTS), 1)
                             < lax.broadcasted_iota(jnp.int32, (N_EXPERTS, N_EXPERTS), 0), 1.0, 0.0).astype(BF16)
    off = _dot(strict_lower, n_units.astype(BF16)) * SEG_ALIGN
    dest = off + rank - 1.0
    slots = [jnp.sum(jnp.where(eio == idxs[k], dest, 0.0), axis=0, keepdims=True).astype(jnp.int32)
             for k in range(TOP_K)]
    for ci in range(t // CHUNK):
        lanes = slice(ci * CHUNK, (ci + 1) * CHUNK)
        for k in range(TOP_K):
            slot_ref[ci, k:k + 1, :] = slots[k][:, lanes]
            gate_ref[ci, k:k + 1, :] = (es[k] / denom)[:, lanes]
        slot_ref[ci, TOP_K:, :] = jnp.zeros((SUBLANES - TOP_K, CHUNK), jnp.int32)
        gate_ref[ci, TOP_K:, :] = jnp.zeros((SUBLANES - TOP_K, CHUNK), F32)
        cnt_ref[ci] = (n_units[:, ci * CHUNK:ci * CHUNK + LANES] * SEG_ALIGN).astype(jnp.int32)


def _segment_tables(cnt, n_tiles_max):
    seg_n = cnt[:, :, 0]
    seg_loc = jnp.cumsum(seg_n, axis=1) - seg_n
    tot = jnp.sum(seg_n, axis=0)
    tiles = (tot + GEMM_TILE - 1) // GEMM_TILE
    tile_end = jnp.cumsum(tiles)
    tile_start = tile_end - tiles
    front_pad = tiles * GEMM_TILE - tot
    seg_dst = (tile_start * GEMM_TILE + front_pad)[None, :] + jnp.cumsum(seg_n, axis=0) - seg_n
    used = tile_end[-1]
    tile_ids = jnp.arange(n_tiles_max, dtype=jnp.int32)
    tile_expert = jnp.sum(tile_ids[:, None] >= tile_end[None, :], axis=1).astype(jnp.int32)
    last_expert = jnp.sum(jnp.maximum(used - 1, 0) >= tile_end).astype(jnp.int32)
    tile_expert = jnp.where(tile_ids < used, tile_expert, last_expert)
    owner = tile_expert[:, None] == jnp.arange(N_EXPERTS, dtype=jnp.int32)[None, :]
    pad_here = jnp.sum(jnp.where(owner, jnp.where(tile_ids[:, None] == tile_start[None, :], front_pad[None, :], 0), 0),
                       axis=1)
    tile_valid = jnp.where(tile_ids < used, GEMM_TILE - pad_here, 0)
    experts = jnp.arange(N_EXPERTS, dtype=jnp.int32)
    ordinal = jnp.cumsum((tiles > 0).astype(jnp.int32)) - 1
    later = jnp.where(jnp.logical_and(experts[None, :] > experts[:, None], (tiles > 0)[None, :]), experts[None, :],
                      N_EXPERTS)
    next_expert = jnp.min(later, axis=1)
    next_expert = jnp.where(next_expert < N_EXPERTS, next_expert, -1)
    tile_buf = jnp.sum(jnp.where(owner, (ordinal % 2)[None, :], 0), axis=1)
    tile_next = jnp.sum(jnp.where(owner, next_expert[None, :], 0), axis=1)
    i32 = lambda v: v.astype(jnp.int32).reshape(-1)
    gap = jnp.concatenate([tile_start * GEMM_TILE, front_pad, used[None]])
    seg_tot = jnp.sum(seg_n, axis=1)
    tiles_info = (tile_expert, i32(tile_valid), i32(used), i32(tile_buf), i32(tile_next))
    return i32(seg_loc), i32(seg_n), i32(seg_dst), i32(seg_tot), tiles_info, i32(gap)


def _segment_copy(loc_ref, n_ref, dst_ref, c, e, local_buf, sorted_hbm, sem, to_sorted):
    j = c * N_EXPERTS + e
    n = pl.multiple_of(n_ref[j], SEG_ALIGN)
    local = local_buf.at[pl.ds(pl.multiple_of(loc_ref[j], SEG_ALIGN), n)]
    remote = sorted_hbm.at[pl.ds(pl.multiple_of(dst_ref[j], SEG_ALIGN), n)]
    return n, (pltpu.make_async_copy(local, remote, sem) if to_sorted
               else pltpu.make_async_copy(remote, local, sem))


def _for_each_segment(loc_ref, n_ref, dst_ref, c, local_buf, sorted_hbm, sem, to_sorted, action):
    def body(e, carry):
        n, cp = _segment_copy(loc_ref, n_ref, dst_ref, c, e, local_buf, sorted_hbm, sem, to_sorted)

        @pl.when(n > 0)
        def _():
            action(cp)
        return carry
    lax.fori_loop(0, N_EXPERTS, body, 0)


def _wait_chunk(tot_ref, c, local_buf, sorted_hbm, sem, to_sorted):
    n = pl.multiple_of(tot_ref[c], SEG_ALIGN)
    local = local_buf.at[pl.ds(0, n)]
    remote = sorted_hbm.at[pl.ds(0, n)]
    (pltpu.make_async_copy(local, remote, sem) if to_sorted else pltpu.make_async_copy(remote, local, sem)).wait()


def _zero_unused_rows(gap_ref, xs_hbm, zeros, sem, action):
    def expert_gap(e, carry):
        n = pl.multiple_of(gap_ref[N_EXPERTS + e], SEG_ALIGN)

        @pl.when(n > 0)
        def _():
            action(pltpu.make_async_copy(
                zeros.at[pl.ds(0, n)], xs_hbm.at[pl.ds(pl.multiple_of(gap_ref[e], SEG_ALIGN), n)], sem))
        return carry
    lax.fori_loop(0, N_EXPERTS, expert_gap, 0)

    def unused_tile(i, carry):
        action(pltpu.make_async_copy(zeros, xs_hbm.at[pl.ds(pl.multiple_of(i * GEMM_TILE, GEMM_TILE), GEMM_TILE)], sem))
        return carry
    lax.fori_loop(gap_ref[2 * N_EXPERTS], xs_hbm.shape[0] // GEMM_TILE, unused_tile, 0)


def _dispatch_kernel(loc_ref, n_ref, dst_ref, tot_ref, gap_ref, hb_ref, slot_ref, xs_hbm, buf, zeros, sems):
    c = pl.program_id(0)
    nc = pl.num_programs(0)
    par = c % 2
    start = lambda cp: cp.start()
    wait = lambda cp: cp.wait()

    @pl.when(c == 0)
    def _():
        zeros[...] = jnp.zeros(zeros.shape, BF16)
        _zero_unused_rows(gap_ref, xs_hbm, zeros, sems.at[2], start)

    @pl.when(c >= 2)
    def _():
        _wait_chunk(tot_ref, c - 2, buf.at[par], xs_hbm, sems.at[par], True)

    rows = lax.broadcasted_iota(jnp.int32, (CHUNK_ROWS, CHUNK), 0)
    hit = rows == slot_ref[0, 0:1, :]
    for k in range(1, TOP_K):
        hit = jnp.logical_or(hit, rows == slot_ref[0, k:k + 1, :])
    onehot = jnp.where(hit, 1.0, 0.0).astype(BF16)
    buf[par] = _dot(onehot, hb_ref[...]).astype(BF16)
    _for_each_segment(loc_ref, n_ref, dst_ref, c, buf.at[par], xs_hbm, sems.at[par], True, start)

    @pl.when(c == nc - 1)
    def _():
        _zero_unused_rows(gap_ref, xs_hbm, zeros, sems.at[2], wait)

        @pl.when(c >= 1)
        def _():
            _wait_chunk(tot_ref, c - 1, buf.at[1 - par], xs_hbm, sems.at[1 - par], True)
        _wait_chunk(tot_ref, c, buf.at[par], xs_hbm, sems.at[par], True)


def _dispatch(hb, slots, seg_loc, seg_n, seg_dst, seg_tot, gap, n_rows):
    t, d = hb.shape
    nc = t // CHUNK
    return pl.pallas_call(
        _dispatch_kernel,
        grid_spec=pltpu.PrefetchScalarGridSpec(
            num_scalar_prefetch=5,
            grid=(nc,),
            in_specs=[
                pl.BlockSpec((CHUNK, d), lambda c, *_: (c, 0)),
                pl.BlockSpec((1, SUBLANES, CHUNK), lambda c, *_: (c, 0, 0)),
            ],
            out_specs=pl.BlockSpec(memory_space=pl.ANY),
            scratch_shapes=[pltpu.VMEM((2, CHUNK_ROWS, d), BF16), pltpu.VMEM((GEMM_TILE, d), BF16),
                            pltpu.SemaphoreType.DMA((3,))],
        ),
        out_shape=jax.ShapeDtypeStruct((n_rows, d), BF16),
        compiler_params=pltpu.CompilerParams(dimension_semantics=("arbitrary",), vmem_limit_bytes=VMEM_LIMIT),
        name="moe_dispatch",
    )(seg_loc, seg_n, seg_dst, seg_tot, gap, hb, slots)


def _expert_kernel(layer, te_ref, valid_ref, used_ref, buf_ref, next_ref, x_ref, wgu_hbm, bgu_ref, wd_hbm, bd_ref,
                   o_ref, wgu_f32, wd_f32, wgu_bf, wd_bf, sems):
    i = pl.program_id(0)
    prev = te_ref[jnp.maximum(i - 1, 0)]
    buf = buf_ref[i]

    def weight_copies(expert, b):
        e = layer * N_EXPERTS + expert
        return (pltpu.make_async_copy(wgu_hbm.at[e], wgu_f32.at[b], sems.at[b]),
                pltpu.make_async_copy(wd_hbm.at[e], wd_f32.at[b], sems.at[b]))

    @pl.when(i == 0)
    def _():
        for cp in weight_copies(te_ref[0], buf):
            cp.start()

    @pl.when(jnp.logical_or(i == 0, te_ref[i] != prev))
    def _():
        for cp in weight_copies(te_ref[i], buf):
            cp.wait()

        @pl.when(next_ref[i] >= 0)
        def _():
            for cp in weight_copies(next_ref[i], 1 - buf):
                cp.start()
        wgu_bf[...] = wgu_f32[buf].astype(BF16)
        wd_bf[...] = wd_f32[buf].astype(BF16)

    def expert_mlp(rows):
        gu = _dot(x_ref[rows, :], wgu_bf[...]) + bgu_ref[0]
        gate = jnp.minimum(gu[:, :D_FF], SWIGLU_LIMIT)
        up = jnp.clip(gu[:, D_FF:], -SWIGLU_LIMIT, SWIGLU_LIMIT)
        act = (up + 1.0) * (gate * (1.0 / (1.0 + jnp.exp(-SWIGLU_ALPHA * gate))))
        out = _dot(act.astype(BF16), wd_bf[...]) + bd_ref[0]
        o_ref[rows, :] = out.astype(BF16)

    n_sub = GEMM_TILE // GEMM_SUB

    @pl.when(valid_ref[i] > GEMM_TILE - GEMM_SUB)
    def _():
        expert_mlp(pl.ds(0, GEMM_TILE))

    @pl.when(valid_ref[i] <= GEMM_TILE - GEMM_SUB)
    def _():
        o_ref[pl.ds(0, GEMM_SUB), :] = jnp.zeros((GEMM_SUB, D_MODEL), BF16)
        for sub in range(1, n_sub):
            rows = pl.ds(sub * GEMM_SUB, GEMM_SUB)

            @pl.when(valid_ref[i] > GEMM_TILE - (sub + 1) * GEMM_SUB)
            def _():
                expert_mlp(rows)

            @pl.when(valid_ref[i] <= GEMM_TILE - (sub + 1) * GEMM_SUB)
            def _():
                o_ref[rows, :] = jnp.zeros((GEMM_SUB, D_MODEL), BF16)


def _experts(xs, tiles_info, layer, w_gate_up, b_gate_up, w_down, b_down):
    n_rows, d = xs.shape
    n_tiles = n_rows // GEMM_TILE
    n_stacked = w_gate_up.shape[0] * N_EXPERTS
    row_map = lambda i, te, tv, u, *_: (jnp.minimum(i, jnp.maximum(u[0] - 1, 0)), 0)
    exp_map = lambda i, te, *_: (layer * N_EXPERTS + te[i], 0, 0)
    return pl.pallas_call(
        functools.partial(_expert_kernel, layer),
        grid_spec=pltpu.PrefetchScalarGridSpec(
            num_scalar_prefetch=5,
            grid=(n_tiles,),
            in_specs=[
                pl.BlockSpec((GEMM_TILE, d), row_map),
                pl.BlockSpec(memory_space=pl.ANY),
                pl.BlockSpec((1, 1, 2 * D_FF), exp_map),
                pl.BlockSpec(memory_space=pl.ANY),
                pl.BlockSpec((1, 1, d), exp_map),
            ],
            out_specs=pl.BlockSpec((GEMM_TILE, d), lambda i, *_: (i, 0)),
            scratch_shapes=[pltpu.VMEM((2, d, 2 * D_FF), F32), pltpu.VMEM((2, D_FF, d), F32),
                            pltpu.VMEM((d, 2 * D_FF), BF16), pltpu.VMEM((D_FF, d), BF16),
                            pltpu.SemaphoreType.DMA((2,))],
        ),
        out_shape=jax.ShapeDtypeStruct((n_rows, d), BF16),
        compiler_params=pltpu.CompilerParams(dimension_semantics=("arbitrary",), vmem_limit_bytes=VMEM_LIMIT),
        name="moe_experts",
    )(*tiles_info, xs, w_gate_up.reshape(n_stacked, d, 2 * D_FF),
      b_gate_up.reshape(n_stacked, 1, 2 * D_FF), w_down.reshape(n_stacked, D_FF, d), b_down.reshape(n_stacked, 1, d))


def _combine_kernel(loc_ref, n_ref, dst_ref, tot_ref, x_ref, slot_t_ref, gate_t_ref, ys_hbm, o_ref, buf, sems):
    c = pl.program_id(0)
    nc = pl.num_programs(0)
    par = c % 2
    start = lambda cp: cp.start()

    @pl.when(c == 0)
    def _():
        buf[...] = jnp.zeros(buf.shape, BF16)
        _for_each_segment(loc_ref, n_ref, dst_ref, c, buf.at[0], ys_hbm, sems.at[0], False, start)

    @pl.when(c + 1 < nc)
    def _():
        _for_each_segment(loc_ref, n_ref, dst_ref, c + 1, buf.at[1 - par], ys_hbm, sems.at[1 - par], False, start)

    _wait_chunk(tot_ref, c, buf.at[par], ys_hbm, sems.at[par], False)

    cols = lax.broadcasted_iota(jnp.int32, (CHUNK, CHUNK_ROWS), 1)
    st = slot_t_ref[0]
    gt = gate_t_ref[0]
    back = jnp.zeros((CHUNK, CHUNK_ROWS), F32)
    for k in range(TOP_K):
        back = jnp.where(cols == st[:, k:k + 1], gt[:, k:k + 1], back)
    back = back.astype(BF16)
    o_ref[...] = x_ref[...] + _dot(back, buf[par])


def _combine(x2d, ys, slots_t, gates_t, seg_loc, seg_n, seg_dst, seg_tot):
    t, d = x2d.shape
    nc = t // CHUNK
    return pl.pallas_call(
        _combine_kernel,
        grid_spec=pltpu.PrefetchScalarGridSpec(
            num_scalar_prefetch=4,
            grid=(nc,),
            in_specs=[
                pl.BlockSpec((CHUNK, d), lambda c, *_: (c, 0)),
                pl.BlockSpec((1, CHUNK, TOP_K), lambda c, *_: (c, 0, 0)),
                pl.BlockSpec((1, CHUNK, TOP_K), lambda c, *_: (c, 0, 0)),
                pl.BlockSpec(memory_space=pl.ANY),
            ],
            out_specs=pl.BlockSpec((CHUNK, d), lambda c, *_: (c, 0)),
            scratch_shapes=[pltpu.VMEM((2, CHUNK_ROWS, d), BF16), pltpu.SemaphoreType.DMA((2,))],
        ),
        out_shape=jax.ShapeDtypeStruct((t, d), F32),
        compiler_params=pltpu.CompilerParams(dimension_semantics=("arbitrary",), vmem_limit_bytes=VMEM_LIMIT),
        name="moe_combine",
    )(seg_loc, seg_n, seg_dst, seg_tot, x2d, slots_t, gates_t, ys)


def _moe_layer(x2d, routed, layer, w_gate_up, b_gate_up, w_down, b_down):
    t, d = x2d.shape
    nc = t // CHUNK
    n_tiles_max = (t * TOP_K + nc * N_EXPERTS * (SEG_ALIGN - 1)) // GEMM_TILE + N_EXPERTS
    hb, slots, gates, cnt = routed
    seg_loc, seg_n, seg_dst, seg_tot, tiles_info, gap = _segment_tables(cnt, n_tiles_max)
    xs = _dispatch(hb, slots, seg_loc, seg_n, seg_dst, seg_tot, gap, n_tiles_max * GEMM_TILE)
    ys = _experts(xs, tiles_info, layer, w_gate_up, b_gate_up, w_down, b_down)
    slots_t = jnp.swapaxes(slots[:, :TOP_K, :], 1, 2)
    gates_t = jnp.swapaxes(gates[:, :TOP_K, :], 1, 2)
    return _combine(x2d, ys, slots_t, gates_t, seg_loc, seg_n, seg_dst, seg_tot)


def _qkv_kernel(x_ref, g_ref, w_ref, qg_ref, kg_ref, q_ref, k_ref, v_ref):
    d = x_ref.shape[1]
    h = _rmsnorm(x_ref[...], g_ref[...]).astype(BF16)
    qkv = _dot(h, w_ref[...])
    width = 2 * LANES
    li = lax.broadcasted_iota(jnp.int32, (width, width), 0) // HEAD_DIM
    lj = lax.broadcasted_iota(jnp.int32, (width, width), 1) // HEAD_DIM
    same_head = jnp.where(li == lj, 1.0, 0.0).astype(BF16)

    def head_norm(v, gain, out_scale):
        parts = []
        for j in range(d // width):
            vj = v[:, j * width:(j + 1) * width]
            ss = _dot((vj * vj).astype(BF16), same_head)
            parts.append(vj * lax.rsqrt(ss * (1.0 / HEAD_DIM) + RMS_EPS))
        return (jnp.concatenate(parts, axis=-1) * gain * out_scale).astype(BF16)

    q_ref[...] = head_norm(qkv[:, :d], qg_ref[...], HEAD_DIM ** -0.5 * LOG2_E)
    k_ref[...] = head_norm(qkv[:, d:2 * d], kg_ref[...], 1.0)
    v_ref[...] = qkv[:, 2 * d:].astype(BF16)


def _qkv(x2d, g, w_qkv, q_gain, k_gain):
    t, d = x2d.shape
    tt = min(TOK_TILE, t)
    row = lambda i: (i, 0)
    fixed = lambda i: (0, 0)
    return pl.pallas_call(
        _qkv_kernel,
        grid=(t // tt,),
        in_specs=[
            pl.BlockSpec((tt, d), row),
            pl.BlockSpec((1, d), fixed),
            pl.BlockSpec((d, 3 * d), fixed),
            pl.BlockSpec((1, d), fixed),
            pl.BlockSpec((1, d), fixed),
        ],
        out_specs=[pl.BlockSpec((tt, d), row)] * 3,
        out_shape=[jax.ShapeDtypeStruct((t, d), BF16)] * 3,
        compiler_params=pltpu.CompilerParams(dimension_semantics=("arbitrary",), vmem_limit_bytes=VMEM_LIMIT),
        name="attn_qkv",
    )(x2d, g.reshape(1, d), w_qkv.astype(BF16), jnp.tile(q_gain, N_HEADS).reshape(1, d),
      jnp.tile(k_gain, N_HEADS).reshape(1, d))


def _attn_kernel(q_ref, k_ref, v_ref, o_ref, acc_ref, carry_ref):
    qi = pl.program_id(2)
    tq, kb = ATT_Q, ATT_K
    q = q_ref[0]
    head_of_lane = lax.broadcasted_iota(jnp.int32, (tq, LANES), 1) // HEAD_DIM
    q_heads = [jnp.where(head_of_lane == hd, q, jnp.zeros_like(q)) for hd in range(2)]
    r2 = lax.broadcasted_iota(jnp.int32, (2 * kb, 2 * kb), 0)
    c2 = lax.broadcasted_iota(jnp.int32, (2 * kb, 2 * kb), 1)
    suffix2 = jnp.where(jnp.logical_and(r2 // kb == c2 // kb, r2 >= c2), 1.0, 0.0).astype(BF16)

    acc_ref[...] = jnp.zeros(acc_ref.shape, F32)
    carry_ref[...] = jnp.zeros(carry_ref.shape, F32)

    def key_group(j0, diag_offsets, r0, r1):
        nb = len(diag_offsets)
        rows = r1 - r0
        start = pl.multiple_of(j0 * kb, kb)
        kk = k_ref[0, pl.ds(start, nb * kb), :]
        vv = v_ref[0, pl.ds(start, nb * kb), :]
        row = lax.broadcasted_iota(jnp.int32, (rows, kb), 0) + r0
        key = lax.broadcasted_iota(jnp.int32, (rows, kb), 1)
        z_all = [_dot_nt(q_heads[hd][r0:r1, :], kk) for hd in range(2)]
        carry = [carry_ref[hd, r0:r1, :] for hd in range(2)]
        probs = [[None] * nb for _ in range(2)]
        for b in reversed(range(nb)):
            causal = None if diag_offsets[b] is None else key + diag_offsets[b] < row
            zs, log_keeps = [], []
            for hd in range(2):
                z = z_all[hd][:, b * kb:(b + 1) * kb]
                neg = -z
                log_keep = jnp.minimum(neg, 0.0) - jnp.log(1.0 + jnp.exp2(jnp.minimum(z, neg))) * LOG2_E
                if causal is not None:
                    log_keep = jnp.where(causal, log_keep, 0.0)
                zs.append(z)
                log_keeps.append(log_keep.astype(BF16))
            sums = _dot(jnp.concatenate(log_keeps, axis=1), suffix2)
            for hd in range(2):
                s_in = sums[:, hd * kb:(hd + 1) * kb]
                a = jnp.exp2(zs[hd] + s_in + carry[hd])
                if causal is not None:
                    a = jnp.where(causal, a, 0.0)
                probs[hd][b] = a.astype(BF16)
                carry[hd] = carry[hd] + jnp.broadcast_to(s_in[:, 0:1], (rows, kb))
        for hd in range(2):
            carry_ref[hd, r0:r1, :] = carry[hd]
            acc_ref[hd, r0:r1, :] += _dot(jnp.concatenate(probs[hd], axis=1), vv)

    blocks_per_q = tq // kb
    for g in reversed(range(blocks_per_q // ATT_DIAG_GROUP)):
        offs = [(g * ATT_DIAG_GROUP + b) * kb for b in range(ATT_DIAG_GROUP)]
        key_group(qi * blocks_per_q + g * ATT_DIAG_GROUP, offs, offs[0], tq)

    def body(s, c):
        key_group(qi * blocks_per_q - (s + 1) * ATT_GROUP, [None] * ATT_GROUP, 0, tq)
        return c
    lax.fori_loop(0, qi * (blocks_per_q // ATT_GROUP), body, 0)
    o_ref[0] = jnp.where(head_of_lane == 0, acc_ref[0], acc_ref[1]).astype(BF16)


def _attention(q, k, v, b, s):
    d = q.shape[-1]
    q3, k3, v3 = (a.reshape(b, s, d) for a in (q, k, v))
    pairs = d // LANES
    o = pl.pallas_call(
        _attn_kernel,
        grid=(b, pairs, s // ATT_Q),
        in_specs=[
            pl.BlockSpec((1, ATT_Q, LANES), lambda bi, hp, i: (bi, i, hp)),
            pl.BlockSpec((1, s, LANES), lambda bi, hp, i: (bi, 0, hp)),
            pl.BlockSpec((1, s, LANES), lambda bi, hp, i: (bi, 0, hp)),
        ],
        out_specs=pl.BlockSpec((1, ATT_Q, LANES), lambda bi, hp, i: (bi, i, hp)),
        out_shape=jax.ShapeDtypeStruct((b, s, d), BF16),
        scratch_shapes=[pltpu.VMEM((2, ATT_Q, LANES), F32), pltpu.VMEM((2, ATT_Q, LANES), F32)],
        compiler_params=pltpu.CompilerParams(
            dimension_semantics=("arbitrary", "arbitrary", "arbitrary"), vmem_limit_bytes=VMEM_LIMIT),
        name="attn_core",
    )(q3, k3, v3)
    return o.reshape(b * s, d)


def _proj_route_kernel(x_ref, o_ref, w_ref, rg_ref, wt_hi_ref, wt_lo_ref, rb_ref,
                       y_ref, hb_ref, slot_ref, gate_ref, cnt_ref):
    x2 = x_ref[...] + _dot(o_ref[...], w_ref[...])
    y_ref[...] = x2
    _route_rows(x2, rg_ref, wt_hi_ref, wt_lo_ref, rb_ref, hb_ref, slot_ref, gate_ref, cnt_ref)


def _proj_route(x2d, o, w_o, ffn_g, router_w, router_b):
    t, d = x2d.shape
    route_in, route_out = _route_specs(d, lambda c: c)
    outs = pl.pallas_call(
        _proj_route_kernel,
        grid=(t // TOK_TILE,),
        in_specs=[
            pl.BlockSpec((TOK_TILE, d), lambda c: (c, 0)),
            pl.BlockSpec((TOK_TILE, d), lambda c: (c, 0)),
            pl.BlockSpec((d, d), lambda c: (0, 0)),
        ] + route_in,
        out_specs=[pl.BlockSpec((TOK_TILE, d), lambda c: (c, 0))] + route_out,
        out_shape=[jax.ShapeDtypeStruct((t, d), F32)] + _route_shapes(t, d),
        compiler_params=pltpu.CompilerParams(dimension_semantics=("arbitrary",), vmem_limit_bytes=VMEM_LIMIT),
        name="attn_proj_route",
    )(x2d, o, w_o.astype(BF16), *_route_args(ffn_g, router_w, router_b))
    return outs[0], outs[1:]


def _attention_route_layer(x2d, b, s, g, w_qkv, q_gain, k_gain, w_o, ffn_g, router_w, router_b):
    q, k, v = _qkv(x2d, g, w_qkv, q_gain, k_gain)
    o = _attention(q, k, v, b, s)
    return _proj_route(x2d, o, w_o, ffn_g, router_w, router_b)


def kernel(x, mix_norm, pool_w, pool_scale, w_qkv, q_norm, k_norm, w_o, ffn_norm, router_w, router_b,
           w_gate_up, b_gate_up, w_down, b_down):
    b, s, d = x.shape
    assert d == D_MODEL and s % ATT_Q == 0 and s % TOK_TILE == 0 and TOK_TILE % CHUNK == 0
    depth = mix_norm.shape[0]
    for i in range(depth):
        j = i // 2
        if i % 2 == 0:
            x2d, routed = _pool_route_layer(x, mix_norm[i], pool_w[j], pool_scale[j],
                                            ffn_norm[i], router_w[i], router_b[i])
        else:
            x2d, routed = _attention_route_layer(x.reshape(b * s, d), b, s, mix_norm[i], w_qkv[j], q_norm[j],
                                                 k_norm[j], w_o[j], ffn_norm[i], router_w[i], router_b[i])
        x2d = _moe_layer(x2d, routed, i, w_gate_up, b_gate_up, w_down, b_down)
        x = x2d.reshape(b, s, d)
    return x
```

```python
import functools

import jax
import jax.numpy as jnp
from jax import lax
from jax.experimental import pallas as pl
from jax.experimental.pallas import tpu as pltpu

D_MODEL = 1024
POOL_WINDOWS = (2, 4, 8, 16)
POOL_GROUP_DIM = D_MODEL // len(POOL_WINDOWS)
POOL_HALO = 16
HEAD_DIM = 64
N_HEADS = D_MODEL // HEAD_DIM
N_EXPERTS = 32
TOP_K = 4
D_FF = D_MODEL
SWIGLU_LIMIT = 7.0
SWIGLU_ALPHA = 1.702
RMS_EPS = 1e-6
LOG2_E = 1.4426950408889634

LANES = 128
SUBLANES = 8
VMEM_LIMIT = 56 * 1024 * 1024

CHUNK = 512
SEG_ALIGN = 2 * SUBLANES
CHUNK_ROWS = CHUNK * TOP_K + N_EXPERTS * SEG_ALIGN
GEMM_TILE = 512
GEMM_SUB = 256
TOK_TILE = 512
ROUTE_TILE = 1024
ATT_Q = 2048
ATT_K = LANES
ATT_GROUP = 4
ATT_DIAG_GROUP = 2

F32 = jnp.float32
BF16 = jnp.bfloat16


def _dot(a, b):
    return jnp.dot(a, b, preferred_element_type=F32)


def _dot_nt(a, b):
    return lax.dot_general(a, b, (((1,), (1,)), ((), ())), preferred_element_type=F32)


def _split_bf16(v):
    hi = v.astype(BF16)
    lo = (v - hi.astype(F32)).astype(BF16)
    return hi, lo


def _rmsnorm(v, g):
    return v * lax.rsqrt(jnp.mean(v * v, axis=-1, keepdims=True) + RMS_EPS) * g


def _pool_route_kernel(x_ref, halo_ref, g_ref, w_ref, scale_ref, rg_ref, wt_hi_ref, wt_lo_ref, rb_ref,
                       o_ref, hb_ref, slot_ref, gate_ref, cnt_ref):
    x1 = _pool_tile(x_ref[0], halo_ref[0], pl.program_id(1), g_ref, w_ref, scale_ref)
    o_ref[0] = x1
    _route_rows(x1, rg_ref, wt_hi_ref, wt_lo_ref, rb_ref, hb_ref, slot_ref, gate_ref, cnt_ref)


def _pool_tile(x, halo, i, g_ref, w_ref, scale_ref):
    g = g_ref[...]
    h = _rmsnorm(x, g)
    hh = _rmsnorm(halo, g)
    hh = jnp.where(i > 0, hh, 0.0)
    a = jnp.concatenate([hh, h], axis=0)
    ts = x.shape[0]
    pos = i * ts + lax.broadcasted_iota(jnp.int32, (ts, 1), 0)
    outs = []
    for gi, w in enumerate(POOL_WINDOWS):
        lo, hi = gi * POOL_GROUP_DIM, (gi + 1) * POOL_GROUP_DIM
        s = a[:, lo:hi]
        span = 1
        while span < w:
            s = s + pltpu.roll(s, span, axis=0)
            span *= 2
        cnt = jnp.minimum(pos + 1, w).astype(F32)
        pooled = s[POOL_HALO:, :] / cnt - h[:, lo:hi]
        outs.append(_dot(pooled.astype(BF16), w_ref[gi]))
    return x + jnp.concatenate(outs, axis=-1) * scale_ref[...]


def _route_specs(d, step_of):
    fixed2 = lambda *_: (0, 0)
    per_step = ROUTE_TILE // CHUNK
    in_specs = [
        pl.BlockSpec((1, d), fixed2),
        pl.BlockSpec((N_EXPERTS, d), fixed2),
        pl.BlockSpec((N_EXPERTS, d), fixed2),
        pl.BlockSpec((N_EXPERTS, 1), fixed2),
    ]
    out_specs = [
        pl.BlockSpec((ROUTE_TILE, d), lambda *ids: (step_of(*ids), 0)),
        pl.BlockSpec((per_step, SUBLANES, CHUNK), lambda *ids: (step_of(*ids), 0, 0)),
        pl.BlockSpec((per_step, SUBLANES, CHUNK), lambda *ids: (step_of(*ids), 0, 0)),
        pl.BlockSpec((per_step, N_EXPERTS, LANES), lambda *ids: (step_of(*ids), 0, 0)),
    ]
    return in_specs, out_specs


def _route_args(g, router_w, router_b):
    wt_hi, wt_lo = _split_bf16(router_w.T)
    return g.reshape(1, -1), wt_hi, wt_lo, router_b.reshape(N_EXPERTS, 1)


def _route_shapes(t, d):
    nc = t // CHUNK
    return [
        jax.ShapeDtypeStruct((t, d), BF16),
        jax.ShapeDtypeStruct((nc, SUBLANES, CHUNK), jnp.int32),
        jax.ShapeDtypeStruct((nc, SUBLANES, CHUNK), F32),
        jax.ShapeDtypeStruct((nc, N_EXPERTS, LANES), jnp.int32),
    ]


def _pool_route_layer(x, g, w, scale, ffn_g, router_w, router_b):
    b, s, d = x.shape
    ts = ROUTE_TILE
    tiles_per_seq = s // ts
    blocks_per_tile = ts // POOL_HALO
    route_in, route_out = _route_specs(d, lambda bi, i: bi * tiles_per_seq + i)
    outs = pl.pallas_call(
        _pool_route_kernel,
        grid=(b, tiles_per_seq),
        in_specs=[
            pl.BlockSpec((1, ts, d), lambda bi, i: (bi, i, 0)),
            pl.BlockSpec((1, POOL_HALO, d), lambda bi, i: (bi, jnp.maximum(i * blocks_per_tile - 1, 0), 0)),
            pl.BlockSpec((1, d), lambda bi, i: (0, 0)),
            pl.BlockSpec((len(POOL_WINDOWS), POOL_GROUP_DIM, POOL_GROUP_DIM), lambda bi, i: (0, 0, 0)),
            pl.BlockSpec((1, d), lambda bi, i: (0, 0)),
        ] + route_in,
        out_specs=[pl.BlockSpec((1, ts, d), lambda bi, i: (bi, i, 0))] + route_out,
        out_shape=[jax.ShapeDtypeStruct(x.shape, F32)] + _route_shapes(b * s, d),
        compiler_params=pltpu.CompilerParams(
            dimension_semantics=("arbitrary", "arbitrary"), vmem_limit_bytes=VMEM_LIMIT),
        name="pool_route",
    )(x, x, g.reshape(1, d), w.astype(BF16), scale.reshape(1, d), *_route_args(ffn_g, router_w, router_b))
    return outs[0].reshape(b * s, d), outs[1:]


def _route_rows(x, g_ref, wt_hi_ref, wt_lo_ref, b_ref, hb_ref, slot_ref, gate_ref, cnt_ref):
    t = x.shape[0]
    h = _rmsnorm(x, g_ref[...])
    h_hi, h_lo = _split_bf16(h)
    hb_ref[...] = h_hi
    logits = (_dot_nt(wt_hi_ref[...], h_hi) + _dot_nt(wt_lo_ref[...], h_hi)
              + _dot_nt(wt_hi_ref[...], h_lo) + b_ref[...])
    eio = lax.broadcasted_iota(jnp.int32, (N_EXPERTS, t), 0).astype(F32)
    vals, idxs = [], []
    l = logits
    for _ in range(TOP_K):
        m = jnp.max(l, axis=0, keepdims=True)
        idx = jnp.min(jnp.where(l == m, eio, float(N_EXPERTS)), axis=0, keepdims=True)
        vals.append(m)
        idxs.append(idx)
        l = jnp.where(eio == idx, -jnp.inf, l)
    es = [jnp.exp(v - vals[0]) for v in vals]
    denom = es[0] + es[1] + es[2] + es[3]
    sel = jnp.zeros((N_EXPERTS, t), F32)
    for idx in idxs:
        sel = sel + jnp.where(eio == idx, 1.0, 0.0)
    ti = lax.broadcasted_iota(jnp.int32, (t, t), 0)
    tj = lax.broadcasted_iota(jnp.int32, (t, t), 1)
    upper = jnp.where(jnp.logical_and(ti <= tj, ti // CHUNK == tj // CHUNK), 1.0, 0.0).astype(BF16)
    rank = _dot(sel.astype(BF16), upper)
    chunk_of_lane = lax.broadcasted_iota(jnp.int32, (N_EXPERTS, t), 1) // CHUNK
    n = jnp.zeros((N_EXPERTS, t), F32)
    for ci in range(t // CHUNK):
        last = (ci + 1) * CHUNK - 1
        n = jnp.where(chunk_of_lane == ci, rank[:, last:last + 1], n)
    n_units = jnp.floor((n + (SEG_ALIGN - 1)) * (1.0 / SEG_ALIGN))
    strict_lower = jnp.where(lax.broadcasted_iota(jnp.int32, (N_EXPERTS, N_EXPERTS), 1)
                             < lax.broadcasted_iota(jnp.int32, (N_EXPERTS, N_EXPERTS), 0), 1.0, 0.0).astype(BF16)
    off = _dot(strict_lower, n_units.astype(BF16)) * SEG_ALIGN
    dest = off + rank - 1.0
    slots = [jnp.sum(jnp.where(eio == idxs[k], dest, 0.0), axis=0, keepdims=True).astype(jnp.int32)
             for k in range(TOP_K)]
    for ci in range(t // CHUNK):
        lanes = slice(ci * CHUNK, (ci + 1) * CHUNK)
        for k in range(TOP_K):
            slot_ref[ci, k:k + 1, :] = slots[k][:, lanes]
            gate_ref[ci, k:k + 1, :] = (es[k] / denom)[:, lanes]
        slot_ref[ci, TOP_K:, :] = jnp.zeros((SUBLANES - TOP_K, CHUNK), jnp.int32)
        gate_ref[ci, TOP_K:, :] = jnp.zeros((SUBLANES - TOP_K, CHUNK), F32)
        cnt_ref[ci] = (n_units[:, ci * CHUNK:ci * CHUNK + LANES] * SEG_ALIGN).astype(jnp.int32)


def _segment_tables(cnt, n_tiles_max):
    seg_n = cnt[:, :, 0]
    seg_loc = jnp.cumsum(seg_n, axis=1) - seg_n
    tot = jnp.sum(seg_n, axis=0)
    tiles = (tot + GEMM_TILE - 1) // GEMM_TILE
    tile_end = jnp.cumsum(tiles)
    tile_start = tile_end - tiles
    front_pad = tiles * GEMM_TILE - tot
    seg_dst = (tile_start * GEMM_TILE + front_pad)[None, :] + jnp.cumsum(seg_n, axis=0) - seg_n
    used = tile_end[-1]
    tile_ids = jnp.arange(n_tiles_max, dtype=jnp.int32)
    tile_expert = jnp.sum(tile_ids[:, None] >= tile_end[None, :], axis=1).astype(jnp.int32)
    last_expert = jnp.sum(jnp.maximum(used - 1, 0) >= tile_end).astype(jnp.int32)
    tile_expert = jnp.where(tile_ids < used, tile_expert, last_expert)
    owner = tile_expert[:, None] == jnp.arange(N_EXPERTS, dtype=jnp.int32)[None, :]
    pad_here = jnp.sum(jnp.where(owner, jnp.where(tile_ids[:, None] == tile_start[None, :], front_pad[None, :], 0), 0),
                       axis=1)
    tile_valid = jnp.where(tile_ids < used, GEMM_TILE - pad_here, 0)
    experts = jnp.arange(N_EXPERTS, dtype=jnp.int32)
    ordinal = jnp.cumsum((tiles > 0).astype(jnp.int32)) - 1
    later = jnp.where(jnp.logical_and(experts[None, :] > experts[:, None], (tiles > 0)[None, :]), experts[None, :],
                      N_EXPERTS)
    next_expert = jnp.min(later, axis=1)
    next_expert = jnp.where(next_expert < N_EXPERTS, next_expert, -1)
    tile_buf = jnp.sum(jnp.where(owner, (ordinal % 2)[None, :], 0), axis=1)
    tile_next = jnp.sum(jnp.where(owner, next_expert[None, :], 0), axis=1)
    i32 = lambda v: v.astype(jnp.int32).reshape(-1)
    gap = jnp.concatenate([tile_start * GEMM_TILE, front_pad, used[None]])
    seg_tot = jnp.sum(seg_n, axis=1)
    tiles_info = (tile_expert, i32(tile_valid), i32(used), i32(tile_buf), i32(tile_next))
    return i32(seg_loc), i32(seg_n), i32(seg_dst), i32(seg_tot), tiles_info, i32(gap)


def _segment_copy(loc_ref, n_ref, dst_ref, c, e, local_buf, sorted_hbm, sem, to_sorted):
    j = c * N_EXPERTS + e
    n = pl.multiple_of(n_ref[j], SEG_ALIGN)
    local = local_buf.at[pl.ds(pl.multiple_of(loc_ref[j], SEG_ALIGN), n)]
    remote = sorted_hbm.at[pl.ds(pl.multiple_of(dst_ref[j], SEG_ALIGN), n)]
    return n, (pltpu.make_async_copy(local, remote, sem) if to_sorted
               else pltpu.make_async_copy(remote, local, sem))


def _for_each_segment(loc_ref, n_ref, dst_ref, c, local_buf, sorted_hbm, sem, to_sorted, action):
    def body(e, carry):
        n, cp = _segment_copy(loc_ref, n_ref, dst_ref, c, e, local_buf, sorted_hbm, sem, to_sorted)

        @pl.when(n > 0)
        def _():
            action(cp)
        return carry
    lax.fori_loop(0, N_EXPERTS, body, 0)


def _wait_chunk(tot_ref, c, local_buf, sorted_hbm, sem, to_sorted):
    n = pl.multiple_of(tot_ref[c], SEG_ALIGN)
    local = local_buf.at[pl.ds(0, n)]
    remote = sorted_hbm.at[pl.ds(0, n)]
    (pltpu.make_async_copy(local, remote, sem) if to_sorted else pltpu.make_async_copy(remote, local, sem)).wait()


def _zero_unused_rows(gap_ref, xs_hbm, zeros, sem, action):
    def expert_gap(e, carry):
        n = pl.multiple_of(gap_ref[N_EXPERTS + e], SEG_ALIGN)

        @pl.when(n > 0)
        def _():
            action(pltpu.make_async_copy(
                zeros.at[pl.ds(0, n)], xs_hbm.at[pl.ds(pl.multiple_of(gap_ref[e], SEG_ALIGN), n)], sem))
        return carry
    lax.fori_loop(0, N_EXPERTS, expert_gap, 0)

    def unused_tile(i, carry):
        action(pltpu.make_async_copy(zeros, xs_hbm.at[pl.ds(pl.multiple_of(i * GEMM_TILE, GEMM_TILE), GEMM_TILE)], sem))
        return carry
    lax.fori_loop(gap_ref[2 * N_EXPERTS], xs_hbm.shape[0] // GEMM_TILE, unused_tile, 0)


def _dispatch_kernel(loc_ref, n_ref, dst_ref, tot_ref, gap_ref, hb_ref, slot_ref, xs_hbm, buf, zeros, sems):
    c = pl.program_id(0)
    nc = pl.num_programs(0)
    par = c % 2
    start = lambda cp: cp.start()
    wait = lambda cp: cp.wait()

    @pl.when(c == 0)
    def _():
        zeros[...] = jnp.zeros(zeros.shape, BF16)
        _zero_unused_rows(gap_ref, xs_hbm, zeros, sems.at[2], start)

    @pl.when(c >= 2)
    def _():
        _wait_chunk(tot_ref, c - 2, buf.at[par], xs_hbm, sems.at[par], True)

    rows = lax.broadcasted_iota(jnp.int32, (CHUNK_ROWS, CHUNK), 0)
    hit = rows == slot_ref[0, 0:1, :]
    for k in range(1, TOP_K):
        hit = jnp.logical_or(hit, rows == slot_ref[0, k:k + 1, :])
    onehot = jnp.where(hit, 1.0, 0.0).astype(BF16)
    buf[par] = _dot(onehot, hb_ref[...]).astype(BF16)
    _for_each_segment(loc_ref, n_ref, dst_ref, c, buf.at[par], xs_hbm, sems.at[par], True, start)

    @pl.when(c == nc - 1)
    def _():
        _zero_unused_rows(gap_ref, xs_hbm, zeros, sems.at[2], wait)

        @pl.when(c >= 1)
        def _():
            _wait_chunk(tot_ref, c - 1, buf.at[1 - par], xs_hbm, sems.at[1 - par], True)
        _wait_chunk(tot_ref, c, buf.at[par], xs_hbm, sems.at[par], True)


def _dispatch(hb, slots, seg_loc, seg_n, seg_dst, seg_tot, gap, n_rows):
    t, d = hb.shape
    nc = t // CHUNK
    return pl.pallas_call(
        _dispatch_kernel,
        grid_spec=pltpu.PrefetchScalarGridSpec(
            num_scalar_prefetch=5,
            grid=(nc,),
            in_specs=[
                pl.BlockSpec((CHUNK, d), lambda c, *_: (c, 0)),
                pl.BlockSpec((1, SUBLANES, CHUNK), lambda c, *_: (c, 0, 0)),
            ],
            out_specs=pl.BlockSpec(memory_space=pl.ANY),
            scratch_shapes=[pltpu.VMEM((2, CHUNK_ROWS, d), BF16), pltpu.VMEM((GEMM_TILE, d), BF16),
                            pltpu.SemaphoreType.DMA((3,))],
        ),
        out_shape=jax.ShapeDtypeStruct((n_rows, d), BF16),
        compiler_params=pltpu.CompilerParams(dimension_semantics=("arbitrary",), vmem_limit_bytes=VMEM_LIMIT),
        name="moe_dispatch",
    )(seg_loc, seg_n, seg_dst, seg_tot, gap, hb, slots)


def _expert_kernel(layer, te_ref, valid_ref, used_ref, buf_ref, next_ref, x_ref, wgu_hbm, bgu_ref, wd_hbm, bd_ref,
                   o_ref, wgu_f32, wd_f32, wgu_bf, wd_bf, sems):
    i = pl.program_id(0)
    prev = te_ref[jnp.maximum(i - 1, 0)]
    buf = buf_ref[i]

    def weight_copies(expert, b):
        e = layer * N_EXPERTS + expert
        return (pltpu.make_async_copy(wgu_hbm.at[e], wgu_f32.at[b], sems.at[b]),
                pltpu.make_async_copy(wd_hbm.at[e], wd_f32.at[b], sems.at[b]))

    @pl.when(i == 0)
    def _():
        for cp in weight_copies(te_ref[0], buf):
            cp.start()

    @pl.when(jnp.logical_or(i == 0, te_ref[i] != prev))
    def _():
        for cp in weight_copies(te_ref[i], buf):
            cp.wait()

        @pl.when(next_ref[i] >= 0)
        def _():
            for cp in weight_copies(next_ref[i], 1 - buf):
                cp.start()
        wgu_bf[...] = wgu_f32[buf].astype(BF16)
        wd_bf[...] = wd_f32[buf].astype(BF16)

    def expert_mlp(rows):
        gu = _dot(x_ref[rows, :], wgu_bf[...]) + bgu_ref[0]
        gate = jnp.minimum(gu[:, :D_FF], SWIGLU_LIMIT)
        up = jnp.clip(gu[:, D_FF:], -SWIGLU_LIMIT, SWIGLU_LIMIT)
        act = (up + 1.0) * (gate * (1.0 / (1.0 + jnp.exp(-SWIGLU_ALPHA * gate))))
        out = _dot(act.astype(BF16), wd_bf[...]) + bd_ref[0]
        o_ref[rows, :] = out.astype(BF16)

    n_sub = GEMM_TILE // GEMM_SUB

    @pl.when(valid_ref[i] > GEMM_TILE - GEMM_SUB)
    def _():
        expert_mlp(pl.ds(0, GEMM_TILE))

    @pl.when(valid_ref[i] <= GEMM_TILE - GEMM_SUB)
    def _():
        o_ref[pl.ds(0, GEMM_SUB), :] = jnp.zeros((GEMM_SUB, D_MODEL), BF16)
        for sub in range(1, n_sub):
            rows = pl.ds(sub * GEMM_SUB, GEMM_SUB)

            @pl.when(valid_ref[i] > GEMM_TILE - (sub + 1) * GEMM_SUB)
            def _():
                expert_mlp(rows)

            @pl.when(valid_ref[i] <= GEMM_TILE - (sub + 1) * GEMM_SUB)
            def _():
                o_ref[rows, :] = jnp.zeros((GEMM_SUB, D_MODEL), BF16)


def _experts(xs, tiles_info, layer, w_gate_up, b_gate_up, w_down, b_down):
    n_rows, d = xs.shape
    n_tiles = n_rows // GEMM_TILE
    n_stacked = w_gate_up.shape[0] * N_EXPERTS
    row_map = lambda i, te, tv, u, *_: (jnp.minimum(i, jnp.maximum(u[0] - 1, 0)), 0)
    exp_map = lambda i, te, *_: (layer * N_EXPERTS + te[i], 0, 0)
    return pl.pallas_call(
        functools.partial(_expert_kernel, layer),
        grid_spec=pltpu.PrefetchScalarGridSpec(
            num_scalar_prefetch=5,
            grid=(n_tiles,),
            in_specs=[
                pl.BlockSpec((GEMM_TILE, d), row_map),
                pl.BlockSpec(memory_space=pl.ANY),
                pl.BlockSpec((1, 1, 2 * D_FF), exp_map),
                pl.BlockSpec(memory_space=pl.ANY),
                pl.BlockSpec((1, 1, d), exp_map),
            ],
            out_specs=pl.BlockSpec((GEMM_TILE, d), lambda i, *_: (i, 0)),
            scratch_shapes=[pltpu.VMEM((2, d, 2 * D_FF), F32), pltpu.VMEM((2, D_FF, d), F32),
                            pltpu.VMEM((d, 2 * D_FF), BF16), pltpu.VMEM((D_FF, d), BF16),
                            pltpu.SemaphoreType.DMA((2,))],
        ),
        out_shape=jax.ShapeDtypeStruct((n_rows, d), BF16),
        compiler_params=pltpu.CompilerParams(dimension_semantics=("arbitrary",), vmem_limit_bytes=VMEM_LIMIT),
        name="moe_experts",
    )(*tiles_info, xs, w_gate_up.reshape(n_stacked, d, 2 * D_FF),
      b_gate_up.reshape(n_stacked, 1, 2 * D_FF), w_down.reshape(n_stacked, D_FF, d), b_down.reshape(n_stacked, 1, d))


def _combine_kernel(loc_ref, n_ref, dst_ref, tot_ref, x_ref, slot_t_ref, gate_t_ref, ys_hbm, o_ref, buf, sems):
    c = pl.program_id(0)
    nc = pl.num_programs(0)
    par = c % 2
    start = lambda cp: cp.start()

    @pl.when(c == 0)
    def _():
        buf[...] = jnp.zeros(buf.shape, BF16)
        _for_each_segment(loc_ref, n_ref, dst_ref, c, buf.at[0], ys_hbm, sems.at[0], False, start)

    @pl.when(c + 1 < nc)
    def _():
        _for_each_segment(loc_ref, n_ref, dst_ref, c + 1, buf.at[1 - par], ys_hbm, sems.at[1 - par], False, start)

    _wait_chunk(tot_ref, c, buf.at[par], ys_hbm, sems.at[par], False)

    cols = lax.broadcasted_iota(jnp.int32, (CHUNK, CHUNK_ROWS), 1)
    st = slot_t_ref[0]
    gt = gate_t_ref[0]
    back = jnp.zeros((CHUNK, CHUNK_ROWS), F32)
    for k in range(TOP_K):
        back = jnp.where(cols == st[:, k:k + 1], gt[:, k:k + 1], back)
    back = back.astype(BF16)
    o_ref[...] = x_ref[...] + _dot(back, buf[par])


def _combine(x2d, ys, slots_t, gates_t, seg_loc, seg_n, seg_dst, seg_tot):
    t, d = x2d.shape
    nc = t // CHUNK
    return pl.pallas_call(
        _combine_kernel,
        grid_spec=pltpu.PrefetchScalarGridSpec(
            num_scalar_prefetch=4,
            grid=(nc,),
            in_specs=[
                pl.BlockSpec((CHUNK, d), lambda c, *_: (c, 0)),
                pl.BlockSpec((1, CHUNK, TOP_K), lambda c, *_: (c, 0, 0)),
                pl.BlockSpec((1, CHUNK, TOP_K), lambda c, *_: (c, 0, 0)),
                pl.BlockSpec(memory_space=pl.ANY),
            ],
            out_specs=pl.BlockSpec((CHUNK, d), lambda c, *_: (c, 0)),
            scratch_shapes=[pltpu.VMEM((2, CHUNK_ROWS, d), BF16), pltpu.SemaphoreType.DMA((2,))],
        ),
        out_shape=jax.ShapeDtypeStruct((t, d), F32),
        compiler_params=pltpu.CompilerParams(dimension_semantics=("arbitrary",), vmem_limit_bytes=VMEM_LIMIT),
        name="moe_combine",
    )(seg_loc, seg_n, seg_dst, seg_tot, x2d, slots_t, gates_t, ys)


def _moe_layer(x2d, routed, layer, w_gate_up, b_gate_up, w_down, b_down):
    t, d = x2d.shape
    nc = t // CHUNK
    n_tiles_max = (t * TOP_K + nc * N_EXPERTS * (SEG_ALIGN - 1)) // GEMM_TILE + N_EXPERTS
    hb, slots, gates, cnt = routed
    seg_loc, seg_n, seg_dst, seg_tot, tiles_info, gap = _segment_tables(cnt, n_tiles_max)
    xs = _dispatch(hb, slots, seg_loc, seg_n, seg_dst, seg_tot, gap, n_tiles_max * GEMM_TILE)
    ys = _experts(xs, tiles_info, layer, w_gate_up, b_gate_up, w_down, b_down)
    slots_t = jnp.swapaxes(slots[:, :TOP_K, :], 1, 2)
    gates_t = jnp.swapaxes(gates[:, :TOP_K, :], 1, 2)
    return _combine(x2d, ys, slots_t, gates_t, seg_loc, seg_n, seg_dst, seg_tot)


def _qkv_kernel(x_ref, g_ref, w_ref, qg_ref, kg_ref, q_ref, k_ref, v_ref):
    d = x_ref.shape[1]
    h = _rmsnorm(x_ref[...], g_ref[...]).astype(BF16)
    qkv = _dot(h, w_ref[...])
    width = 2 * LANES
    li = lax.broadcasted_iota(jnp.int32, (width, width), 0) // HEAD_DIM
    lj = lax.broadcasted_iota(jnp.int32, (width, width), 1) // HEAD_DIM
    same_head = jnp.where(li == lj, 1.0, 0.0).astype(BF16)

    def head_norm(v, gain, out_scale):
        parts = []
        for j in range(d // width):
            vj = v[:, j * width:(j + 1) * width]
            ss = _dot((vj * vj).astype(BF16), same_head)
            parts.append(vj * lax.rsqrt(ss * (1.0 / HEAD_DIM) + RMS_EPS))
        return (jnp.concatenate(parts, axis=-1) * gain * out_scale).astype(BF16)

    q_ref[...] = head_norm(qkv[:, :d], qg_ref[...], HEAD_DIM ** -0.5 * LOG2_E)
    k_ref[...] = head_norm(qkv[:, d:2 * d], kg_ref[...], 1.0)
    v_ref[...] = qkv[:, 2 * d:].astype(BF16)


def _qkv(x2d, g, w_qkv, q_gain, k_gain):
    t, d = x2d.shape
    tt = min(TOK_TILE, t)
    row = lambda i: (i, 0)
    fixed = lambda i: (0, 0)
    return pl.pallas_call(
        _qkv_kernel,
        grid=(t // tt,),
        in_specs=[
            pl.BlockSpec((tt, d), row),
            pl.BlockSpec((1, d), fixed),
            pl.BlockSpec((d, 3 * d), fixed),
            pl.BlockSpec((1, d), fixed),
            pl.BlockSpec((1, d), fixed),
        ],
        out_specs=[pl.BlockSpec((tt, d), row)] * 3,
        out_shape=[jax.ShapeDtypeStruct((t, d), BF16)] * 3,
        compiler_params=pltpu.CompilerParams(dimension_semantics=("arbitrary",), vmem_limit_bytes=VMEM_LIMIT),
        name="attn_qkv",
    )(x2d, g.reshape(1, d), w_qkv.astype(BF16), jnp.tile(q_gain, N_HEADS).reshape(1, d),
      jnp.tile(k_gain, N_HEADS).reshape(1, d))


def _attn_kernel(q_ref, k_ref, v_ref, o_ref, acc_ref, carry_ref):
    qi = pl.program_id(2)
    tq, kb = ATT_Q, ATT_K
    q = q_ref[0]
    head_of_lane = lax.broadcasted_iota(jnp.int32, (tq, LANES), 1) // HEAD_DIM
    q_heads = [jnp.where(head_of_lane == hd, q, jnp.zeros_like(q)) for hd in range(2)]
    r2 = lax.broadcasted_iota(jnp.int32, (2 * kb, 2 * kb), 0)
    c2 = lax.broadcasted_iota(jnp.int32, (2 * kb, 2 * kb), 1)
    suffix2 = jnp.where(jnp.logical_and(r2 // kb == c2 // kb, r2 >= c2), 1.0, 0.0).astype(BF16)

    acc_ref[...] = jnp.zeros(acc_ref.shape, F32)
    carry_ref[...] = jnp.zeros(carry_ref.shape, F32)

    def key_group(j0, diag_offsets, r0, r1):
        nb = len(diag_offsets)
        rows = r1 - r0
        start = pl.multiple_of(j0 * kb, kb)
        kk = k_ref[0, pl.ds(start, nb * kb), :]
        vv = v_ref[0, pl.ds(start, nb * kb), :]
        row = lax.broadcasted_iota(jnp.int32, (rows, kb), 0) + r0
        key = lax.broadcasted_iota(jnp.int32, (rows, kb), 1)
        z_all = [_dot_nt(q_heads[hd][r0:r1, :], kk) for hd in range(2)]
        carry = [carry_ref[hd, r0:r1, :] for hd in range(2)]
        probs = [[None] * nb for _ in range(2)]
        for b in reversed(range(nb)):
            causal = None if diag_offsets[b] is None else key + diag_offsets[b] < row
            zs, log_keeps = [], []
            for hd in range(2):
                z = z_all[hd][:, b * kb:(b + 1) * kb]
                neg = -z
                log_keep = jnp.minimum(neg, 0.0) - jnp.log(1.0 + jnp.exp2(jnp.minimum(z, neg))) * LOG2_E
                if causal is not None:
                    log_keep = jnp.where(causal, log_keep, 0.0)
                zs.append(z)
                log_keeps.append(log_keep.astype(BF16))
            sums = _dot(jnp.concatenate(log_keeps, axis=1), suffix2)
            for hd in range(2):
                s_in = sums[:, hd * kb:(hd + 1) * kb]
                a = jnp.exp2(zs[hd] + s_in + carry[hd])
                if causal is not None:
                    a = jnp.where(causal, a, 0.0)
                probs[hd][b] = a.astype(BF16)
                carry[hd] = carry[hd] + jnp.broadcast_to(s_in[:, 0:1], (rows, kb))
        for hd in range(2):
            carry_ref[hd, r0:r1, :] = carry[hd]
            acc_ref[hd, r0:r1, :] += _dot(jnp.concatenate(probs[hd], axis=1), vv)

    blocks_per_q = tq // kb
    for g in reversed(range(blocks_per_q // ATT_DIAG_GROUP)):
        offs = [(g * ATT_DIAG_GROUP + b) * kb for b in range(ATT_DIAG_GROUP)]
        key_group(qi * blocks_per_q + g * ATT_DIAG_GROUP, offs, offs[0], tq)

    def body(s, c):
        key_group(qi * blocks_per_q - (s + 1) * ATT_GROUP, [None] * ATT_GROUP, 0, tq)
        return c
    lax.fori_loop(0, qi * (blocks_per_q // ATT_GROUP), body, 0)
    o_ref[0] = jnp.where(head_of_lane == 0, acc_ref[0], acc_ref[1]).astype(BF16)


def _attention(q, k, v, b, s):
    d = q.shape[-1]
    q3, k3, v3 = (a.reshape(b, s, d) for a in (q, k, v))
    pairs = d // LANES
    o = pl.pallas_call(
        _attn_kernel,
        grid=(b, pairs, s // ATT_Q),
        in_specs=[
            pl.BlockSpec((1, ATT_Q, LANES), lambda bi, hp, i: (bi, i, hp)),
            pl.BlockSpec((1, s, LANES), lambda bi, hp, i: (bi, 0, hp)),
            pl.BlockSpec((1, s, LANES), lambda bi, hp, i: (bi, 0, hp)),
        ],
        out_specs=pl.BlockSpec((1, ATT_Q, LANES), lambda bi, hp, i: (bi, i, hp)),
        out_shape=jax.ShapeDtypeStruct((b, s, d), BF16),
        scratch_shapes=[pltpu.VMEM((2, ATT_Q, LANES), F32), pltpu.VMEM((2, ATT_Q, LANES), F32)],
        compiler_params=pltpu.CompilerParams(
            dimension_semantics=("arbitrary", "arbitrary", "arbitrary"), vmem_limit_bytes=VMEM_LIMIT),
        name="attn_core",
    )(q3, k3, v3)
    return o.reshape(b * s, d)


def _proj_route_kernel(x_ref, o_ref, w_ref, rg_ref, wt_hi_ref, wt_lo_ref, rb_ref,
                       y_ref, hb_ref, slot_ref, gate_ref, cnt_ref):
    x2 = x_ref[...] + _dot(o_ref[...], w_ref[...])
    y_ref[...] = x2
    _route_rows(x2, rg_ref, wt_hi_ref, wt_lo_ref, rb_ref, hb_ref, slot_ref, gate_ref, cnt_ref)


def _proj_route(x2d, o, w_o, ffn_g, router_w, router_b):
    t, d = x2d.shape
    route_in, route_out = _route_specs(d, lambda c: c)
    outs = pl.pallas_call(
        _proj_route_kernel,
        grid=(t // ROUTE_TILE,),
        in_specs=[
            pl.BlockSpec((ROUTE_TILE, d), lambda c: (c, 0)),
            pl.BlockSpec((ROUTE_TILE, d), lambda c: (c, 0)),
            pl.BlockSpec((d, d), lambda c: (0, 0)),
        ] + route_in,
        out_specs=[pl.BlockSpec((ROUTE_TILE, d), lambda c: (c, 0))] + route_out,
        out_shape=[jax.ShapeDtypeStruct((t, d), F32)] + _route_shapes(t, d),
        compiler_params=pltpu.CompilerParams(dimension_semantics=("arbitrary",), vmem_limit_bytes=VMEM_LIMIT),
        name="attn_proj_route",
    )(x2d, o, w_o.astype(BF16), *_route_args(ffn_g, router_w, router_b))
    return outs[0], outs[1:]


def _attention_route_layer(x2d, b, s, g, w_qkv, q_gain, k_gain, w_o, ffn_g, router_w, router_b):
    q, k, v = _qkv(x2d, g, w_qkv, q_gain, k_gain)
    o = _attention(q, k, v, b, s)
    return _proj_route(x2d, o, w_o, ffn_g, router_w, router_b)


def kernel(x, mix_norm, pool_w, pool_scale, w_qkv, q_norm, k_norm, w_o, ffn_norm, router_w, router_b,
           w_gate_up, b_gate_up, w_down, b_down):
    b, s, d = x.shape
    assert d == D_MODEL and s % ATT_Q == 0 and s % ROUTE_TILE == 0 and ROUTE_TILE % CHUNK == 0
    depth = mix_norm.shape[0]
    for i in range(depth):
        j = i // 2
        if i % 2 == 0:
            x2d, routed = _pool_route_layer(x, mix_norm[i], pool_w[j], pool_scale[j],
                                            ffn_norm[i], router_w[i], router_b[i])
        else:
            x2d, routed = _attention_route_layer(x.reshape(b * s, d), b, s, mix_norm[i], w_qkv[j], q_norm[j],
                                                 k_norm[j], w_o[j], ffn_norm[i], router_w[i], router_b[i])
        x2d = _moe_layer(x2d, routed, i, w_gate_up, b_gate_up, w_down, b_down)
        x = x2d.reshape(b, s, d)
    return x
```

```python
import functools

import jax
import jax.numpy as jnp
from jax import lax
from jax.experimental import pallas as pl
from jax.experimental.pallas import tpu as pltpu

D_MODEL = 1024
POOL_WINDOWS = (2, 4, 8, 16)
POOL_GROUP_DIM = D_MODEL // len(POOL_WINDOWS)
POOL_HALO = 16
HEAD_DIM = 64
N_HEADS = D_MODEL // HEAD_DIM
N_EXPERTS = 32
TOP_K = 4
D_FF = D_MODEL
SWIGLU_LIMIT = 7.0
SWIGLU_ALPHA = 1.702
RMS_EPS = 1e-6
LOG2_E = 1.4426950408889634

LANES = 128
SUBLANES = 8
VMEM_LIMIT = 56 * 1024 * 1024

CHUNK = 512
SEG_ALIGN = 2 * SUBLANES
CHUNK_ROWS = CHUNK * TOP_K + N_EXPERTS * SEG_ALIGN
GEMM_TILE = 512
GEMM_SUB = 256
TOK_TILE = 512
ROUTE_TILE = 1024
ATT_Q = 2048
ATT_K = LANES
ATT_GROUP = 4
ATT_DIAG_GROUP = 2

F32 = jnp.float32
BF16 = jnp.bfloat16


def _dot(a, b):
    return jnp.dot(a, b, preferred_element_type=F32)


def _dot_nt(a, b):
    return lax.dot_general(a, b, (((1,), (1,)), ((), ())), preferred_element_type=F32)


def _split_bf16(v):
    hi = v.astype(BF16)
    lo = (v - hi.astype(F32)).astype(BF16)
    return hi, lo


def _rmsnorm(v, g):
    return v * lax.rsqrt(jnp.mean(v * v, axis=-1, keepdims=True) + RMS_EPS) * g


def _pool_route_kernel(x_ref, halo_ref, g_ref, w_ref, scale_ref, rg_ref, wt_hi_ref, wt_lo_ref, rb_ref,
                       o_ref, hb_ref, slot_ref, gate_ref, cnt_ref):
    x1 = _pool_tile(x_ref[0], halo_ref[0], pl.program_id(1), g_ref, w_ref, scale_ref)
    o_ref[0] = x1
    _route_rows(x1, rg_ref, wt_hi_ref, wt_lo_ref, rb_ref, hb_ref, slot_ref, gate_ref, cnt_ref)


def _pool_tile(x, halo, i, g_ref, w_ref, scale_ref):
    g = g_ref[...]
    h = _rmsnorm(x, g)
    hh = _rmsnorm(halo, g)
    hh = jnp.where(i > 0, hh, 0.0)
    a = jnp.concatenate([hh, h], axis=0)
    ts = x.shape[0]
    pos = i * ts + lax.broadcasted_iota(jnp.int32, (ts, 1), 0)
    outs = []
    for gi, w in enumerate(POOL_WINDOWS):
        lo, hi = gi * POOL_GROUP_DIM, (gi + 1) * POOL_GROUP_DIM
        s = a[:, lo:hi]
        span = 1
        while span < w:
            s = s + pltpu.roll(s, span, axis=0)
            span *= 2
        cnt = jnp.minimum(pos + 1, w).astype(F32)
        pooled = s[POOL_HALO:, :] / cnt - h[:, lo:hi]
        outs.append(_dot(pooled.astype(BF16), w_ref[gi]))
    return x + jnp.concatenate(outs, axis=-1) * scale_ref[...]


def _route_specs(d, step_of):
    fixed2 = lambda *_: (0, 0)
    per_step = ROUTE_TILE // CHUNK
    in_specs = [
        pl.BlockSpec((1, d), fixed2),
        pl.BlockSpec((N_EXPERTS, d), fixed2),
        pl.BlockSpec((N_EXPERTS, d), fixed2),
        pl.BlockSpec((N_EXPERTS, 1), fixed2),
    ]
    out_specs = [
        pl.BlockSpec((ROUTE_TILE, d), lambda *ids: (step_of(*ids), 0)),
        pl.BlockSpec((per_step, SUBLANES, CHUNK), lambda *ids: (step_of(*ids), 0, 0)),
        pl.BlockSpec((per_step, SUBLANES, CHUNK), lambda *ids: (step_of(*ids), 0, 0)),
        pl.BlockSpec((per_step, N_EXPERTS, LANES), lambda *ids: (step_of(*ids), 0, 0)),
    ]
    return in_specs, out_specs


def _route_args(g, router_w, router_b):
    wt_hi, wt_lo = _split_bf16(router_w.T)
    return g.reshape(1, -1), wt_hi, wt_lo, router_b.reshape(N_EXPERTS, 1)


def _route_shapes(t, d):
    nc = t // CHUNK
    return [
        jax.ShapeDtypeStruct((t, d), BF16),
        jax.ShapeDtypeStruct((nc, SUBLANES, CHUNK), jnp.int32),
        jax.ShapeDtypeStruct((nc, SUBLANES, CHUNK), F32),
        jax.ShapeDtypeStruct((nc, N_EXPERTS, LANES), jnp.int32),
    ]


def _pool_route_layer(x, g, w, scale, ffn_g, router_w, router_b):
    b, s, d = x.shape
    ts = ROUTE_TILE
    tiles_per_seq = s // ts
    blocks_per_tile = ts // POOL_HALO
    route_in, route_out = _route_specs(d, lambda bi, i: bi * tiles_per_seq + i)
    outs = pl.pallas_call(
        _pool_route_kernel,
        grid=(b, tiles_per_seq),
        in_specs=[
            pl.BlockSpec((1, ts, d), lambda bi, i: (bi, i, 0)),
            pl.BlockSpec((1, POOL_HALO, d), lambda bi, i: (bi, jnp.maximum(i * blocks_per_tile - 1, 0), 0)),
            pl.BlockSpec((1, d), lambda bi, i: (0, 0)),
            pl.BlockSpec((len(POOL_WINDOWS), POOL_GROUP_DIM, POOL_GROUP_DIM), lambda bi, i: (0, 0, 0)),
            pl.BlockSpec((1, d), lambda bi, i: (0, 0)),
        ] + route_in,
        out_specs=[pl.BlockSpec((1, ts, d), lambda bi, i: (bi, i, 0))] + route_out,
        out_shape=[jax.ShapeDtypeStruct(x.shape, F32)] + _route_shapes(b * s, d),
        compiler_params=pltpu.CompilerParams(
            dimension_semantics=("arbitrary", "arbitrary"), vmem_limit_bytes=VMEM_LIMIT),
        name="pool_route",
    )(x, x, g.reshape(1, d), w.astype(BF16), scale.reshape(1, d), *_route_args(ffn_g, router_w, router_b))
    return outs[0].reshape(b * s, d), outs[1:]


def _route_rows(x, g_ref, wt_hi_ref, wt_lo_ref, b_ref, hb_ref, slot_ref, gate_ref, cnt_ref):
    t = x.shape[0]
    h = _rmsnorm(x, g_ref[...])
    h_hi, h_lo = _split_bf16(h)
    hb_ref[...] = h_hi
    logits = (_dot_nt(wt_hi_ref[...], h_hi) + _dot_nt(wt_lo_ref[...], h_hi)
              + _dot_nt(wt_hi_ref[...], h_lo) + b_ref[...])
    eio = lax.broadcasted_iota(jnp.int32, (N_EXPERTS, t), 0).astype(F32)
    vals, idxs = [], []
    l = logits
    for _ in range(TOP_K):
        m = jnp.max(l, axis=0, keepdims=True)
        idx = jnp.min(jnp.where(l == m, eio, float(N_EXPERTS)), axis=0, keepdims=True)
        vals.append(m)
        idxs.append(idx)
        l = jnp.where(eio == idx, -jnp.inf, l)
    es = [jnp.exp(v - vals[0]) for v in vals]
    denom = es[0] + es[1] + es[2] + es[3]
    sel = jnp.zeros((N_EXPERTS, t), F32)
    for idx in idxs:
        sel = sel + jnp.where(eio == idx, 1.0, 0.0)
    ti = lax.broadcasted_iota(jnp.int32, (t, t), 0)
    tj = lax.broadcasted_iota(jnp.int32, (t, t), 1)
    upper = jnp.where(jnp.logical_and(ti <= tj, ti // CHUNK == tj // CHUNK), 1.0, 0.0).astype(BF16)
    rank = _dot(sel.astype(BF16), upper)
    chunk_of_lane = lax.broadcasted_iota(jnp.int32, (N_EXPERTS, t), 1) // CHUNK
    n = jnp.zeros((N_EXPERTS, t), F32)
    for ci in range(t // CHUNK):
        last = (ci + 1) * CHUNK - 1
        n = jnp.where(chunk_of_lane == ci, rank[:, last:last + 1], n)
    n_units = jnp.floor((n + (SEG_ALIGN - 1)) * (1.0 / SEG_ALIGN))
    strict_lower = jnp.where(lax.broadcasted_iota(jnp.int32, (N_EXPERTS, N_EXPERTS), 1)
                             < lax.broadcasted_iota(jnp.int32, (N_EXPERTS, N_EXPERTS), 0), 1.0, 0.0).astype(BF16)
    off = _dot(strict_lower, n_units.astype(BF16)) * SEG_ALIGN
    dest = off + rank - 1.0
    slots = [jnp.sum(jnp.where(eio == idxs[k], dest, 0.0), axis=0, keepdims=True).astype(jnp.int32)
             for k in range(TOP_K)]
    for ci in range(t // CHUNK):
        lanes = slice(ci * CHUNK, (ci + 1) * CHUNK)
        for k in range(TOP_K):
            slot_ref[ci, k:k + 1, :] = slots[k][:, lanes]
            gate_ref[ci, k:k + 1, :] = (es[k] / denom)[:, lanes]
        slot_ref[ci, TOP_K:, :] = jnp.zeros((SUBLANES - TOP_K, CHUNK), jnp.int32)
        gate_ref[ci, TOP_K:, :] = jnp.zeros((SUBLANES - TOP_K, CHUNK), F32)
        cnt_ref[ci] = (n_units[:, ci * CHUNK:ci * CHUNK + LANES] * SEG_ALIGN).astype(jnp.int32)


def _segment_tables(cnt, n_tiles_max):
    seg_n = cnt[:, :, 0]
    seg_loc = jnp.cumsum(seg_n, axis=1) - seg_n
    tot = jnp.sum(seg_n, axis=0)
    tiles = (tot + GEMM_TILE - 1) // GEMM_TILE
    tile_end = jnp.cumsum(tiles)
    tile_start = tile_end - tiles
    front_pad = tiles * GEMM_TILE - tot
    seg_dst = (tile_start * GEMM_TILE + front_pad)[None, :] + jnp.cumsum(seg_n, axis=0) - seg_n
    used = tile_end[-1]
    tile_ids = jnp.arange(n_tiles_max, dtype=jnp.int32)
    tile_expert = jnp.sum(tile_ids[:, None] >= tile_end[None, :], axis=1).astype(jnp.int32)
    last_expert = jnp.sum(jnp.maximum(used - 1, 0) >= tile_end).astype(jnp.int32)
    tile_expert = jnp.where(tile_ids < used, tile_expert, last_expert)
    owner = tile_expert[:, None] == jnp.arange(N_EXPERTS, dtype=jnp.int32)[None, :]
    pad_here = jnp.sum(jnp.where(owner, jnp.where(tile_ids[:, None] == tile_start[None, :], front_pad[None, :], 0), 0),
                       axis=1)
    tile_valid = jnp.where(tile_ids < used, GEMM_TILE - pad_here, 0)
    experts = jnp.arange(N_EXPERTS, dtype=jnp.int32)
    ordinal = jnp.cumsum((tiles > 0).astype(jnp.int32)) - 1
    later = jnp.where(jnp.logical_and(experts[None, :] > experts[:, None], (tiles > 0)[None, :]), experts[None, :],
                      N_EXPERTS)
    next_expert = jnp.min(later, axis=1)
    next_expert = jnp.where(next_expert < N_EXPERTS, next_expert, -1)
    tile_buf = jnp.sum(jnp.where(owner, (ordinal % 2)[None, :], 0), axis=1)
    tile_next = jnp.sum(jnp.where(owner, next_expert[None, :], 0), axis=1)
    i32 = lambda v: v.astype(jnp.int32).reshape(-1)
    gap = jnp.concatenate([tile_start * GEMM_TILE, front_pad, used[None]])
    seg_tot = jnp.sum(seg_n, axis=1)
    tiles_info = (tile_expert, i32(tile_valid), i32(used), i32(tile_buf), i32(tile_next))
    return i32(seg_loc), i32(seg_n), i32(seg_dst), i32(seg_tot), tiles_info, i32(gap)


def _segment_copy(loc_ref, n_ref, dst_ref, c, e, local_buf, sorted_hbm, sem, to_sorted):
    j = c * N_EXPERTS + e
    n = pl.multiple_of(n_ref[j], SEG_ALIGN)
    local = local_buf.at[pl.ds(pl.multiple_of(loc_ref[j], SEG_ALIGN), n)]
    remote = sorted_hbm.at[pl.ds(pl.multiple_of(dst_ref[j], SEG_ALIGN), n)]
    return n, (pltpu.make_async_copy(local, remote, sem) if to_sorted
               else pltpu.make_async_copy(remote, local, sem))


def _for_each_segment(loc_ref, n_ref, dst_ref, c, local_buf, sorted_hbm, sem, to_sorted, action):
    def body(e, carry):
        n, cp = _segment_copy(loc_ref, n_ref, dst_ref, c, e, local_buf, sorted_hbm, sem, to_sorted)

        @pl.when(n > 0)
        def _():
            action(cp)
        return carry
    lax.fori_loop(0, N_EXPERTS, body, 0)


def _wait_chunk(tot_ref, c, local_buf, sorted_hbm, sem, to_sorted):
    n = pl.multiple_of(tot_ref[c], SEG_ALIGN)
    local = local_buf.at[pl.ds(0, n)]
    remote = sorted_hbm.at[pl.ds(0, n)]
    (pltpu.make_async_copy(local, remote, sem) if to_sorted else pltpu.make_async_copy(remote, local, sem)).wait()


def _zero_unused_rows(gap_ref, xs_hbm, zeros, sem, action):
    def expert_gap(e, carry):
        n = pl.multiple_of(gap_ref[N_EXPERTS + e], SEG_ALIGN)

        @pl.when(n > 0)
        def _():
            action(pltpu.make_async_copy(
                zeros.at[pl.ds(0, n)], xs_hbm.at[pl.ds(pl.multiple_of(gap_ref[e], SEG_ALIGN), n)], sem))
        return carry
    lax.fori_loop(0, N_EXPERTS, expert_gap, 0)

    def unused_tile(i, carry):
        action(pltpu.make_async_copy(zeros, xs_hbm.at[pl.ds(pl.multiple_of(i * GEMM_TILE, GEMM_TILE), GEMM_TILE)], sem))
        return carry
    lax.fori_loop(gap_ref[2 * N_EXPERTS], xs_hbm.shape[0] // GEMM_TILE, unused_tile, 0)


def _dispatch_kernel(loc_ref, n_ref, dst_ref, tot_ref, gap_ref, hb_ref, slot_ref, xs_hbm, buf, zeros, sems):
    c = pl.program_id(0)
    nc = pl.num_programs(0)
    par = c % 2
    start = lambda cp: cp.start()
    wait = lambda cp: cp.wait()

    @pl.when(c == 0)
    def _():
        zeros[...] = jnp.zeros(zeros.shape, BF16)
        _zero_unused_rows(gap_ref, xs_hbm, zeros, sems.at[2], start)

    @pl.when(c >= 2)
    def _():
        _wait_chunk(tot_ref, c - 2, buf.at[par], xs_hbm, sems.at[par], True)

    rows = lax.broadcasted_iota(jnp.int32, (CHUNK_ROWS, CHUNK), 0)
    hit = rows == slot_ref[0, 0:1, :]
    for k in range(1, TOP_K):
        hit = jnp.logical_or(hit, rows == slot_ref[0, k:k + 1, :])
    onehot = jnp.where(hit, 1.0, 0.0).astype(BF16)
    buf[par] = _dot(onehot, hb_ref[...]).astype(BF16)
    _for_each_segment(loc_ref, n_ref, dst_ref, c, buf.at[par], xs_hbm, sems.at[par], True, start)

    @pl.when(c == nc - 1)
    def _():
        _zero_unused_rows(gap_ref, xs_hbm, zeros, sems.at[2], wait)

        @pl.when(c >= 1)
        def _():
            _wait_chunk(tot_ref, c - 1, buf.at[1 - par], xs_hbm, sems.at[1 - par], True)
        _wait_chunk(tot_ref, c, buf.at[par], xs_hbm, sems.at[par], True)


def _dispatch(hb, slots, seg_loc, seg_n, seg_dst, seg_tot, gap, n_rows):
    t, d = hb.shape
    nc = t // CHUNK
    return pl.pallas_call(
        _dispatch_kernel,
        grid_spec=pltpu.PrefetchScalarGridSpec(
            num_scalar_prefetch=5,
            grid=(nc,),
            in_specs=[
                pl.BlockSpec((CHUNK, d), lambda c, *_: (c, 0)),
                pl.BlockSpec((1, SUBLANES, CHUNK), lambda c, *_: (c, 0, 0)),
            ],
            out_specs=pl.BlockSpec(memory_space=pl.ANY),
            scratch_shapes=[pltpu.VMEM((2, CHUNK_ROWS, d), BF16), pltpu.VMEM((GEMM_TILE, d), BF16),
                            pltpu.SemaphoreType.DMA((3,))],
        ),
        out_shape=jax.ShapeDtypeStruct((n_rows, d), BF16),
        compiler_params=pltpu.CompilerParams(dimension_semantics=("arbitrary",), vmem_limit_bytes=VMEM_LIMIT),
        name="moe_dispatch",
    )(seg_loc, seg_n, seg_dst, seg_tot, gap, hb, slots)


def _expert_kernel(layer, te_ref, valid_ref, used_ref, buf_ref, next_ref, x_ref, wgu_hbm, bgu_ref, wd_hbm, bd_ref,
                   o_ref, wgu_f32, wd_f32, wgu_bf, wd_bf, sems):
    i = pl.program_id(0)
    prev = te_ref[jnp.maximum(i - 1, 0)]
    buf = buf_ref[i]

    def weight_copies(expert, b):
        e = layer * N_EXPERTS + expert
        return (pltpu.make_async_copy(wgu_hbm.at[e], wgu_f32.at[b], sems.at[b]),
                pltpu.make_async_copy(wd_hbm.at[e], wd_f32.at[b], sems.at[b]))

    @pl.when(i == 0)
    def _():
        for cp in weight_copies(te_ref[0], buf):
            cp.start()

    @pl.when(jnp.logical_or(i == 0, te_ref[i] != prev))
    def _():
        for cp in weight_copies(te_ref[i], buf):
            cp.wait()

        @pl.when(next_ref[i] >= 0)
        def _():
            for cp in weight_copies(next_ref[i], 1 - buf):
                cp.start()
        wgu_bf[...] = wgu_f32[buf].astype(BF16)
        wd_bf[...] = wd_f32[buf].astype(BF16)

    def expert_mlp(rows):
        gu = _dot(x_ref[rows, :], wgu_bf[...]) + bgu_ref[0]
        gate = jnp.minimum(gu[:, :D_FF], SWIGLU_LIMIT)
        up = jnp.clip(gu[:, D_FF:], -SWIGLU_LIMIT, SWIGLU_LIMIT)
        act = (up + 1.0) * (gate * (1.0 / (1.0 + jnp.exp(-SWIGLU_ALPHA * gate))))
        out = _dot(act.astype(BF16), wd_bf[...]) + bd_ref[0]
        o_ref[rows, :] = out.astype(BF16)

    n_sub = GEMM_TILE // GEMM_SUB

    @pl.when(valid_ref[i] > GEMM_TILE - GEMM_SUB)
    def _():
        expert_mlp(pl.ds(0, GEMM_TILE))

    @pl.when(valid_ref[i] <= GEMM_TILE - GEMM_SUB)
    def _():
        o_ref[pl.ds(0, GEMM_SUB), :] = jnp.zeros((GEMM_SUB, D_MODEL), BF16)
        for sub in range(1, n_sub):
            rows = pl.ds(sub * GEMM_SUB, GEMM_SUB)

            @pl.when(valid_ref[i] > GEMM_TILE - (sub + 1) * GEMM_SUB)
            def _():
                expert_mlp(rows)

            @pl.when(valid_ref[i] <= GEMM_TILE - (sub + 1) * GEMM_SUB)
            def _():
                o_ref[rows, :] = jnp.zeros((GEMM_SUB, D_MODEL), BF16)


def _experts(xs, tiles_info, layer, w_gate_up, b_gate_up, w_down, b_down):
    n_rows, d = xs.shape
    n_tiles = n_rows // GEMM_TILE
    n_stacked = w_gate_up.shape[0] * N_EXPERTS
    row_map = lambda i, te, tv, u, *_: (jnp.minimum(i, jnp.maximum(u[0] - 1, 0)), 0)
    exp_map = lambda i, te, *_: (layer * N_EXPERTS + te[i], 0, 0)
    return pl.pallas_call(
        functools.partial(_expert_kernel, layer),
        grid_spec=pltpu.PrefetchScalarGridSpec(
            num_scalar_prefetch=5,
            grid=(n_tiles,),
            in_specs=[
                pl.BlockSpec((GEMM_TILE, d), row_map),
                pl.BlockSpec(memory_space=pl.ANY),
                pl.BlockSpec((1, 1, 2 * D_FF), exp_map),
                pl.BlockSpec(memory_space=pl.ANY),
                pl.BlockSpec((1, 1, d), exp_map),
            ],
            out_specs=pl.BlockSpec((GEMM_TILE, d), lambda i, *_: (i, 0)),
            scratch_shapes=[pltpu.VMEM((2, d, 2 * D_FF), F32), pltpu.VMEM((2, D_FF, d), F32),
                            pltpu.VMEM((d, 2 * D_FF), BF16), pltpu.VMEM((D_FF, d), BF16),
                            pltpu.SemaphoreType.DMA((2,))],
        ),
        out_shape=jax.ShapeDtypeStruct((n_rows, d), BF16),
        compiler_params=pltpu.CompilerParams(dimension_semantics=("arbitrary",), vmem_limit_bytes=VMEM_LIMIT),
        name="moe_experts",
    )(*tiles_info, xs, w_gate_up.reshape(n_stacked, d, 2 * D_FF),
      b_gate_up.reshape(n_stacked, 1, 2 * D_FF), w_down.reshape(n_stacked, D_FF, d), b_down.reshape(n_stacked, 1, d))


def _combine_kernel(loc_ref, n_ref, dst_ref, tot_ref, x_ref, slot_t_ref, gate_t_ref, ys_hbm, o_ref, buf, sems):
    c = pl.program_id(0)
    nc = pl.num_programs(0)
    par = c % 2
    start = lambda cp: cp.start()

    @pl.when(c == 0)
    def _():
        buf[...] = jnp.zeros(buf.shape, BF16)
        _for_each_segment(loc_ref, n_ref, dst_ref, c, buf.at[0], ys_hbm, sems.at[0], False, start)

    @pl.when(c + 1 < nc)
    def _():
        _for_each_segment(loc_ref, n_ref, dst_ref, c + 1, buf.at[1 - par], ys_hbm, sems.at[1 - par], False, start)

    _wait_chunk(tot_ref, c, buf.at[par], ys_hbm, sems.at[par], False)

    cols = lax.broadcasted_iota(jnp.int32, (CHUNK, CHUNK_ROWS), 1)
    st = slot_t_ref[0]
    gt = gate_t_ref[0]
    back = jnp.zeros((CHUNK, CHUNK_ROWS), F32)
    for k in range(TOP_K):
        back = jnp.where(cols == st[:, k:k + 1], gt[:, k:k + 1], back)
    back = back.astype(BF16)
    o_ref[...] = x_ref[...] + _dot(back, buf[par])


def _combine(x2d, ys, slots_t, gates_t, seg_loc, seg_n, seg_dst, seg_tot):
    t, d = x2d.shape
    nc = t // CHUNK
    return pl.pallas_call(
        _combine_kernel,
        grid_spec=pltpu.PrefetchScalarGridSpec(
            num_scalar_prefetch=4,
            grid=(nc,),
            in_specs=[
                pl.BlockSpec((CHUNK, d), lambda c, *_: (c, 0)),
                pl.BlockSpec((1, CHUNK, TOP_K), lambda c, *_: (c, 0, 0)),
                pl.BlockSpec((1, CHUNK, TOP_K), lambda c, *_: (c, 0, 0)),
                pl.BlockSpec(memory_space=pl.ANY),
            ],
            out_specs=pl.BlockSpec((CHUNK, d), lambda c, *_: (c, 0)),
            scratch_shapes=[pltpu.VMEM((2, CHUNK_ROWS, d), BF16), pltpu.SemaphoreType.DMA((2,))],
        ),
        out_shape=jax.ShapeDtypeStruct((t, d), F32),
        compiler_params=pltpu.CompilerParams(dimension_semantics=("arbitrary",), vmem_limit_bytes=VMEM_LIMIT),
        name="moe_combine",
    )(seg_loc, seg_n, seg_dst, seg_tot, x2d, slots_t, gates_t, ys)


def _moe_layer(x2d, routed, layer, w_gate_up, b_gate_up, w_down, b_down):
    t, d = x2d.shape
    nc = t // CHUNK
    n_tiles_max = (t * TOP_K + nc * N_EXPERTS * (SEG_ALIGN - 1)) // GEMM_TILE + N_EXPERTS
    hb, slots, gates, cnt = routed
    seg_loc, seg_n, seg_dst, seg_tot, tiles_info, gap = _segment_tables(cnt, n_tiles_max)
    xs = _dispatch(hb, slots, seg_loc, seg_n, seg_dst, seg_tot, gap, n_tiles_max * GEMM_TILE)
    ys = _experts(xs, tiles_info, layer, w_gate_up, b_gate_up, w_down, b_down)
    slots_t = jnp.swapaxes(slots[:, :TOP_K, :], 1, 2)
    gates_t = jnp.swapaxes(gates[:, :TOP_K, :], 1, 2)
    return _combine(x2d, ys, slots_t, gates_t, seg_loc, seg_n, seg_dst, seg_tot)


def _qkv_kernel(x_ref, g_ref, w_ref, qg_ref, kg_ref, q_ref, k_ref, v_ref):
    d = x_ref.shape[1]
    h = _rmsnorm(x_ref[...], g_ref[...]).astype(BF16)
    qkv = _dot(h, w_ref[...])
    width = 2 * LANES
    li = lax.broadcasted_iota(jnp.int32, (width, width), 0) // HEAD_DIM
    lj = lax.broadcasted_iota(jnp.int32, (width, width), 1) // HEAD_DIM
    same_head = jnp.where(li == lj, 1.0, 0.0).astype(BF16)

    def head_norm(v, gain, out_scale):
        parts = []
        for j in range(d // width):
            vj = v[:, j * width:(j + 1) * width]
            ss = _dot((vj * vj).astype(BF16), same_head)
            parts.append(vj * lax.rsqrt(ss * (1.0 / HEAD_DIM) + RMS_EPS))
        return (jnp.concatenate(parts, axis=-1) * gain * out_scale).astype(BF16)

    q_ref[...] = head_norm(qkv[:, :d], qg_ref[...], HEAD_DIM ** -0.5 * LOG2_E)
    k_ref[...] = head_norm(qkv[:, d:2 * d], kg_ref[...], 1.0)
    v_ref[...] = qkv[:, 2 * d:].astype(BF16)


def _qkv(x2d, g, w_qkv, q_gain, k_gain):
    t, d = x2d.shape
    tt = min(TOK_TILE, t)
    row = lambda i: (i, 0)
    fixed = lambda i: (0, 0)
    return pl.pallas_call(
        _qkv_kernel,
        grid=(t // tt,),
        in_specs=[
            pl.BlockSpec((tt, d), row),
            pl.BlockSpec((1, d), fixed),
            pl.BlockSpec((d, 3 * d), fixed),
            pl.BlockSpec((1, d), fixed),
            pl.BlockSpec((1, d), fixed),
        ],
        out_specs=[pl.BlockSpec((tt, d), row)] * 3,
        out_shape=[jax.ShapeDtypeStruct((t, d), BF16)] * 3,
        compiler_params=pltpu.CompilerParams(dimension_semantics=("arbitrary",), vmem_limit_bytes=VMEM_LIMIT),
        name="attn_qkv",
    )(x2d, g.reshape(1, d), w_qkv.astype(BF16), jnp.tile(q_gain, N_HEADS).reshape(1, d),
      jnp.tile(k_gain, N_HEADS).reshape(1, d))


def _attn_kernel(q_ref, k_ref, v_ref, o_ref, acc_ref, carry_ref):
    qi = pl.program_id(2)
    tq, kb = ATT_Q, ATT_K
    q = q_ref[0]
    head_of_lane = lax.broadcasted_iota(jnp.int32, (tq, LANES), 1) // HEAD_DIM
    q_heads = [jnp.where(head_of_lane == hd, q, jnp.zeros_like(q)) for hd in range(2)]
    r2 = lax.broadcasted_iota(jnp.int32, (2 * kb, 2 * kb), 0)
    c2 = lax.broadcasted_iota(jnp.int32, (2 * kb, 2 * kb), 1)
    suffix2 = jnp.where(r2 >= c2, 1.0, 0.0).astype(BF16)

    acc_ref[...] = jnp.zeros(acc_ref.shape, F32)
    carry_ref[...] = jnp.zeros(carry_ref.shape, F32)

    def key_group(j0, diag_offsets, r0, r1):
        nb = len(diag_offsets)
        rows = r1 - r0
        start = pl.multiple_of(j0 * kb, kb)
        kk = k_ref[0, pl.ds(start, nb * kb), :]
        vv = v_ref[0, pl.ds(start, nb * kb), :]
        row = lax.broadcasted_iota(jnp.int32, (rows, kb), 0) + r0
        key = lax.broadcasted_iota(jnp.int32, (rows, kb), 1)
        z_all = [_dot_nt(q_heads[hd][r0:r1, :], kk) for hd in range(2)]
        carry = [carry_ref[hd, r0:r1, :] for hd in range(2)]
        probs = [[None] * nb for _ in range(2)]
        for pair in reversed(range(nb // 2)):
            blocks = (2 * pair, 2 * pair + 1)
            causal = [None if diag_offsets[b] is None else key + diag_offsets[b] < row for b in blocks]
            for hd in range(2):
                zs, log_keeps = [], []
                for j, b in enumerate(blocks):
                    z = z_all[hd][:, b * kb:(b + 1) * kb]
                    neg = -z
                    log_keep = jnp.minimum(neg, 0.0) - jnp.log(1.0 + jnp.exp2(jnp.minimum(z, neg))) * LOG2_E
                    if causal[j] is not None:
                        log_keep = jnp.where(causal[j], log_keep, 0.0)
                    zs.append(z)
                    log_keeps.append(log_keep.astype(BF16))
                sums = _dot(jnp.concatenate(log_keeps, axis=1), suffix2)
                for j, b in enumerate(blocks):
                    s_in = sums[:, j * kb:(j + 1) * kb]
                    a = jnp.exp2(zs[j] + s_in + carry[hd])
                    if causal[j] is not None:
                        a = jnp.where(causal[j], a, 0.0)
                    probs[hd][b] = a.astype(BF16)
                carry[hd] = carry[hd] + jnp.broadcast_to(sums[:, 0:1], (rows, kb))
        for hd in range(2):
            carry_ref[hd, r0:r1, :] = carry[hd]
            acc_ref[hd, r0:r1, :] += _dot(jnp.concatenate(probs[hd], axis=1), vv)

    blocks_per_q = tq // kb
    for g in reversed(range(blocks_per_q // ATT_DIAG_GROUP)):
        offs = [(g * ATT_DIAG_GROUP + b) * kb for b in range(ATT_DIAG_GROUP)]
        key_group(qi * blocks_per_q + g * ATT_DIAG_GROUP, offs, offs[0], tq)

    def body(s, c):
        key_group(qi * blocks_per_q - (s + 1) * ATT_GROUP, [None] * ATT_GROUP, 0, tq)
        return c
    lax.fori_loop(0, qi * (blocks_per_q // ATT_GROUP), body, 0)
    o_ref[0] = jnp.where(head_of_lane == 0, acc_ref[0], acc_ref[1]).astype(BF16)


def _attention(q, k, v, b, s):
    d = q.shape[-1]
    q3, k3, v3 = (a.reshape(b, s, d) for a in (q, k, v))
    pairs = d // LANES
    o = pl.pallas_call(
        _attn_kernel,
        grid=(b, pairs, s // ATT_Q),
        in_specs=[
            pl.BlockSpec((1, ATT_Q, LANES), lambda bi, hp, i: (bi, i, hp)),
            pl.BlockSpec((1, s, LANES), lambda bi, hp, i: (bi, 0, hp)),
            pl.BlockSpec((1, s, LANES), lambda bi, hp, i: (bi, 0, hp)),
        ],
        out_specs=pl.BlockSpec((1, ATT_Q, LANES), lambda bi, hp, i: (bi, i, hp)),
        out_shape=jax.ShapeDtypeStruct((b, s, d), BF16),
        scratch_shapes=[pltpu.VMEM((2, ATT_Q, LANES), F32), pltpu.VMEM((2, ATT_Q, LANES), F32)],
        compiler_params=pltpu.CompilerParams(
            dimension_semantics=("arbitrary", "arbitrary", "arbitrary"), vmem_limit_bytes=VMEM_LIMIT),
        name="attn_core",
    )(q3, k3, v3)
    return o.reshape(b * s, d)


def _proj_route_kernel(x_ref, o_ref, w_ref, rg_ref, wt_hi_ref, wt_lo_ref, rb_ref,
                       y_ref, hb_ref, slot_ref, gate_ref, cnt_ref):
    x2 = x_ref[...] + _dot(o_ref[...], w_ref[...])
    y_ref[...] = x2
    _route_rows(x2, rg_ref, wt_hi_ref, wt_lo_ref, rb_ref, hb_ref, slot_ref, gate_ref, cnt_ref)


def _proj_route(x2d, o, w_o, ffn_g, router_w, router_b):
    t, d = x2d.shape
    route_in, route_out = _route_specs(d, lambda c: c)
    outs = pl.pallas_call(
        _proj_route_kernel,
        grid=(t // ROUTE_TILE,),
        in_specs=[
            pl.BlockSpec((ROUTE_TILE, d), lambda c: (c, 0)),
            pl.BlockSpec((ROUTE_TILE, d), lambda c: (c, 0)),
            pl.BlockSpec((d, d), lambda c: (0, 0)),
        ] + route_in,
        out_specs=[pl.BlockSpec((ROUTE_TILE, d), lambda c: (c, 0))] + route_out,
        out_shape=[jax.ShapeDtypeStruct((t, d), F32)] + _route_shapes(t, d),
        compiler_params=pltpu.CompilerParams(dimension_semantics=("arbitrary",), vmem_limit_bytes=VMEM_LIMIT),
        name="attn_proj_route",
    )(x2d, o, w_o.astype(BF16), *_route_args(ffn_g, router_w, router_b))
    return outs[0], outs[1:]


def _attention_route_layer(x2d, b, s, g, w_qkv, q_gain, k_gain, w_o, ffn_g, router_w, router_b):
    q, k, v = _qkv(x2d, g, w_qkv, q_gain, k_gain)
    o = _attention(q, k, v, b, s)
    return _proj_route(x2d, o, w_o, ffn_g, router_w, router_b)


def kernel(x, mix_norm, pool_w, pool_scale, w_qkv, q_norm, k_norm, w_o, ffn_norm, router_w, router_b,
           w_gate_up, b_gate_up, w_down, b_down):
    b, s, d = x.shape
    assert d == D_MODEL and s % ATT_Q == 0 and s % ROUTE_TILE == 0 and ROUTE_TILE % CHUNK == 0
    depth = mix_norm.shape[0]
    for i in range(depth):
        j = i // 2
        if i % 2 == 0:
            x2d, routed = _pool_route_layer(x, mix_norm[i], pool_w[j], pool_scale[j],
                                            ffn_norm[i], router_w[i], router_b[i])
        else:
            x2d, routed = _attention_route_layer(x.reshape(b * s, d), b, s, mix_norm[i], w_qkv[j], q_norm[j],
                                                 k_norm[j], w_o[j], ffn_norm[i], router_w[i], router_b[i])
        x2d = _moe_layer(x2d, routed, i, w_gate_up, b_gate_up, w_down, b_down)
        x = x2d.reshape(b, s, d)
    return x
```

```python
import functools

import jax
import jax.numpy as jnp
from jax import lax
from jax.experimental import pallas as pl
from jax.experimental.pallas import tpu as pltpu

D_MODEL = 1024
POOL_WINDOWS = (2, 4, 8, 16)
POOL_GROUP_DIM = D_MODEL // len(POOL_WINDOWS)
POOL_HALO = 16
HEAD_DIM = 64
N_HEADS = D_MODEL // HEAD_DIM
N_EXPERTS = 32
TOP_K = 4
D_FF = D_MODEL
SWIGLU_LIMIT = 7.0
SWIGLU_ALPHA = 1.702
RMS_EPS = 1e-6
LOG2_E = 1.4426950408889634

LANES = 128
SUBLANES = 8
VMEM_LIMIT = 56 * 1024 * 1024

CHUNK = 512
SEG_ALIGN = 2 * SUBLANES
CHUNK_ROWS = CHUNK * TOP_K + N_EXPERTS * SEG_ALIGN
GEMM_TILE = 512
GEMM_SUB = 256
TOK_TILE = 512
ROUTE_TILE = 1024
ATT_Q = 2048
ATT_K = LANES
ATT_GROUP = 4
ATT_DIAG_GROUP = 2

F32 = jnp.float32
BF16 = jnp.bfloat16


def _dot(a, b):
    return jnp.dot(a, b, preferred_element_type=F32)


def _dot_nt(a, b):
    return lax.dot_general(a, b, (((1,), (1,)), ((), ())), preferred_element_type=F32)


def _split_bf16(v):
    hi = v.astype(BF16)
    lo = (v - hi.astype(F32)).astype(BF16)
    return hi, lo


def _rmsnorm(v, g):
    return v * lax.rsqrt(jnp.mean(v * v, axis=-1, keepdims=True) + RMS_EPS) * g


def _pool_route_kernel(x_ref, halo_ref, g_ref, w_ref, scale_ref, rg_ref, wt_hi_ref, wt_lo_ref, rb_ref,
                       o_ref, hb_ref, slot_ref, gate_ref, cnt_ref):
    x1 = _pool_tile(x_ref[0], halo_ref[0], pl.program_id(1), g_ref, w_ref, scale_ref)
    o_ref[0] = x1
    _route_rows(x1, rg_ref, wt_hi_ref, wt_lo_ref, rb_ref, hb_ref, slot_ref, gate_ref, cnt_ref)


def _pool_tile(x, halo, i, g_ref, w_ref, scale_ref):
    g = g_ref[...]
    h = _rmsnorm(x, g)
    hh = _rmsnorm(halo, g)
    hh = jnp.where(i > 0, hh, 0.0)
    a = jnp.concatenate([hh, h], axis=0)
    ts = x.shape[0]
    pos = i * ts + lax.broadcasted_iota(jnp.int32, (ts, 1), 0)
    outs = []
    for gi, w in enumerate(POOL_WINDOWS):
        lo, hi = gi * POOL_GROUP_DIM, (gi + 1) * POOL_GROUP_DIM
        s = a[:, lo:hi]
        span = 1
        while span < w:
            s = s + pltpu.roll(s, span, axis=0)
            span *= 2
        cnt = jnp.minimum(pos + 1, w).astype(F32)
        pooled = s[POOL_HALO:, :] / cnt - h[:, lo:hi]
        outs.append(_dot(pooled.astype(BF16), w_ref[gi]))
    return x + jnp.concatenate(outs, axis=-1) * scale_ref[...]


def _route_specs(d, step_of):
    fixed2 = lambda *_: (0, 0)
    per_step = ROUTE_TILE // CHUNK
    in_specs = [
        pl.BlockSpec((1, d), fixed2),
        pl.BlockSpec((N_EXPERTS, d), fixed2),
        pl.BlockSpec((N_EXPERTS, d), fixed2),
        pl.BlockSpec((N_EXPERTS, 1), fixed2),
    ]
    out_specs = [
        pl.BlockSpec((ROUTE_TILE, d), lambda *ids: (step_of(*ids), 0)),
        pl.BlockSpec((per_step, SUBLANES, CHUNK), lambda *ids: (step_of(*ids), 0, 0)),
        pl.BlockSpec((per_step, SUBLANES, CHUNK), lambda *ids: (step_of(*ids), 0, 0)),
        pl.BlockSpec((per_step, N_EXPERTS, LANES), lambda *ids: (step_of(*ids), 0, 0)),
    ]
    return in_specs, out_specs


def _route_args(g, router_w, router_b):
    wt_hi, wt_lo = _split_bf16(router_w.T)
    return g.reshape(1, -1), wt_hi, wt_lo, router_b.reshape(N_EXPERTS, 1)


def _route_shapes(t, d):
    nc = t // CHUNK
    return [
        jax.ShapeDtypeStruct((t, d), BF16),
        jax.ShapeDtypeStruct((nc, SUBLANES, CHUNK), jnp.int32),
        jax.ShapeDtypeStruct((nc, SUBLANES, CHUNK), F32),
        jax.ShapeDtypeStruct((nc, N_EXPERTS, LANES), jnp.int32),
    ]


def _pool_route_layer(x, g, w, scale, ffn_g, router_w, router_b):
    b, s, d = x.shape
    ts = ROUTE_TILE
    tiles_per_seq = s // ts
    blocks_per_tile = ts // POOL_HALO
    route_in, route_out = _route_specs(d, lambda bi, i: bi * tiles_per_seq + i)
    outs = pl.pallas_call(
        _pool_route_kernel,
        grid=(b, tiles_per_seq),
        in_specs=[
            pl.BlockSpec((1, ts, d), lambda bi, i: (bi, i, 0)),
            pl.BlockSpec((1, POOL_HALO, d), lambda bi, i: (bi, jnp.maximum(i * blocks_per_tile - 1, 0), 0)),
            pl.BlockSpec((1, d), lambda bi, i: (0, 0)),
            pl.BlockSpec((len(POOL_WINDOWS), POOL_GROUP_DIM, POOL_GROUP_DIM), lambda bi, i: (0, 0, 0)),
            pl.BlockSpec((1, d), lambda bi, i: (0, 0)),
        ] + route_in,
        out_specs=[pl.BlockSpec((1, ts, d), lambda bi, i: (bi, i, 0))] + route_out,
        out_shape=[jax.ShapeDtypeStruct(x.shape, F32)] + _route_shapes(b * s, d),
        compiler_params=pltpu.CompilerParams(
            dimension_semantics=("arbitrary", "arbitrary"), vmem_limit_bytes=VMEM_LIMIT),
        name="pool_route",
    )(x, x, g.reshape(1, d), w.astype(BF16), scale.reshape(1, d), *_route_args(ffn_g, router_w, router_b))
    return outs[0].reshape(b * s, d), outs[1:]


def _route_rows(x, g_ref, wt_hi_ref, wt_lo_ref, b_ref, hb_ref, slot_ref, gate_ref, cnt_ref):
    t = x.shape[0]
    h = _rmsnorm(x, g_ref[...])
    h_hi, h_lo = _split_bf16(h)
    hb_ref[...] = h_hi
    logits = (_dot_nt(wt_hi_ref[...], h_hi) + _dot_nt(wt_lo_ref[...], h_hi)
              + _dot_nt(wt_hi_ref[...], h_lo) + b_ref[...])
    eio = lax.broadcasted_iota(jnp.int32, (N_EXPERTS, t), 0).astype(F32)
    vals, idxs = [], []
    l = logits
    for _ in range(TOP_K):
        m = jnp.max(l, axis=0, keepdims=True)
        idx = jnp.min(jnp.where(l == m, eio, float(N_EXPERTS)), axis=0, keepdims=True)
        vals.append(m)
        idxs.append(idx)
        l = jnp.where(eio == idx, -jnp.inf, l)
    es = [jnp.exp(v - vals[0]) for v in vals]
    denom = es[0] + es[1] + es[2] + es[3]
    sel = jnp.zeros((N_EXPERTS, t), F32)
    for idx in idxs:
        sel = sel + jnp.where(eio == idx, 1.0, 0.0)
    ti = lax.broadcasted_iota(jnp.int32, (t, t), 0)
    tj = lax.broadcasted_iota(jnp.int32, (t, t), 1)
    upper = jnp.where(jnp.logical_and(ti <= tj, ti // CHUNK == tj // CHUNK), 1.0, 0.0).astype(BF16)
    rank = _dot(sel.astype(BF16), upper)
    chunk_of_lane = lax.broadcasted_iota(jnp.int32, (N_EXPERTS, t), 1) // CHUNK
    n = jnp.zeros((N_EXPERTS, t), F32)
    for ci in range(t // CHUNK):
        last = (ci + 1) * CHUNK - 1
        n = jnp.where(chunk_of_lane == ci, rank[:, last:last + 1], n)
    n_units = jnp.floor((n + (SEG_ALIGN - 1)) * (1.0 / SEG_ALIGN))
    strict_lower = jnp.where(lax.broadcasted_iota(jnp.int32, (N_EXPERTS, N_EXPERTS), 1)
                             < lax.broadcasted_iota(jnp.int32, (N_EXPERTS, N_EXPERTS), 0), 1.0, 0.0).astype(BF16)
    off = _dot(strict_lower, n_units.astype(BF16)) * SEG_ALIGN
    dest = off + rank - 1.0
    slots = [jnp.sum(jnp.where(eio == idxs[k], dest, 0.0), axis=0, keepdims=True).astype(jnp.int32)
             for k in range(TOP_K)]
    for ci in range(t // CHUNK):
        lanes = slice(ci * CHUNK, (ci + 1) * CHUNK)
        for k in range(TOP_K):
            slot_ref[ci, k:k + 1, :] = slots[k][:, lanes]
            gate_ref[ci, k:k + 1, :] = (es[k] / denom)[:, lanes]
        slot_ref[ci, TOP_K:, :] = jnp.zeros((SUBLANES - TOP_K, CHUNK), jnp.int32)
        gate_ref[ci, TOP_K:, :] = jnp.zeros((SUBLANES - TOP_K, CHUNK), F32)
        cnt_ref[ci] = (n_units[:, ci * CHUNK:ci * CHUNK + LANES] * SEG_ALIGN).astype(jnp.int32)


def _segment_tables(cnt, n_tiles_max):
    seg_n = cnt[:, :, 0]
    seg_loc = jnp.cumsum(seg_n, axis=1) - seg_n
    tot = jnp.sum(seg_n, axis=0)
    tiles = (tot + GEMM_TILE - 1) // GEMM_TILE
    tile_end = jnp.cumsum(tiles)
    tile_start = tile_end - tiles
    front_pad = tiles * GEMM_TILE - tot
    seg_dst = (tile_start * GEMM_TILE + front_pad)[None, :] + jnp.cumsum(seg_n, axis=0) - seg_n
    used = tile_end[-1]
    tile_ids = jnp.arange(n_tiles_max, dtype=jnp.int32)
    tile_expert = jnp.sum(tile_ids[:, None] >= tile_end[None, :], axis=1).astype(jnp.int32)
    last_expert = jnp.sum(jnp.maximum(used - 1, 0) >= tile_end).astype(jnp.int32)
    tile_expert = jnp.where(tile_ids < used, tile_expert, last_expert)
    owner = tile_expert[:, None] == jnp.arange(N_EXPERTS, dtype=jnp.int32)[None, :]
    pad_here = jnp.sum(jnp.where(owner, jnp.where(tile_ids[:, None] == tile_start[None, :], front_pad[None, :], 0), 0),
                       axis=1)
    tile_valid = jnp.where(tile_ids < used, GEMM_TILE - pad_here, 0)
    experts = jnp.arange(N_EXPERTS, dtype=jnp.int32)
    ordinal = jnp.cumsum((tiles > 0).astype(jnp.int32)) - 1
    later = jnp.where(jnp.logical_and(experts[None, :] > experts[:, None], (tiles > 0)[None, :]), experts[None, :],
                      N_EXPERTS)
    next_expert = jnp.min(later, axis=1)
    next_expert = jnp.where(next_expert < N_EXPERTS, next_expert, -1)
    tile_buf = jnp.sum(jnp.where(owner, (ordinal % 2)[None, :], 0), axis=1)
    tile_next = jnp.sum(jnp.where(owner, next_expert[None, :], 0), axis=1)
    i32 = lambda v: v.astype(jnp.int32).reshape(-1)
    gap = jnp.concatenate([tile_start * GEMM_TILE, front_pad, used[None]])
    seg_tot = jnp.sum(seg_n, axis=1)
    tiles_info = (tile_expert, i32(tile_valid), i32(used), i32(tile_buf), i32(tile_next))
    return i32(seg_loc), i32(seg_n), i32(seg_dst), i32(seg_tot), tiles_info, i32(gap)


def _segment_copy(loc_ref, n_ref, dst_ref, c, e, local_buf, sorted_hbm, sem, to_sorted):
    j = c * N_EXPERTS + e
    n = pl.multiple_of(n_ref[j], SEG_ALIGN)
    local = local_buf.at[pl.ds(pl.multiple_of(loc_ref[j], SEG_ALIGN), n)]
    remote = sorted_hbm.at[pl.ds(pl.multiple_of(dst_ref[j], SEG_ALIGN), n)]
    return n, (pltpu.make_async_copy(local, remote, sem) if to_sorted
               else pltpu.make_async_copy(remote, local, sem))


def _start_segments(loc_ref, n_ref, dst_ref, c, local_buf, sorted_hbm, sem, to_sorted):
    def body(pair, carry):
        for priority in range(2):
            n, cp = _segment_copy(loc_ref, n_ref, dst_ref, c, 2 * pair + priority, local_buf, sorted_hbm, sem,
                                  to_sorted)

            @pl.when(n > 0)
            def _():
                cp.start(priority=priority)
        return carry
    lax.fori_loop(0, N_EXPERTS // 2, body, 0)


def _wait_chunk(tot_ref, c, local_buf, sorted_hbm, sem, to_sorted):
    n = pl.multiple_of(tot_ref[c], SEG_ALIGN)
    local = local_buf.at[pl.ds(0, n)]
    remote = sorted_hbm.at[pl.ds(0, n)]
    (pltpu.make_async_copy(local, remote, sem) if to_sorted else pltpu.make_async_copy(remote, local, sem)).wait()


def _zero_unused_rows(gap_ref, xs_hbm, zeros, sem, action):
    def expert_gap(e, carry):
        n = pl.multiple_of(gap_ref[N_EXPERTS + e], SEG_ALIGN)

        @pl.when(n > 0)
        def _():
            action(pltpu.make_async_copy(
                zeros.at[pl.ds(0, n)], xs_hbm.at[pl.ds(pl.multiple_of(gap_ref[e], SEG_ALIGN), n)], sem))
        return carry
    lax.fori_loop(0, N_EXPERTS, expert_gap, 0)

    def unused_tile(i, carry):
        action(pltpu.make_async_copy(zeros, xs_hbm.at[pl.ds(pl.multiple_of(i * GEMM_TILE, GEMM_TILE), GEMM_TILE)], sem))
        return carry
    lax.fori_loop(gap_ref[2 * N_EXPERTS], xs_hbm.shape[0] // GEMM_TILE, unused_tile, 0)


def _dispatch_kernel(loc_ref, n_ref, dst_ref, tot_ref, gap_ref, hb_ref, slot_ref, xs_hbm, buf, zeros, sems):
    c = pl.program_id(0)
    nc = pl.num_programs(0)
    par = c % 2
    start = lambda cp: cp.start()
    wait = lambda cp: cp.wait()

    @pl.when(c == 0)
    def _():
        zeros[...] = jnp.zeros(zeros.shape, BF16)
        _zero_unused_rows(gap_ref, xs_hbm, zeros, sems.at[2], start)

    @pl.when(c >= 2)
    def _():
        _wait_chunk(tot_ref, c - 2, buf.at[par], xs_hbm, sems.at[par], True)

    rows = lax.broadcasted_iota(jnp.int32, (CHUNK_ROWS, CHUNK), 0)
    hit = rows == slot_ref[0, 0:1, :]
    for k in range(1, TOP_K):
        hit = jnp.logical_or(hit, rows == slot_ref[0, k:k + 1, :])
    onehot = jnp.where(hit, 1.0, 0.0).astype(BF16)
    buf[par] = _dot(onehot, hb_ref[...]).astype(BF16)
    _start_segments(loc_ref, n_ref, dst_ref, c, buf.at[par], xs_hbm, sems.at[par], True)

    @pl.when(c == nc - 1)
    def _():
        _zero_unused_rows(gap_ref, xs_hbm, zeros, sems.at[2], wait)

        @pl.when(c >= 1)
        def _():
            _wait_chunk(tot_ref, c - 1, buf.at[1 - par], xs_hbm, sems.at[1 - par], True)
        _wait_chunk(tot_ref, c, buf.at[par], xs_hbm, sems.at[par], True)


def _dispatch(hb, slots, seg_loc, seg_n, seg_dst, seg_tot, gap, n_rows):
    t, d = hb.shape
    nc = t // CHUNK
    return pl.pallas_call(
        _dispatch_kernel,
        grid_spec=pltpu.PrefetchScalarGridSpec(
            num_scalar_prefetch=5,
            grid=(nc,),
            in_specs=[
                pl.BlockSpec((CHUNK, d), lambda c, *_: (c, 0)),
                pl.BlockSpec((1, SUBLANES, CHUNK), lambda c, *_: (c, 0, 0)),
            ],
            out_specs=pl.BlockSpec(memory_space=pl.ANY),
            scratch_shapes=[pltpu.VMEM((2, CHUNK_ROWS, d), BF16), pltpu.VMEM((GEMM_TILE, d), BF16),
                            pltpu.SemaphoreType.DMA((3,))],
        ),
        out_shape=jax.ShapeDtypeStruct((n_rows, d), BF16),
        compiler_params=pltpu.CompilerParams(dimension_semantics=("arbitrary",), vmem_limit_bytes=VMEM_LIMIT),
        name="moe_dispatch",
    )(seg_loc, seg_n, seg_dst, seg_tot, gap, hb, slots)


def _expert_kernel(layer, te_ref, valid_ref, used_ref, buf_ref, next_ref, x_ref, wgu_hbm, bgu_ref, wd_hbm, bd_ref,
                   o_ref, wgu_f32, wd_f32, wgu_bf, wd_bf, sems):
    i = pl.program_id(0)
    prev = te_ref[jnp.maximum(i - 1, 0)]
    buf = buf_ref[i]

    def weight_copies(expert, b):
        e = layer * N_EXPERTS + expert
        return (pltpu.make_async_copy(wgu_hbm.at[e], wgu_f32.at[b], sems.at[b]),
                pltpu.make_async_copy(wd_hbm.at[e], wd_f32.at[b], sems.at[b]))

    @pl.when(i == 0)
    def _():
        for cp in weight_copies(te_ref[0], buf):
            cp.start()

    @pl.when(jnp.logical_or(i == 0, te_ref[i] != prev))
    def _():
        for cp in weight_copies(te_ref[i], buf):
            cp.wait()

        @pl.when(next_ref[i] >= 0)
        def _():
            for cp in weight_copies(next_ref[i], 1 - buf):
                cp.start()
        wgu_bf[...] = wgu_f32[buf].astype(BF16)
        wd_bf[...] = wd_f32[buf].astype(BF16)

    def expert_mlp(rows):
        gu = _dot(x_ref[rows, :], wgu_bf[...]) + bgu_ref[0]
        gate = jnp.minimum(gu[:, :D_FF], SWIGLU_LIMIT)
        up = jnp.clip(gu[:, D_FF:], -SWIGLU_LIMIT, SWIGLU_LIMIT)
        act = (up + 1.0) * (gate * (1.0 / (1.0 + jnp.exp(-SWIGLU_ALPHA * gate))))
        out = _dot(act.astype(BF16), wd_bf[...]) + bd_ref[0]
        o_ref[rows, :] = out.astype(BF16)

    n_sub = GEMM_TILE // GEMM_SUB

    @pl.when(valid_ref[i] > GEMM_TILE - GEMM_SUB)
    def _():
        expert_mlp(pl.ds(0, GEMM_TILE))

    @pl.when(valid_ref[i] <= GEMM_TILE - GEMM_SUB)
    def _():
        o_ref[pl.ds(0, GEMM_SUB), :] = jnp.zeros((GEMM_SUB, D_MODEL), BF16)
        for sub in range(1, n_sub):
            rows = pl.ds(sub * GEMM_SUB, GEMM_SUB)

            @pl.when(valid_ref[i] > GEMM_TILE - (sub + 1) * GEMM_SUB)
            def _():
                expert_mlp(rows)

            @pl.when(valid_ref[i] <= GEMM_TILE - (sub + 1) * GEMM_SUB)
            def _():
                o_ref[rows, :] = jnp.zeros((GEMM_SUB, D_MODEL), BF16)


def _experts(xs, tiles_info, layer, w_gate_up, b_gate_up, w_down, b_down):
    n_rows, d = xs.shape
    n_tiles = n_rows // GEMM_TILE
    n_stacked = w_gate_up.shape[0] * N_EXPERTS
    row_map = lambda i, te, tv, u, *_: (jnp.minimum(i, jnp.maximum(u[0] - 1, 0)), 0)
    exp_map = lambda i, te, *_: (layer * N_EXPERTS + te[i], 0, 0)
    return pl.pallas_call(
        functools.partial(_expert_kernel, layer),
        grid_spec=pltpu.PrefetchScalarGridSpec(
            num_scalar_prefetch=5,
            grid=(n_tiles,),
            in_specs=[
                pl.BlockSpec((GEMM_TILE, d), row_map),
                pl.BlockSpec(memory_space=pl.ANY),
                pl.BlockSpec((1, 1, 2 * D_FF), exp_map),
                pl.BlockSpec(memory_space=pl.ANY),
                pl.BlockSpec((1, 1, d), exp_map),
            ],
            out_specs=pl.BlockSpec((GEMM_TILE, d), lambda i, *_: (i, 0)),
            scratch_shapes=[pltpu.VMEM((2, d, 2 * D_FF), F32), pltpu.VMEM((2, D_FF, d), F32),
                            pltpu.VMEM((d, 2 * D_FF), BF16), pltpu.VMEM((D_FF, d), BF16),
                            pltpu.SemaphoreType.DMA((2,))],
        ),
        out_shape=jax.ShapeDtypeStruct((n_rows, d), BF16),
        compiler_params=pltpu.CompilerParams(dimension_semantics=("arbitrary",), vmem_limit_bytes=VMEM_LIMIT),
        name="moe_experts",
    )(*tiles_info, xs, w_gate_up.reshape(n_stacked, d, 2 * D_FF),
      b_gate_up.reshape(n_stacked, 1, 2 * D_FF), w_down.reshape(n_stacked, D_FF, d), b_down.reshape(n_stacked, 1, d))


def _combine_kernel(loc_ref, n_ref, dst_ref, tot_ref, x_ref, slot_t_ref, gate_t_ref, ys_hbm, o_ref, buf, sems):
    c = pl.program_id(0)
    nc = pl.num_programs(0)
    par = c % 2

    @pl.when(c == 0)
    def _():
        buf[...] = jnp.zeros(buf.shape, BF16)
        _start_segments(loc_ref, n_ref, dst_ref, c, buf.at[0], ys_hbm, sems.at[0], False)

    @pl.when(c + 1 < nc)
    def _():
        _start_segments(loc_ref, n_ref, dst_ref, c + 1, buf.at[1 - par], ys_hbm, sems.at[1 - par], False)

    _wait_chunk(tot_ref, c, buf.at[par], ys_hbm, sems.at[par], False)

    cols = lax.broadcasted_iota(jnp.int32, (CHUNK, CHUNK_ROWS), 1)
    st = slot_t_ref[0]
    gt = gate_t_ref[0]
    back = jnp.zeros((CHUNK, CHUNK_ROWS), F32)
    for k in range(TOP_K):
        back = jnp.where(cols == st[:, k:k + 1], gt[:, k:k + 1], back)
    back = back.astype(BF16)
    o_ref[...] = x_ref[...] + _dot(back, buf[par])


def _combine(x2d, ys, slots_t, gates_t, seg_loc, seg_n, seg_dst, seg_tot):
    t, d = x2d.shape
    nc = t // CHUNK
    return pl.pallas_call(
        _combine_kernel,
        grid_spec=pltpu.PrefetchScalarGridSpec(
            num_scalar_prefetch=4,
            grid=(nc,),
            in_specs=[
                pl.BlockSpec((CHUNK, d), lambda c, *_: (c, 0)),
                pl.BlockSpec((1, CHUNK, TOP_K), lambda c, *_: (c, 0, 0)),
                pl.BlockSpec((1, CHUNK, TOP_K), lambda c, *_: (c, 0, 0)),
                pl.BlockSpec(memory_space=pl.ANY),
            ],
            out_specs=pl.BlockSpec((CHUNK, d), lambda c, *_: (c, 0)),
            scratch_shapes=[pltpu.VMEM((2, CHUNK_ROWS, d), BF16), pltpu.SemaphoreType.DMA((2,))],
        ),
        out_shape=jax.ShapeDtypeStruct((t, d), F32),
        compiler_params=pltpu.CompilerParams(dimension_semantics=("arbitrary",), vmem_limit_bytes=VMEM_LIMIT),
        name="moe_combine",
    )(seg_loc, seg_n, seg_dst, seg_tot, x2d, slots_t, gates_t, ys)


def _moe_layer(x2d, routed, layer, w_gate_up, b_gate_up, w_down, b_down):
    t, d = x2d.shape
    nc = t // CHUNK
    n_tiles_max = (t * TOP_K + nc * N_EXPERTS * (SEG_ALIGN - 1)) // GEMM_TILE + N_EXPERTS
    hb, slots, gates, cnt = routed
    seg_loc, seg_n, seg_dst, seg_tot, tiles_info, gap = _segment_tables(cnt, n_tiles_max)
    xs = _dispatch(hb, slots, seg_loc, seg_n, seg_dst, seg_tot, gap, n_tiles_max * GEMM_TILE)
    ys = _experts(xs, tiles_info, layer, w_gate_up, b_gate_up, w_down, b_down)
    slots_t = jnp.swapaxes(slots[:, :TOP_K, :], 1, 2)
    gates_t = jnp.swapaxes(gates[:, :TOP_K, :], 1, 2)
    return _combine(x2d, ys, slots_t, gates_t, seg_loc, seg_n, seg_dst, seg_tot)


def _qkv_kernel(x_ref, g_ref, w_ref, qg_ref, kg_ref, q_ref, k_ref, v_ref):
    d = x_ref.shape[1]
    h = _rmsnorm(x_ref[...], g_ref[...]).astype(BF16)
    qkv = _dot(h, w_ref[...])
    width = 2 * LANES
    li = lax.broadcasted_iota(jnp.int32, (width, width), 0) // HEAD_DIM
    lj = lax.broadcasted_iota(jnp.int32, (width, width), 1) // HEAD_DIM
    same_head = jnp.where(li == lj, 1.0, 0.0).astype(BF16)

    def head_norm(v, gain, out_scale):
        parts = []
        for j in range(d // width):
            vj = v[:, j * width:(j + 1) * width]
            ss = _dot((vj * vj).astype(BF16), same_head)
            parts.append(vj * lax.rsqrt(ss * (1.0 / HEAD_DIM) + RMS_EPS))
        return (jnp.concatenate(parts, axis=-1) * gain * out_scale).astype(BF16)

    q_ref[...] = head_norm(qkv[:, :d], qg_ref[...], HEAD_DIM ** -0.5 * LOG2_E)
    k_ref[...] = head_norm(qkv[:, d:2 * d], kg_ref[...], 1.0)
    v_ref[...] = qkv[:, 2 * d:].astype(BF16)


def _qkv(x2d, g, w_qkv, q_gain, k_gain):
    t, d = x2d.shape
    tt = min(TOK_TILE, t)
    row = lambda i: (i, 0)
    fixed = lambda i: (0, 0)
    return pl.pallas_call(
        _qkv_kernel,
        grid=(t // tt,),
        in_specs=[
            pl.BlockSpec((tt, d), row),
            pl.BlockSpec((1, d), fixed),
            pl.BlockSpec((d, 3 * d), fixed),
            pl.BlockSpec((1, d), fixed),
            pl.BlockSpec((1, d), fixed),
        ],
        out_specs=[pl.BlockSpec((tt, d), row)] * 3,
        out_shape=[jax.ShapeDtypeStruct((t, d), BF16)] * 3,
        compiler_params=pltpu.CompilerParams(dimension_semantics=("arbitrary",), vmem_limit_bytes=VMEM_LIMIT),
        name="attn_qkv",
    )(x2d, g.reshape(1, d), w_qkv.astype(BF16), jnp.tile(q_gain, N_HEADS).reshape(1, d),
      jnp.tile(k_gain, N_HEADS).reshape(1, d))


def _attn_kernel(q_ref, k_ref, v_ref, o_ref, acc_ref, carry_ref):
    qi = pl.program_id(2)
    tq, kb = ATT_Q, ATT_K
    q = q_ref[0]
    head_of_lane = lax.broadcasted_iota(jnp.int32, (tq, LANES), 1) // HEAD_DIM
    q_heads = [jnp.where(head_of_lane == hd, q, jnp.zeros_like(q)) for hd in range(2)]
    r2 = lax.broadcasted_iota(jnp.int32, (2 * kb, 2 * kb), 0)
    c2 = lax.broadcasted_iota(jnp.int32, (2 * kb, 2 * kb), 1)
    suffix2 = jnp.where(r2 >= c2, 1.0, 0.0).astype(BF16)

    acc_ref[...] = jnp.zeros(acc_ref.shape, F32)
    carry_ref[...] = jnp.zeros(carry_ref.shape, F32)

    def key_group(j0, diag_offsets, r0, r1):
        nb = len(diag_offsets)
        rows = r1 - r0
        start = pl.multiple_of(j0 * kb, kb)
        kk = k_ref[0, pl.ds(start, nb * kb), :]
        vv = v_ref[0, pl.ds(start, nb * kb), :]
        row = lax.broadcasted_iota(jnp.int32, (rows, kb), 0) + r0
        key = lax.broadcasted_iota(jnp.int32, (rows, kb), 1)
        z_all = [_dot_nt(q_heads[hd][r0:r1, :], kk) for hd in range(2)]
        carry = [carry_ref[hd, r0:r1, :] for hd in range(2)]
        probs = [[None] * nb for _ in range(2)]
        for pair in reversed(range(nb // 2)):
            blocks = (2 * pair, 2 * pair + 1)
            causal = [None if diag_offsets[b] is None else key + diag_offsets[b] < row for b in blocks]
            for hd in range(2):
                zs, log_keeps = [], []
                for j, b in enumerate(blocks):
                    z = z_all[hd][:, b * kb:(b + 1) * kb]
                    neg = -z
                    log_keep = jnp.minimum(neg, 0.0) - jnp.log(1.0 + jnp.exp2(jnp.minimum(z, neg))) * LOG2_E
                    if causal[j] is not None:
                        log_keep = jnp.where(causal[j], log_keep, 0.0)
                    zs.append(z)
                    log_keeps.append(log_keep.astype(BF16))
                sums = _dot(jnp.concatenate(log_keeps, axis=1), suffix2)
                for j, b in enumerate(blocks):
                    s_in = sums[:, j * kb:(j + 1) * kb]
                    a = jnp.exp2(zs[j] + s_in + carry[hd])
                    if causal[j] is not None:
                        a = jnp.where(causal[j], a, 0.0)
                    probs[hd][b] = a.astype(BF16)
                carry[hd] = carry[hd] + jnp.broadcast_to(sums[:, 0:1], (rows, kb))
        for hd in range(2):
            carry_ref[hd, r0:r1, :] = carry[hd]
            acc_ref[hd, r0:r1, :] += _dot(jnp.concatenate(probs[hd], axis=1), vv)

    blocks_per_q = tq // kb
    for g in reversed(range(blocks_per_q // ATT_DIAG_GROUP)):
        offs = [(g * ATT_DIAG_GROUP + b) * kb for b in range(ATT_DIAG_GROUP)]
        key_group(qi * blocks_per_q + g * ATT_DIAG_GROUP, offs, offs[0], tq)

    def body(s, c):
        key_group(qi * blocks_per_q - (s + 1) * ATT_GROUP, [None] * ATT_GROUP, 0, tq)
        return c
    lax.fori_loop(0, qi * (blocks_per_q // ATT_GROUP), body, 0)
    o_ref[0] = jnp.where(head_of_lane == 0, acc_ref[0], acc_ref[1]).astype(BF16)


def _attention(q, k, v, b, s):
    d = q.shape[-1]
    q3, k3, v3 = (a.reshape(b, s, d) for a in (q, k, v))
    pairs = d // LANES
    o = pl.pallas_call(
        _attn_kernel,
        grid=(b, pairs, s // ATT_Q),
        in_specs=[
            pl.BlockSpec((1, ATT_Q, LANES), lambda bi, hp, i: (bi, i, hp)),
            pl.BlockSpec((1, s, LANES), lambda bi, hp, i: (bi, 0, hp)),
            pl.BlockSpec((1, s, LANES), lambda bi, hp, i: (bi, 0, hp)),
        ],
        out_specs=pl.BlockSpec((1, ATT_Q, LANES), lambda bi, hp, i: (bi, i, hp)),
        out_shape=jax.ShapeDtypeStruct((b, s, d), BF16),
        scratch_shapes=[pltpu.VMEM((2, ATT_Q, LANES), F32), pltpu.VMEM((2, ATT_Q, LANES), F32)],
        compiler_params=pltpu.CompilerParams(
            dimension_semantics=("arbitrary", "arbitrary", "arbitrary"), vmem_limit_bytes=VMEM_LIMIT),
        name="attn_core",
    )(q3, k3, v3)
    return o.reshape(b * s, d)


def _proj_route_kernel(x_ref, o_ref, w_ref, rg_ref, wt_hi_ref, wt_lo_ref, rb_ref,
                       y_ref, hb_ref, slot_ref, gate_ref, cnt_ref):
    x2 = x_ref[...] + _dot(o_ref[...], w_ref[...])
    y_ref[...] = x2
    _route_rows(x2, rg_ref, wt_hi_ref, wt_lo_ref, rb_ref, hb_ref, slot_ref, gate_ref, cnt_ref)


def _proj_route(x2d, o, w_o, ffn_g, router_w, router_b):
    t, d = x2d.shape
    route_in, route_out = _route_specs(d, lambda c: c)
    outs = pl.pallas_call(
        _proj_route_kernel,
        grid=(t // ROUTE_TILE,),
        in_specs=[
            pl.BlockSpec((ROUTE_TILE, d), lambda c: (c, 0)),
            pl.BlockSpec((ROUTE_TILE, d), lambda c: (c, 0)),
            pl.BlockSpec((d, d), lambda c: (0, 0)),
        ] + route_in,
        out_specs=[pl.BlockSpec((ROUTE_TILE, d), lambda c: (c, 0))] + route_out,
        out_shape=[jax.ShapeDtypeStruct((t, d), F32)] + _route_shapes(t, d),
        compiler_params=pltpu.CompilerParams(dimension_semantics=("arbitrary",), vmem_limit_bytes=VMEM_LIMIT),
        name="attn_proj_route",
    )(x2d, o, w_o.astype(BF16), *_route_args(ffn_g, router_w, router_b))
    return outs[0], outs[1:]


def _attention_route_layer(x2d, b, s, g, w_qkv, q_gain, k_gain, w_o, ffn_g, router_w, router_b):
    q, k, v = _qkv(x2d, g, w_qkv, q_gain, k_gain)
    o = _attention(q, k, v, b, s)
    return _proj_route(x2d, o, w_o, ffn_g, router_w, router_b)


def kernel(x, mix_norm, pool_w, pool_scale, w_qkv, q_norm, k_norm, w_o, ffn_norm, router_w, router_b,
           w_gate_up, b_gate_up, w_down, b_down):
    b, s, d = x.shape
    assert d == D_MODEL and s % ATT_Q == 0 and s % ROUTE_TILE == 0 and ROUTE_TILE % CHUNK == 0
    depth = mix_norm.shape[0]
    for i in range(depth):
        j = i // 2
        if i % 2 == 0:
            x2d, routed = _pool_route_layer(x, mix_norm[i], pool_w[j], pool_scale[j],
                                            ffn_norm[i], router_w[i], router_b[i])
        else:
            x2d, routed = _attention_route_layer(x.reshape(b * s, d), b, s, mix_norm[i], w_qkv[j], q_norm[j],
                                                 k_norm[j], w_o[j], ffn_norm[i], router_w[i], router_b[i])
        x2d = _moe_layer(x2d, routed, i, w_gate_up, b_gate_up, w_down, b_down)
        x = x2d.reshape(b, s, d)
    return x
```
